```python
import math
import jax, jax.numpy as jnp
from jax import lax
import numpy as np

D_MODEL = 1024
BATCH = 8
SEQ = 2048
DEPTH = 2

HEAD_DIM = 64
Q_BLOCK = 128
SB_HEADS = 8
SB_W = SB_HEADS * HEAD_DIM
MLA_HEADS = 8
MLA_Q_RANK = 256
MLA_KV_RANK = 256
MLA_NOPE_DIM = 64
MLA_ROPE_DIM = 32
MLA_V_DIM = 64
MLA_W = MLA_HEADS * MLA_V_DIM
ROPE_BASE = 10000.0
IN_AB = 3 * SB_W + MLA_Q_RANK + MLA_KV_RANK + MLA_ROPE_DIM
S5_CHANNELS = 512
S5_GROUP = 16
S5_GROUPS = S5_CHANNELS // S5_GROUP
S5_STATE = 64
MOBA_HEADS = 8
MOBA_W = MOBA_HEADS * HEAD_DIM
MOBA_BLOCK = 256
MOBA_TOPK = 3
MOBA_QCHUNK = 16
IN_CD = S5_CHANNELS + 3 * MOBA_W
D_FF = -(-8 * D_MODEL // (3 * 256)) * 256
DN_ALPHA = (2 * DEPTH) ** 0.25
DN_BETA = (8 * DEPTH) ** -0.25
LN_EPS = 1e-5
RMS_EPS = 1e-6
N_EVEN = (DEPTH + 1) // 2
N_ODD = DEPTH // 2

kernel_name = 'stick_mla_s5_moba_hybrid'


def _split(h, sizes):
    out, start = [], 0
    for n in sizes:
        out.append(h[..., start:start + n])
        start += n
    return out


def _heads(t, n_heads):
    b, s, _ = t.shape
    return t.reshape(b, s, n_heads, -1).transpose(0, 2, 1, 3)


def _merge(t):
    b, h, s, dh = t.shape
    return t.transpose(0, 2, 1, 3).reshape(b, s, h * dh)


def layer_norm(x, g, b):
    xf = x.astype(jnp.float32)
    mu = xf.mean(-1, keepdims=True)
    var = jnp.square(xf - mu).mean(-1, keepdims=True)
    return ((xf - mu) * lax.rsqrt(var + LN_EPS) * g + b).astype(x.dtype)


def rms_norm(x, g):
    xf = x.astype(jnp.float32)
    return (xf * lax.rsqrt(jnp.mean(xf * xf, -1, keepdims=True) + RMS_EPS) * g).astype(x.dtype)


def rotary(x, pos):
    half = x.shape[-1] // 2
    freqs = ROPE_BASE ** (-jnp.arange(half, dtype=jnp.float32) / half)
    ang = pos.astype(jnp.float32)[:, None] * freqs
    cos, sin = jnp.cos(ang), jnp.sin(ang)
    xf = x.astype(jnp.float32)
    x1, x2 = xf[..., :half], xf[..., half:]
    return jnp.concatenate([x1 * cos - x2 * sin, x1 * sin + x2 * cos], -1).astype(x.dtype)


def stick_breaking_attention(q, k, v):
    s_len, dh = q.shape[2], q.shape[3]
    scale = dh ** -0.5
    outs = []
    for start in range(0, s_len, Q_BLOCK):
        end = start + Q_BLOCK
        z = jnp.einsum('bhqd,bhkd->bhqk', q[:, :, start:end], k[:, :, :end]).astype(jnp.float32) * scale
        past = jnp.arange(end)[None, :] < jnp.arange(start, end)[:, None]
        log_keep = jnp.where(past, jax.nn.log_sigmoid(-z), 0.0)
        later = lax.cumsum(log_keep, axis=3, reverse=True) - log_keep
        w = jnp.where(past, jnp.exp(jax.nn.log_sigmoid(z) + later), 0.0)
        outs.append(jnp.einsum('bhqk,bhkd->bhqd', w.astype(v.dtype), v[:, :, :end]))
    return jnp.concatenate(outs, axis=2)


def causal_softmax_attention(q, k, v, scale):
    s_len = q.shape[2]
    outs = []
    for start in range(0, s_len, Q_BLOCK):
        end = start + Q_BLOCK
        sc = jnp.einsum('bhqd,bhkd->bhqk', q[:, :, start:end], k[:, :, :end]).astype(jnp.float32) * scale
        causal = jnp.arange(end)[None, :] <= jnp.arange(start, end)[:, None]
        p = jax.nn.softmax(jnp.where(causal, sc, -jnp.inf), axis=-1)
        outs.append(jnp.einsum('bhqk,bhkd->bhqd', p.astype(v.dtype), v[:, :, :end]))
    return jnp.concatenate(outs, axis=2)


def moba_attention(q, k, v):
    b, h, s_len, dh = q.shape
    scale = dh ** -0.5
    nb = -(-s_len // MOBA_BLOCK)
    pad = nb * MOBA_BLOCK - s_len
    kb = jnp.pad(k, ((0, 0), (0, 0), (0, pad), (0, 0))).reshape(b, h, nb, MOBA_BLOCK, dh)
    vb = jnp.pad(v, ((0, 0), (0, 0), (0, pad), (0, 0))).reshape(b, h, nb, MOBA_BLOCK, dh)
    k_mean = kb.astype(jnp.float32).mean(axis=3)
    gate = jnp.einsum('bhsd,bhnd->bhsn', q.astype(jnp.float32), k_mean)
    q_pos = jnp.arange(s_len)
    q_blk = q_pos // MOBA_BLOCK
    own = jnp.broadcast_to(q_blk, (b, h, s_len))[..., None]
    n_top = min(MOBA_TOPK, nb - 1)
    if n_top > 0:
        fully_past = jnp.arange(nb)[None, :] < q_blk[:, None]
        _, top_idx = lax.top_k(jnp.where(fully_past, gate, -jnp.inf), n_top)
        sel = jnp.concatenate([top_idx, own], -1)
        sel_valid = jnp.concatenate([top_idx < q_blk[:, None], jnp.ones_like(own, dtype=bool)], -1)
    else:
        sel = own
        sel_valid = jnp.ones_like(own, dtype=bool)
    n_chunks = s_len // MOBA_QCHUNK

    def to_chunks(a):
        a = a.reshape(b, h, n_chunks, MOBA_QCHUNK, *a.shape[3:])
        return jnp.moveaxis(a, 2, 0)

    bi = jnp.arange(b)[:, None, None, None]
    hi = jnp.arange(h)[None, :, None, None]
    key_off = jnp.arange(MOBA_BLOCK)

    def chunk_fn(args):
        qc, selc, validc, posc = args
        kg = kb[bi, hi, selc]
        vg = vb[bi, hi, selc]
        sc = jnp.einsum('bhcd,bhcknd->bhckn', qc, kg).astype(jnp.float32) * scale
        kpos = selc[..., None] * MOBA_BLOCK + key_off
        mask = validc[..., None] & (kpos <= posc[None, None, :, None, None])
        sc = jnp.where(mask, sc, -jnp.inf)
        n_sel = sc.shape[3]
        p = jax.nn.softmax(sc.reshape(b, h, MOBA_QCHUNK, n_sel * MOBA_BLOCK), axis=-1)
        p = p.reshape(b, h, MOBA_QCHUNK, n_sel, MOBA_BLOCK)
        return jnp.einsum('bhckn,bhcknd->bhcd', p.astype(v.dtype), vg)

    out = lax.map(chunk_fn, (to_chunks(q), to_chunks(sel), to_chunks(sel_valid), q_pos.reshape(n_chunks, MOBA_QCHUNK)))
    return jnp.moveaxis(out, 0, 2).reshape(b, h, s_len, dh)


def s5_glu(u, lam_re, lam_im, log_dt, b_re, b_im, c_re, c_im, d_skip, w_glu, b_glu):
    b, s_len, _ = u.shape
    uf = u.astype(jnp.float32).reshape(b, s_len, S5_GROUPS, S5_GROUP)
    dt = jnp.exp(log_dt.astype(jnp.float32))[:, None]
    lr, li = lam_re.astype(jnp.float32), lam_im.astype(jnp.float32)
    mag = jnp.exp(lr * dt)
    ab_re, ab_im = mag * jnp.cos(li * dt), mag * jnp.sin(li * dt)
    den = lr * lr + li * li
    nr, ni = ab_re - 1.0, ab_im
    f_re, f_im = (nr * lr + ni * li) / den, (ni * lr - nr * li) / den
    br, bim = b_re.astype(jnp.float32), b_im.astype(jnp.float32)
    bb_re = f_re[..., None] * br - f_im[..., None] * bim
    bb_im = f_re[..., None] * bim + f_im[..., None] * br
    bu_re = jnp.einsum('bsgh,gph->bsgp', uf, bb_re)
    bu_im = jnp.einsum('bsgh,gph->bsgp', uf, bb_im)
    a_re = jnp.broadcast_to(ab_re, (1, s_len, S5_GROUPS, S5_STATE))
    a_im = jnp.broadcast_to(ab_im, (1, s_len, S5_GROUPS, S5_STATE))

    def combine(e1, e2):
        a1r, a1i, b1r, b1i = e1
        a2r, a2i, b2r, b2i = e2
        return (a1r * a2r - a1i * a2i, a1r * a2i + a1i * a2r,
                a2r * b1r - a2i * b1i + b2r, a2r * b1i + a2i * b1r + b2i)

    _, _, xr, xi = lax.associative_scan(combine, (a_re, a_im, bu_re, bu_im), axis=1)
    y = (jnp.einsum('bsgp,ghp->bsgh', xr, c_re.astype(jnp.float32))
         - jnp.einsum('bsgp,ghp->bsgh', xi, c_im.astype(jnp.float32))
         + d_skip.astype(jnp.float32).reshape(S5_GROUPS, S5_GROUP) * uf)
    z = jax.nn.gelu(y.reshape(b, s_len, S5_CHANNELS))
    out = z * jax.nn.sigmoid(z @ w_glu.astype(jnp.float32) + b_glu.astype(jnp.float32))
    return out.astype(u.dtype)


def even_mixer(x, w_in, q_norm_g, w_uq, kv_norm_g, w_ukv, w_out, pos):
    q_sb, k_sb, v_sb, c_q, c_kv, k_rope = _split(
        x @ w_in, (SB_W, SB_W, SB_W, MLA_Q_RANK, MLA_KV_RANK, MLA_ROPE_DIM))
    o_sb = _merge(stick_breaking_attention(_heads(q_sb, SB_HEADS), _heads(k_sb, SB_HEADS), _heads(v_sb, SB_HEADS)))
    q = _heads(rms_norm(c_q, q_norm_g) @ w_uq, MLA_HEADS)
    kv = _heads(rms_norm(c_kv, kv_norm_g) @ w_ukv, MLA_HEADS)
    q_full = jnp.concatenate([q[..., :MLA_NOPE_DIM], rotary(q[..., MLA_NOPE_DIM:], pos)], -1)
    k_nope, v = kv[..., :MLA_NOPE_DIM], kv[..., MLA_NOPE_DIM:]
    k_pe = rotary(k_rope[:, None], pos)
    k_full = jnp.concatenate([k_nope, jnp.broadcast_to(k_pe, k_nope.shape[:3] + (MLA_ROPE_DIM,))], -1)
    o_mla = _merge(causal_softmax_attention(q_full, k_full, v, (MLA_NOPE_DIM + MLA_ROPE_DIM) ** -0.5))
    return jnp.concatenate([o_sb, o_mla], -1) @ w_out


def odd_mixer(x, w_in, lam_re, lam_im, log_dt, b_re, b_im, c_re, c_im, d_skip, w_glu, b_glu, w_out):
    u, q, k, v = _split(x @ w_in, (S5_CHANNELS, MOBA_W, MOBA_W, MOBA_W))
    o_s5 = s5_glu(u, lam_re, lam_im, log_dt, b_re, b_im, c_re, c_im, d_skip, w_glu, b_glu)
    o_moba = _merge(moba_attention(_heads(q, MOBA_HEADS), _heads(k, MOBA_HEADS), _heads(v, MOBA_HEADS)))
    return jnp.concatenate([o_s5, o_moba], -1) @ w_out


def swiglu(x, w_gate, w_up, w_down):
    return (jax.nn.silu(x @ w_gate) * (x @ w_up)) @ w_down


def setup_inputs(seed: int = 0) -> dict:
    key = jax.random.key(seed)
    ks = list(jax.random.split(key, 32))

    def nrm(k, shape, scale):
        return jax.random.normal(k, shape, jnp.float32) * scale

    n_idx = jnp.arange(S5_STATE, dtype=jnp.float32)
    return {
        'x': nrm(ks[0], (BATCH, SEQ, D_MODEL), 1.0),
        'ab_w_in': nrm(ks[1], (N_EVEN, D_MODEL, IN_AB), D_MODEL ** -0.5),
        'ab_q_norm': 1.0 + nrm(ks[2], (N_EVEN, MLA_Q_RANK), 0.01),
        'ab_w_uq': nrm(ks[3], (N_EVEN, MLA_Q_RANK, MLA_HEADS * (MLA_NOPE_DIM + MLA_ROPE_DIM)), MLA_Q_RANK ** -0.5),
        'ab_kv_norm': 1.0 + nrm(ks[4], (N_EVEN, MLA_KV_RANK), 0.01),
        'ab_w_ukv': nrm(ks[5], (N_EVEN, MLA_KV_RANK, MLA_HEADS * (MLA_NOPE_DIM + MLA_V_DIM)), MLA_KV_RANK ** -0.5),
        'ab_w_out': nrm(ks[6], (N_EVEN, SB_W + MLA_W, D_MODEL), DN_BETA * (SB_W + MLA_W) ** -0.5),
        'cd_w_in': nrm(ks[7], (N_ODD, D_MODEL, IN_CD), D_MODEL ** -0.5),
        's5_lambda_re': -0.5 + nrm(ks[8], (N_ODD, S5_GROUPS, S5_STATE), 0.01),
        's5_lambda_im': jnp.pi * n_idx + nrm(ks[9], (N_ODD, S5_GROUPS, S5_STATE), 0.01),
        's5_log_dt': jax.random.uniform(ks[10], (N_ODD, S5_GROUPS), jnp.float32, math.log(1e-3), math.log(1e-1)),
        's5_b_re': nrm(ks[11], (N_ODD, S5_GROUPS, S5_STATE, S5_GROUP), (2 * S5_GROUP) ** -0.5),
        's5_b_im': nrm(ks[12], (N_ODD, S5_GROUPS, S5_STATE, S5_GROUP), (2 * S5_GROUP) ** -0.5),
        's5_c_re': nrm(ks[13], (N_ODD, S5_GROUPS, S5_GROUP, S5_STATE), S5_STATE ** -0.5),
        's5_c_im': nrm(ks[14], (N_ODD, S5_GROUPS, S5_GROUP, S5_STATE), S5_STATE ** -0.5),
        's5_d': nrm(ks[15], (N_ODD, S5_CHANNELS), 1.0),
        's5_w_glu': nrm(ks[16], (N_ODD, S5_CHANNELS, S5_CHANNELS), S5_CHANNELS ** -0.5),
        's5_b_glu': nrm(ks[17], (N_ODD, S5_CHANNELS), 0.01),
        'cd_w_out': nrm(ks[18], (N_ODD, S5_CHANNELS + MOBA_W, D_MODEL), DN_BETA * (S5_CHANNELS + MOBA_W) ** -0.5),
        'ln1_g': 1.0 + nrm(ks[19], (DEPTH, D_MODEL), 0.01),
        'ln1_b': nrm(ks[20], (DEPTH, D_MODEL), 0.01),
        'ln2_g': 1.0 + nrm(ks[21], (DEPTH, D_MODEL), 0.01),
        'ln2_b': nrm(ks[22], (DEPTH, D_MODEL), 0.01),
        'ffn_w_gate': nrm(ks[23], (DEPTH, D_MODEL, D_FF), D_MODEL ** -0.5),
        'ffn_w_up': nrm(ks[24], (DEPTH, D_MODEL, D_FF), D_MODEL ** -0.5),
        'ffn_w_down': nrm(ks[25], (DEPTH, D_FF, D_MODEL), DN_BETA * D_FF ** -0.5),
    }


def reference(x, ab_w_in, ab_q_norm, ab_w_uq, ab_kv_norm, ab_w_ukv, ab_w_out,
              cd_w_in, s5_lambda_re, s5_lambda_im, s5_log_dt, s5_b_re, s5_b_im, s5_c_re, s5_c_im,
              s5_d, s5_w_glu, s5_b_glu, cd_w_out,
              ln1_g, ln1_b, ln2_g, ln2_b, ffn_w_gate, ffn_w_up, ffn_w_down):
    pos = jnp.arange(x.shape[1])
    for layer in range(DEPTH):
        i = layer // 2
        if layer % 2 == 0:
            mix = even_mixer(x, ab_w_in[i], ab_q_norm[i], ab_w_uq[i], ab_kv_norm[i], ab_w_ukv[i], ab_w_out[i], pos)
        else:
            mix = odd_mixer(x, cd_w_in[i], s5_lambda_re[i], s5_lambda_im[i], s5_log_dt[i], s5_b_re[i], s5_b_im[i],
                            s5_c_re[i], s5_c_im[i], s5_d[i], s5_w_glu[i], s5_b_glu[i], cd_w_out[i])
        x = layer_norm(DN_ALPHA * x + mix, ln1_g[layer], ln1_b[layer])
        x = layer_norm(DN_ALPHA * x + swiglu(x, ffn_w_gate[layer], ffn_w_up[layer], ffn_w_down[layer]),
                       ln2_g[layer], ln2_b[layer])
    return x
```

```python
import functools
import math

import jax
import jax.numpy as jnp
from jax import lax
from jax.experimental import pallas as pl
from jax.experimental.pallas import tpu as pltpu

F32 = jnp.float32
BF16 = jnp.bfloat16

D_MODEL = 1024
HEAD_DIM = 64
N_HEADS = 8
HEADS_W = N_HEADS * HEAD_DIM
MLA_RANK = 256
MLA_NOPE = 64
MLA_ROPE = 32
ROPE_BASE = 10000.0
S5_CHANNELS = 512
S5_GROUP = 16
S5_GROUPS = 32
S5_STATE = 64
S5_WIDTH = S5_GROUPS * S5_STATE
MOBA_BLOCK = 256
MOBA_TOPK = 3
D_FF = 2816
DEPTH = 2
DN_ALPHA = (2 * DEPTH) ** 0.25
LN_EPS = 1e-5
RMS_EPS = 1e-6

LANES = 128
NEG_BIG = -1e30
VMEM_LIMIT = 56 * 1024 * 1024

ROW_TILE = 512
ATT_TILE = 256
S5_TIME_TILE = 64


def _cparams(*sem):
    return pltpu.CompilerParams(dimension_semantics=sem, vmem_limit_bytes=VMEM_LIMIT)


def _full_spec(shape):
    nd = len(shape)
    return pl.BlockSpec(shape, lambda *_: (0,) * nd)


def _dot(a, b):
    return jnp.dot(a, b, preferred_element_type=F32)


def _dot_nt(a, b):
    return lax.dot_general(a, b, (((1,), (1,)), ((), ())), preferred_element_type=F32)


def _layer_norm_rows(r, g, b):
    mu = jnp.mean(r, axis=-1, keepdims=True)
    d = r - mu
    var = jnp.mean(d * d, axis=-1, keepdims=True)
    return d * lax.rsqrt(var + LN_EPS) * g + b


def _proj_ab_kernel(x_ref, w1_ref, gq_ref, gkv_ref, wq_ref, wqr_ref, wkn_ref, wv_ref,
                    cosq_ref, sinq_ref, cosk_ref, sink_ref,
                    qsb_ref, ksb_ref, vsb_ref, qf_ref, kf_ref, vm_ref):
    xb = x_ref[...].astype(BF16)
    p = _dot(xb, w1_ref[...])
    qsb_ref[...] = (p[:, 0:512] * (HEAD_DIM ** -0.5)).astype(BF16)
    ksb_ref[...] = p[:, 512:1024].astype(BF16)
    vsb_ref[...] = p[:, 1024:1536].astype(BF16)
    cq = p[:, 1536:1792]
    ckv = p[:, 1792:2048]
    rope = p[:, 2048:2176]
    rope_rot = p[:, 2176:2304]

    cqn = cq * lax.rsqrt(jnp.mean(cq * cq, axis=-1, keepdims=True) + RMS_EPS) * gq_ref[...]
    ckvn = ckv * lax.rsqrt(jnp.mean(ckv * ckv, axis=-1, keepdims=True) + RMS_EPS) * gkv_ref[...]
    cqb = cqn.astype(BF16)
    ckvb = ckvn.astype(BF16)

    cosq = jnp.concatenate([cosq_ref[...]] * N_HEADS, axis=1)
    sinq = jnp.concatenate([sinq_ref[...]] * N_HEADS, axis=1)
    qf = _dot(cqb, wq_ref[...]) * cosq + _dot(cqb, wqr_ref[...]) * sinq
    qf_ref[...] = qf.astype(BF16)

    kpe = rope * cosk_ref[...] + rope_rot * sink_ref[...]
    kf = _dot(ckvb, wkn_ref[...]) + jnp.concatenate([kpe] * N_HEADS, axis=1)
    kf_ref[...] = kf.astype(BF16)
    vm_ref[...] = _dot(ckvb, wv_ref[...]).astype(BF16)


def _proj_ab(x2, w1, gq, gkv, wq, wqr, wkn, wv, cosq, sinq, cosk, sink, seq):
    t = x2.shape[0]
    tm = min(ROW_TILE, seq)
    n_seq_tiles = seq // tm
    row = lambda c: pl.BlockSpec((tm, c), lambda i: (i, 0))
    tab = pl.BlockSpec((tm, LANES), lambda i: (i % n_seq_tiles, 0))
    out_w = (HEADS_W, HEADS_W, HEADS_W, N_HEADS * LANES, N_HEADS * LANES, HEADS_W)
    return pl.pallas_call(
        _proj_ab_kernel,
        grid=(t // tm,),
        in_specs=[row(D_MODEL), _full_spec(w1.shape), _full_spec(gq.shape), _full_spec(gkv.shape),
                  _full_spec(wq.shape), _full_spec(wqr.shape), _full_spec(wkn.shape), _full_spec(wv.shape),
                  tab, tab, tab, tab],
        out_specs=[row(c) for c in out_w],
        out_shape=[jax.ShapeDtypeStruct((t, c), BF16) for c in out_w],
        compiler_params=_cparams("parallel"),
        name="proj_ab",
    )(x2, w1, gq, gkv, wq, wqr, wkn, wv, cosq, sinq, cosk, sink)


def _sb_attn_kernel(q_ref, k_ref, v_ref, o_ref):
    i = pl.program_id(2)
    tq = q_ref.shape[0]
    tk = tq
    q2 = q_ref[...]
    lane = lax.broadcasted_iota(jnp.int32, (1, LANES), 1)
    first = lane < HEAD_DIM
    zero = jnp.zeros_like(q2)
    q_heads = (jnp.where(first, q2, zero), jnp.where(first, zero, q2))

    r = lax.broadcasted_iota(jnp.int32, (tk, tk), 0)
    c = lax.broadcasted_iota(jnp.int32, (tk, tk), 1)
    suffix = jnp.where(r > c, 1.0, 0.0).astype(BF16)
    past = c < r

    def tile(j, carry, diagonal):
        start = pl.multiple_of(j * tk, tk)
        k2 = k_ref[pl.ds(start, tk), :]
        v2 = v_ref[pl.ds(start, tk), :]
        out = []
        for h in range(2):
            acc, run = carry[h]
            z = _dot_nt(q_heads[h], k2)
            soft = jnp.log1p(jnp.exp(-jnp.abs(z)))
            log_beta = jnp.minimum(z, 0.0) - soft
            log_keep = -jnp.maximum(z, 0.0) - soft
            if diagonal:
                log_keep = jnp.where(past, log_keep, 0.0)
            hi = log_keep.astype(BF16)
            lo = (log_keep - hi.astype(F32)).astype(BF16)
            later = _dot(hi, suffix) + _dot(lo, suffix)
            w = jnp.exp(log_beta + later + run)
            if diagonal:
                w = jnp.where(past, w, 0.0)
            acc = acc + _dot(w.astype(BF16), v2)
            run = run + jnp.sum(log_keep, axis=-1, keepdims=True)
            out.append((acc, run))
        return tuple(out)

    init = tuple((jnp.zeros((tq, LANES), F32), jnp.zeros((tq, 1), F32)) for _ in range(2))
    carry = tile(i, init, True)
    carry = lax.fori_loop(0, i, lambda jj, cr: tile(i - 1 - jj, cr, False), carry)
    o_ref[...] = jnp.where(first, carry[0][0], carry[1][0]).astype(o_ref.dtype)


def _sb_attention(q, k, v):
    b, s, _ = q.shape
    tq = min(ATT_TILE, s)
    qspec = pl.BlockSpec((None, tq, LANES), lambda bi, h, i: (bi, i, h))
    kspec = pl.BlockSpec((None, s, LANES), lambda bi, h, i: (bi, 0, h))
    return pl.pallas_call(
        _sb_attn_kernel,
        grid=(b, N_HEADS // 2, s // tq),
        in_specs=[qspec, kspec, kspec],
        out_specs=qspec,
        out_shape=jax.ShapeDtypeStruct((b, s, HEADS_W), BF16),
        compiler_params=_cparams("parallel", "parallel", "arbitrary"),
        name="sb_attn",
    )(q, k, v)


def _mla_attn_kernel(q_ref, k_ref, v_ref, o_ref):
    i = pl.program_id(2)
    tq = q_ref.shape[0]
    tk = tq
    q_heads = (q_ref[:, 0:LANES], q_ref[:, LANES:2 * LANES])
    lane = lax.broadcasted_iota(jnp.int32, (1, LANES), 1)
    first = lane < HEAD_DIM
    r = lax.broadcasted_iota(jnp.int32, (tq, tk), 0)
    c = lax.broadcasted_iota(jnp.int32, (tq, tk), 1)
    causal = c <= r

    def tile(j, carry, diagonal):
        start = pl.multiple_of(j * tk, tk)
        v2 = v_ref[pl.ds(start, tk), :]
        out = []
        for h in range(2):
            m, l, acc = carry[h]
            kh = k_ref[pl.ds(start, tk), h * LANES:(h + 1) * LANES]
            s = _dot_nt(q_heads[h], kh)
            if diagonal:
                s = jnp.where(causal, s, NEG_BIG)
            m_new = jnp.maximum(m, jnp.max(s, axis=-1, keepdims=True))
            p = jnp.exp(s - m_new)
            alpha = jnp.exp(m - m_new)
            l = alpha * l + jnp.sum(p, axis=-1, keepdims=True)
            acc = alpha * acc + _dot(p.astype(BF16), v2)
            out.append((m_new, l, acc))
        return tuple(out)

    init = tuple((jnp.full((tq, 1), NEG_BIG, F32), jnp.zeros((tq, 1), F32), jnp.zeros((tq, LANES), F32))
                 for _ in range(2))
    carry = tile(i, init, True)
    carry = lax.fori_loop(0, i, lambda j, cr: tile(j, cr, False), carry)
    o0 = carry[0][2] / carry[0][1]
    o1 = carry[1][2] / carry[1][1]
    o_ref[...] = jnp.where(first, o0, o1).astype(o_ref.dtype)


def _mla_attention(qf, kf, v):
    b, s, _ = v.shape
    tq = min(ATT_TILE, s)
    qspec = pl.BlockSpec((None, tq, 2 * LANES), lambda bi, h, i: (bi, i, h))
    kspec = pl.BlockSpec((None, s, 2 * LANES), lambda bi, h, i: (bi, 0, h))
    vspec = pl.BlockSpec((None, s, LANES), lambda bi, h, i: (bi, 0, h))
    ospec = pl.BlockSpec((None, tq, LANES), lambda bi, h, i: (bi, i, h))
    return pl.pallas_call(
        _mla_attn_kernel,
        grid=(b, N_HEADS // 2, s // tq),
        in_specs=[qspec, kspec, vspec],
        out_specs=ospec,
        out_shape=jax.ShapeDtypeStruct((b, s, HEADS_W), BF16),
        compiler_params=_cparams("parallel", "parallel", "arbitrary"),
        name="mla_attn",
    )(qf, kf, v)


def _out_ln_kernel(a_ref, b_ref, x_ref, wa_ref, wb_ref, g_ref, beta_ref, o_ref):
    mix = _dot(a_ref[...], wa_ref[...]) + _dot(b_ref[...], wb_ref[...])
    o_ref[...] = _layer_norm_rows(DN_ALPHA * x_ref[...] + mix, g_ref[...], beta_ref[...])


def _out_ln(a, b, x2, wa, wb, g, beta):
    t = x2.shape[0]
    tm = min(ROW_TILE, t)
    row = lambda c: pl.BlockSpec((tm, c), lambda i: (i, 0))
    return pl.pallas_call(
        _out_ln_kernel,
        grid=(t // tm,),
        in_specs=[row(a.shape[1]), row(b.shape[1]), row(D_MODEL), _full_spec(wa.shape), _full_spec(wb.shape),
                  _full_spec(g.shape), _full_spec(beta.shape)],
        out_specs=row(D_MODEL),
        out_shape=jax.ShapeDtypeStruct((t, D_MODEL), F32),
        compiler_params=_cparams("parallel"),
        name="out_ln",
    )(a, b, x2, wa, wb, g, beta)


def _ffn_ln_kernel(x_ref, wg_ref, wu_ref, wd_ref, g_ref, beta_ref, o_ref):
    x = x_ref[...]
    xb = x.astype(BF16)
    gate = _dot(xb, wg_ref[...])
    up = _dot(xb, wu_ref[...])
    h = (gate * (1.0 / (1.0 + jnp.exp(-gate))) * up).astype(BF16)
    y = _dot(h, wd_ref[...])
    o_ref[...] = _layer_norm_rows(DN_ALPHA * x + y, g_ref[...], beta_ref[...])


def _ffn_ln(x2, wg, wu, wd, g, beta):
    t = x2.shape[0]
    tm = min(ROW_TILE, t)
    row = pl.BlockSpec((tm, D_MODEL), lambda i: (i, 0))
    resident = lambda shape: pl.BlockSpec(shape, lambda i: (0, 0), pipeline_mode=pl.Buffered(1))
    return pl.pallas_call(
        _ffn_ln_kernel,
        grid=(t // tm,),
        in_specs=[row, resident(wg.shape), resident(wu.shape), resident(wd.shape),
                  _full_spec(g.shape), _full_spec(beta.shape)],
        out_specs=row,
        out_shape=jax.ShapeDtypeStruct((t, D_MODEL), F32),
        compiler_params=_cparams("parallel"),
        name="ffn_ln",
    )(x2, wg, wu, wd, g, beta)


def _proj_cd_kernel(x_ref, w_ref, u_ref, q_ref, k_ref, v_ref):
    p = _dot(x_ref[...].astype(BF16), w_ref[...])
    u_ref[...] = p[:, 0:512].astype(BF16)
    q_ref[...] = (p[:, 512:1024] * (HEAD_DIM ** -0.5)).astype(BF16)
    k_ref[...] = p[:, 1024:1536].astype(BF16)
    v_ref[...] = p[:, 1536:2048].astype(BF16)


def _proj_cd(x2, w):
    t = x2.shape[0]
    tm = min(ROW_TILE, t)
    row = lambda c: pl.BlockSpec((tm, c), lambda i: (i, 0))
    return pl.pallas_call(
        _proj_cd_kernel,
        grid=(t // tm,),
        in_specs=[row(D_MODEL), _full_spec(w.shape)],
        out_specs=[row(HEADS_W)] * 4,
        out_shape=[jax.ShapeDtypeStruct((t, HEADS_W), BF16)] * 4,
        compiler_params=_cparams("parallel"),
        name="proj_cd",
    )(x2, w)


S5_SCAN_LANES = 1024


def _s5_kernel(u_ref, bmat_ref, cmat_ref, are_ref, aim_ref, d_ref, wglu_ref, bglu_ref, o_ref,
               bu_ref, state_ref):
    nb, ts, _ = u_ref.shape
    rows = ts * nb

    @pl.when(pl.program_id(0) == 0)
    def _():
        state_ref[...] = jnp.zeros_like(state_ref)

    uf = jnp.swapaxes(u_ref[...].astype(F32), 0, 1).reshape(rows, S5_CHANNELS)
    ub = uf.astype(BF16)
    n_chunks = S5_CHANNELS // LANES
    cw = S5_WIDTH // n_chunks
    for j in range(n_chunks):
        bu = _dot(ub[:, j * LANES:(j + 1) * LANES], bmat_ref[j])
        bu_ref[:, j * cw:(j + 1) * cw] = bu[:, :cw]
        bu_ref[:, S5_WIDTH + j * cw:S5_WIDTH + (j + 1) * cw] = bu[:, cw:]

    for w in range(S5_WIDTH // S5_SCAN_LANES):
        re_cols = slice(w * S5_SCAN_LANES, (w + 1) * S5_SCAN_LANES)
        im_cols = slice(S5_WIDTH + w * S5_SCAN_LANES, S5_WIDTH + (w + 1) * S5_SCAN_LANES)
        ar = jnp.broadcast_to(are_ref[:, re_cols], (nb, S5_SCAN_LANES))
        ai = jnp.broadcast_to(aim_ref[:, re_cols], (nb, S5_SCAN_LANES))

        def step(t, carry):
            xr, xi = carry
            r0 = pl.multiple_of(t * nb, nb)
            br = bu_ref[pl.ds(r0, nb), re_cols]
            bi = bu_ref[pl.ds(r0, nb), im_cols]
            nr = ar * xr - ai * xi + br
            ni = ar * xi + ai * xr + bi
            bu_ref[pl.ds(r0, nb), re_cols] = nr
            bu_ref[pl.ds(r0, nb), im_cols] = ni
            return nr, ni

        xr, xi = lax.fori_loop(0, ts, step, (state_ref[:, re_cols], state_ref[:, im_cols]), unroll=8)
        state_ref[:, re_cols] = xr
        state_ref[:, im_cols] = xi

    ys = []
    for j in range(n_chunks):
        xin = jnp.concatenate([bu_ref[:, j * cw:(j + 1) * cw],
                               bu_ref[:, S5_WIDTH + j * cw:S5_WIDTH + (j + 1) * cw]], axis=1).astype(BF16)
        ys.append(_dot(xin, cmat_ref[j]))
    y = jnp.concatenate(ys, axis=1) + d_ref[...] * uf
    z = 0.5 * y * (1.0 + jnp.tanh(math.sqrt(2.0 / math.pi) * (y + 0.044715 * (y * y * y))))
    gate = _dot(z.astype(BF16), wglu_ref[...]) + bglu_ref[...]
    out = z * (1.0 / (1.0 + jnp.exp(-gate)))
    o_ref[...] = jnp.swapaxes(out.reshape(ts, nb, S5_CHANNELS), 0, 1).astype(o_ref.dtype)


def _s5(u, bmat, cmat, are, aim, d, wglu, bglu):
    b, s, _ = u.shape
    ts = min(S5_TIME_TILE, s)
    blk = pl.BlockSpec((b, ts, S5_CHANNELS), lambda i: (0, i, 0))
    return pl.pallas_call(
        _s5_kernel,
        grid=(s // ts,),
        in_specs=[blk, _full_spec(bmat.shape), _full_spec(cmat.shape), _full_spec(are.shape),
                  _full_spec(aim.shape), _full_spec(d.shape), _full_spec(wglu.shape), _full_spec(bglu.shape)],
        out_specs=blk,
        out_shape=jax.ShapeDtypeStruct((b, s, S5_CHANNELS), BF16),
        scratch_shapes=[pltpu.VMEM((ts * b, 2 * S5_WIDTH), F32), pltpu.VMEM((b, 2 * S5_WIDTH), F32)],
        compiler_params=_cparams("arbitrary"),
        name="s5_scan",
    )(u, bmat, cmat, are, aim, d, wglu, bglu)


def _moba_kernel(q_ref, k_ref, v_ref, o_ref, kmean_ref, sel_ref):
    i = pl.program_id(2)
    tq = q_ref.shape[0]
    nblk = k_ref.shape[0] // MOBA_BLOCK
    lane = lax.broadcasted_iota(jnp.int32, (1, LANES), 1)
    first = lane < HEAD_DIM

    @pl.when(i == 0)
    def _():
        kmean_ref[...] = jnp.zeros_like(kmean_ref)
        for n in range(nblk):
            kb = k_ref[n * MOBA_BLOCK:(n + 1) * MOBA_BLOCK, :].astype(F32)
            kmean_ref[n:n + 1, :] = jnp.mean(kb, axis=0, keepdims=True)

    q2 = q_ref[...]
    zero = jnp.zeros_like(q2)
    q_heads = (jnp.where(first, q2, zero), jnp.where(first, zero, q2))

    km = kmean_ref[...]
    k1 = km.astype(BF16)
    rem = km - k1.astype(F32)
    k2 = rem.astype(BF16)
    k3 = (rem - k2.astype(F32)).astype(BF16)
    rowid = lax.broadcasted_iota(jnp.int32, (8, tq), 0)
    valid = rowid < i
    for h in range(2):
        qh = q_heads[h]
        gate_t = _dot_nt(k1, qh) + _dot_nt(k2, qh) + _dot_nt(k3, qh)
        g = jnp.where(valid, gate_t[0:8, :], -jnp.inf)
        beaten = jnp.zeros((8, tq), jnp.int32)
        for m in range(min(nblk, 8)):
            gm = g[m:m + 1, :]
            wins = (gm > g) | ((gm == g) & (rowid > m))
            beaten = beaten + jnp.where(wins, 1, 0)
        chosen = jnp.where(valid & (beaten < MOBA_TOPK), 1.0, 0.0)
        chosen = jnp.concatenate([chosen, jnp.zeros((LANES - 8, tq), F32)], axis=0)
        sel_ref[h] = chosen.T

    r = lax.broadcasted_iota(jnp.int32, (tq, tq), 0)
    c = lax.broadcasted_iota(jnp.int32, (tq, tq), 1)
    causal = c <= r

    def tile(n, carry, own):
        start = pl.multiple_of(n * MOBA_BLOCK, MOBA_BLOCK)
        k2b = k_ref[pl.ds(start, MOBA_BLOCK), :]
        v2b = v_ref[pl.ds(start, MOBA_BLOCK), :]
        out = []
        for h in range(2):
            m, l, acc = carry[h]
            s = _dot_nt(q_heads[h], k2b)
            if own:
                s = jnp.where(causal, s, NEG_BIG)
            else:
                picked = jnp.max(jnp.where(lane == n, sel_ref[h], 0.0), axis=-1, keepdims=True) > 0.5
                s = jnp.where(picked, s, NEG_BIG)
            m_new = jnp.maximum(m, jnp.max(s, axis=-1, keepdims=True))
            p = jnp.exp(s - m_new)
            alpha = jnp.exp(m - m_new)
            l = alpha * l + jnp.sum(p, axis=-1, keepdims=True)
            acc = alpha * acc + _dot(p.astype(BF16), v2b)
            out.append((m_new, l, acc))
        return tuple(out)

    init = tuple((jnp.full((tq, 1), NEG_BIG, F32), jnp.zeros((tq, 1), F32), jnp.zeros((tq, LANES), F32))
                 for _ in range(2))
    carry = tile(i, init, True)
    carry = lax.fori_loop(0, i, lambda n, cr: tile(n, cr, False), carry)
    o0 = carry[0][2] / carry[0][1]
    o1 = carry[1][2] / carry[1][1]
    o_ref[...] = jnp.where(first, o0, o1).astype(o_ref.dtype)


def _moba_attention(q, k, v):
    b, s, _ = q.shape
    assert s % MOBA_BLOCK == 0 and s // MOBA_BLOCK <= 8
    tq = MOBA_BLOCK
    qspec = pl.BlockSpec((None, tq, LANES), lambda bi, h, i: (bi, i, h))
    kspec = pl.BlockSpec((None, s, LANES), lambda bi, h, i: (bi, 0, h))
    return pl.pallas_call(
        _moba_kernel,
        grid=(b, N_HEADS // 2, s // tq),
        in_specs=[qspec, kspec, kspec],
        out_specs=qspec,
        out_shape=jax.ShapeDtypeStruct((b, s, HEADS_W), BF16),
        scratch_shapes=[pltpu.VMEM((LANES, LANES), F32), pltpu.VMEM((2, tq, LANES), F32)],
        compiler_params=_cparams("parallel", "parallel", "arbitrary"),
        name="moba_attn",
    )(q, k, v)


def _rotate_half_cols(w):
    half = w.shape[-1] // 2
    return jnp.concatenate([-w[..., half:], w[..., :half]], axis=-1)


def _layer0_params(w_in, q_norm, w_uq, kv_norm, w_ukv, seq):
    d = w_in.shape[0]
    w_rope = w_in[:, 2048:2080]
    pad_l = jnp.zeros((d, MLA_NOPE), F32)
    pad_r = jnp.zeros((d, LANES - MLA_NOPE - MLA_ROPE), F32)
    w1 = jnp.concatenate([w_in[:, :2048], pad_l, w_rope, pad_r, pad_l, _rotate_half_cols(w_rope), pad_r], axis=1)

    wq3 = w_uq.reshape(MLA_RANK, N_HEADS, MLA_NOPE + MLA_ROPE)
    nope, rope = wq3[..., :MLA_NOPE], wq3[..., MLA_NOPE:]
    z_nope = jnp.zeros_like(nope)
    z_pad = jnp.zeros((MLA_RANK, N_HEADS, LANES - MLA_NOPE - MLA_ROPE), F32)
    wq = jnp.concatenate([nope, rope, z_pad], axis=-1).reshape(MLA_RANK, N_HEADS * LANES)
    wqr = jnp.concatenate([z_nope, _rotate_half_cols(rope), z_pad], axis=-1).reshape(MLA_RANK, N_HEADS * LANES)

    wkv3 = w_ukv.reshape(MLA_RANK, N_HEADS, 2 * HEAD_DIM)
    wkn = jnp.concatenate([wkv3[..., :MLA_NOPE], jnp.zeros((MLA_RANK, N_HEADS, LANES - MLA_NOPE), F32)],
                          axis=-1).reshape(MLA_RANK, N_HEADS * LANES)
    wv = wkv3[..., MLA_NOPE:].reshape(MLA_RANK, HEADS_W)

    half = MLA_ROPE // 2
    freqs = ROPE_BASE ** (-jnp.arange(half, dtype=F32) / half)
    ang = jnp.arange(seq, dtype=F32)[:, None] * freqs
    cos = jnp.concatenate([jnp.cos(ang)] * 2, axis=1)
    sin = jnp.concatenate([jnp.sin(ang)] * 2, axis=1)
    ones = jnp.ones((seq, MLA_NOPE), F32)
    zl = jnp.zeros((seq, MLA_NOPE), F32)
    zr = jnp.zeros((seq, LANES - MLA_NOPE - MLA_ROPE), F32)
    scale = (MLA_NOPE + MLA_ROPE) ** -0.5
    cosq = jnp.concatenate([ones, cos, zr], axis=1) * scale
    sinq = jnp.concatenate([zl, sin, zr], axis=1) * scale
    cosk = jnp.concatenate([zl, cos, zr], axis=1)
    sink = jnp.concatenate([zl, sin, zr], axis=1)
    return (w1.astype(BF16), q_norm.reshape(1, -1), kv_norm.reshape(1, -1), wq.astype(BF16), wqr.astype(BF16),
            wkn.astype(BF16), wv.astype(BF16), cosq, sinq, cosk, sink)


def _s5_params(lam_re, lam_im, log_dt, b_re, b_im, c_re, c_im):
    dt = jnp.exp(log_dt)[:, None]
    mag = jnp.exp(lam_re * dt)
    ab_re, ab_im = mag * jnp.cos(lam_im * dt), mag * jnp.sin(lam_im * dt)
    den = lam_re * lam_re + lam_im * lam_im
    nr, ni = ab_re - 1.0, ab_im
    f_re, f_im = (nr * lam_re + ni * lam_im) / den, (ni * lam_re - nr * lam_im) / den
    bb_re = f_re[..., None] * b_re - f_im[..., None] * b_im
    bb_im = f_re[..., None] * b_im + f_im[..., None] * b_re

    gpc = LANES // S5_GROUP
    n_chunks = S5_GROUPS // gpc
    eye = jnp.eye(gpc, dtype=F32)

    def in_blocks(bb):
        t = bb.reshape(n_chunks, gpc, S5_STATE, S5_GROUP)
        return jnp.einsum('cgph,gk->cghkp', t, eye).reshape(n_chunks, gpc * S5_GROUP, gpc * S5_STATE)

    def out_blocks(cc):
        t = cc.reshape(n_chunks, gpc, S5_GROUP, S5_STATE)
        return jnp.einsum('cghp,gk->cgpkh', t, eye).reshape(n_chunks, gpc * S5_STATE, gpc * S5_GROUP)

    bmat = jnp.concatenate([in_blocks(bb_re), in_blocks(bb_im)], axis=2)
    cmat = jnp.concatenate([out_blocks(c_re), -out_blocks(c_im)], axis=1)
    a_re = ab_re.reshape(1, S5_WIDTH)
    a_im = ab_im.reshape(1, S5_WIDTH)
    return bmat.astype(BF16), cmat.astype(BF16), a_re, a_im


def kernel(x, ab_w_in, ab_q_norm, ab_w_uq, ab_kv_norm, ab_w_ukv, ab_w_out, cd_w_in, s5_lambda_re, s5_lambda_im,
           s5_log_dt, s5_b_re, s5_b_im, s5_c_re, s5_c_im, s5_d, s5_w_glu, s5_b_glu, cd_w_out, ln1_g, ln1_b, ln2_g,
           ln2_b, ffn_w_gate, ffn_w_up, ffn_w_down):
    b, s, d = x.shape
    t = b * s
    x2 = x.reshape(t, d)
    vec = lambda a: a.reshape(1, -1)

    p0 = _layer0_params(ab_w_in[0], ab_q_norm[0], ab_w_uq[0], ab_kv_norm[0], ab_w_ukv[0], s)
    qsb, ksb, vsb, qf, kf, vm = _proj_ab(x2, *p0, seq=s)
    sh = lambda a: a.reshape(b, s, a.shape[-1])
    o_sb = _sb_attention(sh(qsb), sh(ksb), sh(vsb)).reshape(t, HEADS_W)
    o_mla = _mla_attention(sh(qf), sh(kf), sh(vm)).reshape(t, HEADS_W)
    w_out = ab_w_out[0].astype(BF16)
    x2 = _out_ln(o_sb, o_mla, x2, w_out[:HEADS_W], w_out[HEADS_W:], vec(ln1_g[0]), vec(ln1_b[0]))
    x2 = _ffn_ln(x2, ffn_w_gate[0].astype(BF16), ffn_w_up[0].astype(BF16), ffn_w_down[0].astype(BF16),
                 vec(ln2_g[0]), vec(ln2_b[0]))

    u, q, k, v = _proj_cd(x2, cd_w_in[0].astype(BF16))
    bmat, cmat, a_re, a_im = _s5_params(s5_lambda_re[0], s5_lambda_im[0], s5_log_dt[0], s5_b_re[0], s5_b_im[0],
                                        s5_c_re[0], s5_c_im[0])
    o_s5 = _s5(sh(u), bmat, cmat, a_re, a_im, vec(s5_d[0]), s5_w_glu[0].astype(BF16), vec(s5_b_glu[0]))
    o_moba = _moba_attention(sh(q), sh(k), sh(v))
    w_out = cd_w_out[0].astype(BF16)
    x2 = _out_ln(o_s5.reshape(t, S5_CHANNELS), o_moba.reshape(t, HEADS_W), x2, w_out[:S5_CHANNELS],
                 w_out[S5_CHANNELS:], vec(ln1_g[1]), vec(ln1_b[1]))
    x2 = _ffn_ln(x2, ffn_w_gate[1].astype(BF16), ffn_w_up[1].astype(BF16), ffn_w_down[1].astype(BF16),
                 vec(ln2_g[1]), vec(ln2_b[1]))
    return x2.reshape(b, s, d)
```

```python
import functools
import math

import jax
import jax.numpy as jnp
from jax import lax
from jax.experimental import pallas as pl
from jax.experimental.pallas import tpu as pltpu

F32 = jnp.float32
BF16 = jnp.bfloat16

D_MODEL = 1024
HEAD_DIM = 64
N_HEADS = 8
HEADS_W = N_HEADS * HEAD_DIM
MLA_RANK = 256
MLA_NOPE = 64
MLA_ROPE = 32
ROPE_BASE = 10000.0
S5_CHANNELS = 512
S5_GROUP = 16
S5_GROUPS = 32
S5_STATE = 64
S5_WIDTH = S5_GROUPS * S5_STATE
MOBA_BLOCK = 256
MOBA_TOPK = 3
D_FF = 2816
DEPTH = 2
DN_ALPHA = (2 * DEPTH) ** 0.25
LN_EPS = 1e-5
RMS_EPS = 1e-6

LANES = 128
NEG_BIG = -1e30
VMEM_LIMIT = 56 * 1024 * 1024

ROW_TILE = 512
ATT_TILE = 512
SB_GROUP = 256
S5_TIME_TILE = 64


def _cparams(*sem):
    return pltpu.CompilerParams(dimension_semantics=sem, vmem_limit_bytes=VMEM_LIMIT)


def _full_spec(shape):
    nd = len(shape)
    return pl.BlockSpec(shape, lambda *_: (0,) * nd)


def _dot(a, b):
    return jnp.dot(a, b, preferred_element_type=F32)


def _dot_nt(a, b):
    return lax.dot_general(a, b, (((1,), (1,)), ((), ())), preferred_element_type=F32)


def _layer_norm_rows(r, g, b):
    mu = jnp.mean(r, axis=-1, keepdims=True)
    d = r - mu
    var = jnp.mean(d * d, axis=-1, keepdims=True)
    return d * lax.rsqrt(var + LN_EPS) * g + b


def _proj_ab_kernel(x_ref, w1_ref, gq_ref, gkv_ref, wq_ref, wqr_ref, wkn_ref, wv_ref,
                    cosq_ref, sinq_ref, cosk_ref, sink_ref,
                    qsb_ref, ksb_ref, vsb_ref, qf_ref, kf_ref, vm_ref):
    xb = x_ref[...].astype(BF16)
    p = _dot(xb, w1_ref[...])
    qsb_ref[...] = (p[:, 0:512] * (HEAD_DIM ** -0.5)).astype(BF16)
    ksb_ref[...] = p[:, 512:1024].astype(BF16)
    vsb_ref[...] = p[:, 1024:1536].astype(BF16)
    cq = p[:, 1536:1792]
    ckv = p[:, 1792:2048]
    rope = p[:, 2048:2176]
    rope_rot = p[:, 2176:2304]

    cqn = cq * lax.rsqrt(jnp.mean(cq * cq, axis=-1, keepdims=True) + RMS_EPS) * gq_ref[...]
    ckvn = ckv * lax.rsqrt(jnp.mean(ckv * ckv, axis=-1, keepdims=True) + RMS_EPS) * gkv_ref[...]
    cqb = cqn.astype(BF16)
    ckvb = ckvn.astype(BF16)

    cosq = jnp.concatenate([cosq_ref[...]] * N_HEADS, axis=1)
    sinq = jnp.concatenate([sinq_ref[...]] * N_HEADS, axis=1)
    qf = _dot(cqb, wq_ref[...]) * cosq + _dot(cqb, wqr_ref[...]) * sinq
    qf_ref[...] = qf.astype(BF16)

    kpe = rope * cosk_ref[...] + rope_rot * sink_ref[...]
    kf = _dot(ckvb, wkn_ref[...]) + jnp.concatenate([kpe] * N_HEADS, axis=1)
    kf_ref[...] = kf.astype(BF16)
    vm_ref[...] = _dot(ckvb, wv_ref[...]).astype(BF16)


def _proj_ab(x2, w1, gq, gkv, wq, wqr, wkn, wv, cosq, sinq, cosk, sink, seq):
    t = x2.shape[0]
    tm = min(ROW_TILE, seq)
    n_seq_tiles = seq // tm
    row = lambda c: pl.BlockSpec((tm, c), lambda i: (i, 0))
    tab = pl.BlockSpec((tm, LANES), lambda i: (i % n_seq_tiles, 0))
    out_w = (HEADS_W, HEADS_W, HEADS_W, N_HEADS * LANES, N_HEADS * LANES, HEADS_W)
    return pl.pallas_call(
        _proj_ab_kernel,
        grid=(t // tm,),
        in_specs=[row(D_MODEL), _full_spec(w1.shape), _full_spec(gq.shape), _full_spec(gkv.shape),
                  _full_spec(wq.shape), _full_spec(wqr.shape), _full_spec(wkn.shape), _full_spec(wv.shape),
                  tab, tab, tab, tab],
        out_specs=[row(c) for c in out_w],
        out_shape=[jax.ShapeDtypeStruct((t, c), BF16) for c in out_w],
        compiler_params=_cparams("parallel"),
        name="proj_ab",
    )(x2, w1, gq, gkv, wq, wqr, wkn, wv, cosq, sinq, cosk, sink)


def _sb_attn_kernel(q_ref, k_ref, v_ref, o_ref):
    i = pl.program_id(2)
    tq = q_ref.shape[0]
    tk = tq
    grp = min(SB_GROUP, tk)
    q2 = q_ref[...]
    lane = lax.broadcasted_iota(jnp.int32, (1, LANES), 1)
    first = lane < HEAD_DIM
    zero = jnp.zeros_like(q2)
    q_heads = (jnp.where(first, q2, zero), jnp.where(first, zero, q2))

    rg = lax.broadcasted_iota(jnp.int32, (grp, grp), 0)
    cg = lax.broadcasted_iota(jnp.int32, (grp, grp), 1)
    suffix = jnp.where(rg > cg, 1.0, 0.0).astype(BF16)
    r = lax.broadcasted_iota(jnp.int32, (tq, tk), 0)
    c = lax.broadcasted_iota(jnp.int32, (tq, tk), 1)
    past = c < r

    def tile(j, carry, diagonal):
        start = pl.multiple_of(j * tk, tk)
        k2 = k_ref[pl.ds(start, tk), :]
        v2 = v_ref[pl.ds(start, tk), :]
        out = []
        for h in range(2):
            acc, run = carry[h]
            z = _dot_nt(q_heads[h], k2)
            soft = jnp.log1p(jnp.exp(-jnp.abs(z)))
            log_beta = jnp.minimum(z, 0.0) - soft
            log_keep = -jnp.maximum(z, 0.0) - soft
            if diagonal:
                log_keep = jnp.where(past, log_keep, 0.0)
            hi = log_keep.astype(BF16)
            lo = (log_keep - hi.astype(F32)).astype(BF16)
            laters = []
            for g in reversed(range(tk // grp)):
                cols = slice(g * grp, (g + 1) * grp)
                laters.insert(0, _dot(hi[:, cols], suffix) + _dot(lo[:, cols], suffix) + run)
                run = run + jnp.sum(log_keep[:, cols], axis=-1, keepdims=True)
            w = jnp.exp(log_beta + jnp.concatenate(laters, axis=1))
            if diagonal:
                w = jnp.where(past, w, 0.0)
            acc = acc + _dot(w.astype(BF16), v2)
            out.append((acc, run))
        return tuple(out)

    init = tuple((jnp.zeros((tq, LANES), F32), jnp.zeros((tq, 1), F32)) for _ in range(2))
    carry = tile(i, init, True)
    carry = lax.fori_loop(0, i, lambda jj, cr: tile(i - 1 - jj, cr, False), carry)
    o_ref[...] = jnp.where(first, carry[0][0], carry[1][0]).astype(o_ref.dtype)


def _sb_attention(q, k, v):
    b, s, _ = q.shape
    tq = min(ATT_TILE, s)
    qspec = pl.BlockSpec((None, tq, LANES), lambda bi, h, i: (bi, i, h))
    kspec = pl.BlockSpec((None, s, LANES), lambda bi, h, i: (bi, 0, h))
    return pl.pallas_call(
        _sb_attn_kernel,
        grid=(b, N_HEADS // 2, s // tq),
        in_specs=[qspec, kspec, kspec],
        out_specs=qspec,
        out_shape=jax.ShapeDtypeStruct((b, s, HEADS_W), BF16),
        compiler_params=_cparams("parallel", "parallel", "arbitrary"),
        name="sb_attn",
    )(q, k, v)


def _softmax_tile(s, v2, state):
    m, l, acc = state
    m_new = jnp.maximum(m, jnp.max(s, axis=-1, keepdims=True))
    p = jnp.exp(s - m_new)
    alpha = jnp.exp(m - m_new)
    l = alpha * l + jnp.sum(p, axis=-1, keepdims=True)
    acc = alpha * acc + _dot(p.astype(BF16), v2)
    return m_new, l, acc


def _softmax_init(tq):
    return tuple((jnp.full((tq, 1), NEG_BIG, F32), jnp.zeros((tq, 1), F32), jnp.zeros((tq, LANES), F32))
                 for _ in range(2))


def _mla_attn_kernel(q_ref, k_ref, v_ref, o_ref):
    i = pl.program_id(2)
    tq = q_ref.shape[0]
    tk = tq
    q_heads = (q_ref[:, 0:LANES], q_ref[:, LANES:2 * LANES])
    lane = lax.broadcasted_iota(jnp.int32, (1, LANES), 1)
    first = lane < HEAD_DIM
    r = lax.broadcasted_iota(jnp.int32, (tq, tk), 0)
    c = lax.broadcasted_iota(jnp.int32, (tq, tk), 1)
    causal = c <= r

    def tile(j, carry, diagonal):
        start = pl.multiple_of(j * tk, tk)
        v2 = v_ref[pl.ds(start, tk), :]
        out = []
        for h in range(2):
            kh = k_ref[pl.ds(start, tk), h * LANES:(h + 1) * LANES]
            s = _dot_nt(q_heads[h], kh)
            if diagonal:
                s = jnp.where(causal, s, NEG_BIG)
            out.append(_softmax_tile(s, v2, carry[h]))
        return tuple(out)

    carry = tile(i, _softmax_init(tq), True)
    carry = lax.fori_loop(0, i, lambda j, cr: tile(j, cr, False), carry)
    o0 = carry[0][2] / carry[0][1]
    o1 = carry[1][2] / carry[1][1]
    o_ref[...] = jnp.where(first, o0, o1).astype(o_ref.dtype)


def _mla_attention(qf, kf, v):
    b, s, _ = v.shape
    tq = min(ATT_TILE, s)
    qspec = pl.BlockSpec((None, tq, 2 * LANES), lambda bi, h, i: (bi, i, h))
    kspec = pl.BlockSpec((None, s, 2 * LANES), lambda bi, h, i: (bi, 0, h))
    vspec = pl.BlockSpec((None, s, LANES), lambda bi, h, i: (bi, 0, h))
    ospec = pl.BlockSpec((None, tq, LANES), lambda bi, h, i: (bi, i, h))
    return pl.pallas_call(
        _mla_attn_kernel,
        grid=(b, N_HEADS // 2, s // tq),
        in_specs=[qspec, kspec, vspec],
        out_specs=ospec,
        out_shape=jax.ShapeDtypeStruct((b, s, HEADS_W), BF16),
        compiler_params=_cparams("parallel", "parallel", "arbitrary"),
        name="mla_attn",
    )(qf, kf, v)


def _out_ln_kernel(a_ref, b_ref, x_ref, wa_ref, wb_ref, g_ref, beta_ref, o_ref):
    mix = _dot(a_ref[...], wa_ref[...]) + _dot(b_ref[...], wb_ref[...])
    o_ref[...] = _layer_norm_rows(DN_ALPHA * x_ref[...] + mix, g_ref[...], beta_ref[...])


def _out_ln(a, b, x2, wa, wb, g, beta):
    t = x2.shape[0]
    tm = min(ROW_TILE, t)
    row = lambda c: pl.BlockSpec((tm, c), lambda i: (i, 0))
    return pl.pallas_call(
        _out_ln_kernel,
        grid=(t // tm,),
        in_specs=[row(a.shape[1]), row(b.shape[1]), row(D_MODEL), _full_spec(wa.shape), _full_spec(wb.shape),
                  _full_spec(g.shape), _full_spec(beta.shape)],
        out_specs=row(D_MODEL),
        out_shape=jax.ShapeDtypeStruct((t, D_MODEL), F32),
        compiler_params=_cparams("parallel"),
        name="out_ln",
    )(a, b, x2, wa, wb, g, beta)


def _ffn_ln_kernel(x_ref, wg_ref, wu_ref, wd_ref, g_ref, beta_ref, o_ref):
    x = x_ref[...]
    xb = x.astype(BF16)
    gate = _dot(xb, wg_ref[...])
    up = _dot(xb, wu_ref[...])
    h = (gate * (1.0 / (1.0 + jnp.exp(-gate))) * up).astype(BF16)
    y = _dot(h, wd_ref[...])
    o_ref[...] = _layer_norm_rows(DN_ALPHA * x + y, g_ref[...], beta_ref[...])


def _ffn_ln(x2, wg, wu, wd, g, beta):
    t = x2.shape[0]
    tm = min(ROW_TILE, t)
    row = pl.BlockSpec((tm, D_MODEL), lambda i: (i, 0))
    resident = lambda shape: pl.BlockSpec(shape, lambda i: (0, 0), pipeline_mode=pl.Buffered(1))
    return pl.pallas_call(
        _ffn_ln_kernel,
        grid=(t // tm,),
        in_specs=[row, resident(wg.shape), resident(wu.shape), resident(wd.shape),
                  _full_spec(g.shape), _full_spec(beta.shape)],
        out_specs=row,
        out_shape=jax.ShapeDtypeStruct((t, D_MODEL), F32),
        compiler_params=_cparams("parallel"),
        name="ffn_ln",
    )(x2, wg, wu, wd, g, beta)


def _proj_cd_kernel(x_ref, w_ref, u_ref, q_ref, k_ref, v_ref):
    p = _dot(x_ref[...].astype(BF16), w_ref[...])
    u_ref[...] = p[:, 0:512].astype(BF16)
    q_ref[...] = (p[:, 512:1024] * (HEAD_DIM ** -0.5)).astype(BF16)
    k_ref[...] = p[:, 1024:1536].astype(BF16)
    v_ref[...] = p[:, 1536:2048].astype(BF16)


def _proj_cd(x2, w):
    t = x2.shape[0]
    tm = min(ROW_TILE, t)
    row = lambda c: pl.BlockSpec((tm, c), lambda i: (i, 0))
    return pl.pallas_call(
        _proj_cd_kernel,
        grid=(t // tm,),
        in_specs=[row(D_MODEL), _full_spec(w.shape)],
        out_specs=[row(HEADS_W)] * 4,
        out_shape=[jax.ShapeDtypeStruct((t, HEADS_W), BF16)] * 4,
        compiler_params=_cparams("parallel"),
        name="proj_cd",
    )(x2, w)


S5_SCAN_LANES = 1024


def _s5_kernel(u_ref, bmat_ref, cmat_ref, are_ref, aim_ref, d_ref, wglu_ref, bglu_ref, o_ref,
               bu_ref, state_ref):
    nb, ts, _ = u_ref.shape
    rows = ts * nb

    @pl.when(pl.program_id(0) == 0)
    def _():
        state_ref[...] = jnp.zeros_like(state_ref)

    uf = jnp.swapaxes(u_ref[...].astype(F32), 0, 1).reshape(rows, S5_CHANNELS)
    ub = uf.astype(BF16)
    n_chunks = S5_CHANNELS // LANES
    cw = S5_WIDTH // n_chunks
    for j in range(n_chunks):
        bu = _dot(ub[:, j * LANES:(j + 1) * LANES], bmat_ref[j])
        bu_ref[:, j * cw:(j + 1) * cw] = bu[:, :cw]
        bu_ref[:, S5_WIDTH + j * cw:S5_WIDTH + (j + 1) * cw] = bu[:, cw:]

    for w in range(S5_WIDTH // S5_SCAN_LANES):
        re_cols = slice(w * S5_SCAN_LANES, (w + 1) * S5_SCAN_LANES)
        im_cols = slice(S5_WIDTH + w * S5_SCAN_LANES, S5_WIDTH + (w + 1) * S5_SCAN_LANES)
        ar = jnp.broadcast_to(are_ref[:, re_cols], (nb, S5_SCAN_LANES))
        ai = jnp.broadcast_to(aim_ref[:, re_cols], (nb, S5_SCAN_LANES))

        def step(t, carry):
            xr, xi = carry
            r0 = pl.multiple_of(t * nb, nb)
            br = bu_ref[pl.ds(r0, nb), re_cols]
            bi = bu_ref[pl.ds(r0, nb), im_cols]
            nr = ar * xr - ai * xi + br
            ni = ar * xi + ai * xr + bi
            bu_ref[pl.ds(r0, nb), re_cols] = nr
            bu_ref[pl.ds(r0, nb), im_cols] = ni
            return nr, ni

        xr, xi = lax.fori_loop(0, ts, step, (state_ref[:, re_cols], state_ref[:, im_cols]), unroll=8)
        state_ref[:, re_cols] = xr
        state_ref[:, im_cols] = xi

    ys = []
    for j in range(n_chunks):
        xin = jnp.concatenate([bu_ref[:, j * cw:(j + 1) * cw],
                               bu_ref[:, S5_WIDTH + j * cw:S5_WIDTH + (j + 1) * cw]], axis=1).astype(BF16)
        ys.append(_dot(xin, cmat_ref[j]))
    y = jnp.concatenate(ys, axis=1) + d_ref[...] * uf
    z = 0.5 * y * (1.0 + jnp.tanh(math.sqrt(2.0 / math.pi) * (y + 0.044715 * (y * y * y))))
    gate = _dot(z.astype(BF16), wglu_ref[...]) + bglu_ref[...]
    out = z * (1.0 / (1.0 + jnp.exp(-gate)))
    o_ref[...] = jnp.swapaxes(out.reshape(ts, nb, S5_CHANNELS), 0, 1).astype(o_ref.dtype)


def _s5(u, bmat, cmat, are, aim, d, wglu, bglu):
    b, s, _ = u.shape
    ts = min(S5_TIME_TILE, s)
    blk = pl.BlockSpec((b, ts, S5_CHANNELS), lambda i: (0, i, 0))
    return pl.pallas_call(
        _s5_kernel,
        grid=(s // ts,),
        in_specs=[blk, _full_spec(bmat.shape), _full_spec(cmat.shape), _full_spec(are.shape),
                  _full_spec(aim.shape), _full_spec(d.shape), _full_spec(wglu.shape), _full_spec(bglu.shape)],
        out_specs=blk,
        out_shape=jax.ShapeDtypeStruct((b, s, S5_CHANNELS), BF16),
        scratch_shapes=[pltpu.VMEM((ts * b, 2 * S5_WIDTH), F32), pltpu.VMEM((b, 2 * S5_WIDTH), F32)],
        compiler_params=_cparams("arbitrary"),
        name="s5_scan",
    )(u, bmat, cmat, are, aim, d, wglu, bglu)


def _moba_kernel(q_ref, k_ref, v_ref, o_ref, kmean_ref, sel_ref):
    i = pl.program_id(2)
    tq = q_ref.shape[0]
    tk = tq
    per_tile = tq // MOBA_BLOCK
    nblk = k_ref.shape[0] // MOBA_BLOCK
    lane = lax.broadcasted_iota(jnp.int32, (1, LANES), 1)
    first = lane < HEAD_DIM

    @pl.when(i == 0)
    def _():
        kmean_ref[...] = jnp.zeros_like(kmean_ref)
        for n in range(nblk):
            kb = k_ref[n * MOBA_BLOCK:(n + 1) * MOBA_BLOCK, :].astype(F32)
            kmean_ref[n:n + 1, :] = jnp.mean(kb, axis=0, keepdims=True)

    q2 = q_ref[...]
    zero = jnp.zeros_like(q2)
    q_heads = (jnp.where(first, q2, zero), jnp.where(first, zero, q2))

    km = kmean_ref[...]
    k1 = km.astype(BF16)
    rem = km - k1.astype(F32)
    k2 = rem.astype(BF16)
    k3 = (rem - k2.astype(F32)).astype(BF16)
    rowid = lax.broadcasted_iota(jnp.int32, (8, tq), 0)
    qblk = i * per_tile + lax.broadcasted_iota(jnp.int32, (8, tq), 1) // MOBA_BLOCK
    valid = rowid < qblk
    for h in range(2):
        qh = q_heads[h]
        gate_t = _dot_nt(k1, qh) + _dot_nt(k2, qh) + _dot_nt(k3, qh)
        g = jnp.where(valid, gate_t[0:8, :], -jnp.inf)
        beaten = jnp.zeros((8, tq), jnp.int32)
        for m in range(min(nblk, 8)):
            gm = g[m:m + 1, :]
            wins = (gm > g) | ((gm == g) & (rowid > m))
            beaten = beaten + jnp.where(wins, 1, 0)
        chosen = jnp.where(valid & (beaten < MOBA_TOPK), 1.0, 0.0)
        chosen = jnp.concatenate([chosen, jnp.zeros((LANES - 8, tq), F32)], axis=0)
        sel_ref[h] = chosen.T

    r = lax.broadcasted_iota(jnp.int32, (tq, tk), 0)
    c = lax.broadcasted_iota(jnp.int32, (tq, tk), 1)
    own_causal = (r // MOBA_BLOCK == c // MOBA_BLOCK) & (c <= r)

    def tile(j, carry, diagonal):
        start = pl.multiple_of(j * tk, tk)
        k2b = k_ref[pl.ds(start, tk), :]
        v2b = v_ref[pl.ds(start, tk), :]
        out = []
        for h in range(2):
            sel = sel_ref[h]
            picked = None
            for g in range(per_tile):
                col = jnp.max(jnp.where(lane == j * per_tile + g, sel, 0.0), axis=-1, keepdims=True)
                col = jnp.broadcast_to(col, (tq, tk))
                picked = col if picked is None else jnp.where(c >= g * MOBA_BLOCK, col, picked)
            allowed = picked > 0.5
            if diagonal:
                allowed = allowed | own_causal
            s = jnp.where(allowed, _dot_nt(q_heads[h], k2b), NEG_BIG)
            out.append(_softmax_tile(s, v2b, carry[h]))
        return tuple(out)

    carry = tile(i, _softmax_init(tq), True)
    carry = lax.fori_loop(0, i, lambda n, cr: tile(n, cr, False), carry)
    o0 = carry[0][2] / carry[0][1]
    o1 = carry[1][2] / carry[1][1]
    o_ref[...] = jnp.where(first, o0, o1).astype(o_ref.dtype)


def _moba_attention(q, k, v):
    b, s, _ = q.shape
    tq = min(ATT_TILE, s)
    assert s % tq == 0 and tq % MOBA_BLOCK == 0 and s // MOBA_BLOCK <= 8
    qspec = pl.BlockSpec((None, tq, LANES), lambda bi, h, i: (bi, i, h))
    kspec = pl.BlockSpec((None, s, LANES), lambda bi, h, i: (bi, 0, h))
    return pl.pallas_call(
        _moba_kernel,
        grid=(b, N_HEADS // 2, s // tq),
        in_specs=[qspec, kspec, kspec],
        out_specs=qspec,
        out_shape=jax.ShapeDtypeStruct((b, s, HEADS_W), BF16),
        scratch_shapes=[pltpu.VMEM((LANES, LANES), F32), pltpu.VMEM((2, tq, LANES), F32)],
        compiler_params=_cparams("parallel", "parallel", "arbitrary"),
        name="moba_attn",
    )(q, k, v)


def _rotate_half_cols(w):
    half = w.shape[-1] // 2
    return jnp.concatenate([-w[..., half:], w[..., :half]], axis=-1)


def _layer0_params(w_in, q_norm, w_uq, kv_norm, w_ukv, seq):
    d = w_in.shape[0]
    w_rope = w_in[:, 2048:2080]
    pad_l = jnp.zeros((d, MLA_NOPE), F32)
    pad_r = jnp.zeros((d, LANES - MLA_NOPE - MLA_ROPE), F32)
    w1 = jnp.concatenate([w_in[:, :2048], pad_l, w_rope, pad_r, pad_l, _rotate_half_cols(w_rope), pad_r], axis=1)

    wq3 = w_uq.reshape(MLA_RANK, N_HEADS, MLA_NOPE + MLA_ROPE)
    nope, rope = wq3[..., :MLA_NOPE], wq3[..., MLA_NOPE:]
    z_nope = jnp.zeros_like(nope)
    z_pad = jnp.zeros((MLA_RANK, N_HEADS, LANES - MLA_NOPE - MLA_ROPE), F32)
    wq = jnp.concatenate([nope, rope, z_pad], axis=-1).reshape(MLA_RANK, N_HEADS * LANES)
    wqr = jnp.concatenate([z_nope, _rotate_half_cols(rope), z_pad], axis=-1).reshape(MLA_RANK, N_HEADS * LANES)

    wkv3 = w_ukv.reshape(MLA_RANK, N_HEADS, 2 * HEAD_DIM)
    wkn = jnp.concatenate([wkv3[..., :MLA_NOPE], jnp.zeros((MLA_RANK, N_HEADS, LANES - MLA_NOPE), F32)],
                          axis=-1).reshape(MLA_RANK, N_HEADS * LANES)
    wv = wkv3[..., MLA_NOPE:].reshape(MLA_RANK, HEADS_W)

    half = MLA_ROPE // 2
    freqs = ROPE_BASE ** (-jnp.arange(half, dtype=F32) / half)
    ang = jnp.arange(seq, dtype=F32)[:, None] * freqs
    cos = jnp.concatenate([jnp.cos(ang)] * 2, axis=1)
    sin = jnp.concatenate([jnp.sin(ang)] * 2, axis=1)
    ones = jnp.ones((seq, MLA_NOPE), F32)
    zl = jnp.zeros((seq, MLA_NOPE), F32)
    zr = jnp.zeros((seq, LANES - MLA_NOPE - MLA_ROPE), F32)
    scale = (MLA_NOPE + MLA_ROPE) ** -0.5
    cosq = jnp.concatenate([ones, cos, zr], axis=1) * scale
    sinq = jnp.concatenate([zl, sin, zr], axis=1) * scale
    cosk = jnp.concatenate([zl, cos, zr], axis=1)
    sink = jnp.concatenate([zl, sin, zr], axis=1)
    return (w1.astype(BF16), q_norm.reshape(1, -1), kv_norm.reshape(1, -1), wq.astype(BF16), wqr.astype(BF16),
            wkn.astype(BF16), wv.astype(BF16), cosq, sinq, cosk, sink)


def _s5_params(lam_re, lam_im, log_dt, b_re, b_im, c_re, c_im):
    dt = jnp.exp(log_dt)[:, None]
    mag = jnp.exp(lam_re * dt)
    ab_re, ab_im = mag * jnp.cos(lam_im * dt), mag * jnp.sin(lam_im * dt)
    den = lam_re * lam_re + lam_im * lam_im
    nr, ni = ab_re - 1.0, ab_im
    f_re, f_im = (nr * lam_re + ni * lam_im) / den, (ni * lam_re - nr * lam_im) / den
    bb_re = f_re[..., None] * b_re - f_im[..., None] * b_im
    bb_im = f_re[..., None] * b_im + f_im[..., None] * b_re

    gpc = LANES // S5_GROUP
    n_chunks = S5_GROUPS // gpc
    eye = jnp.eye(gpc, dtype=F32)

    def in_blocks(bb):
        t = bb.reshape(n_chunks, gpc, S5_STATE, S5_GROUP)
        return jnp.einsum('cgph,gk->cghkp', t, eye).reshape(n_chunks, gpc * S5_GROUP, gpc * S5_STATE)

    def out_blocks(cc):
        t = cc.reshape(n_chunks, gpc, S5_GROUP, S5_STATE)
        return jnp.einsum('cghp,gk->cgpkh', t, eye).reshape(n_chunks, gpc * S5_STATE, gpc * S5_GROUP)

    bmat = jnp.concatenate([in_blocks(bb_re), in_blocks(bb_im)], axis=2)
    cmat = jnp.concatenate([out_blocks(c_re), -out_blocks(c_im)], axis=1)
    a_re = ab_re.reshape(1, S5_WIDTH)
    a_im = ab_im.reshape(1, S5_WIDTH)
    return bmat.astype(BF16), cmat.astype(BF16), a_re, a_im


def kernel(x, ab_w_in, ab_q_norm, ab_w_uq, ab_kv_norm, ab_w_ukv, ab_w_out, cd_w_in, s5_lambda_re, s5_lambda_im,
           s5_log_dt, s5_b_re, s5_b_im, s5_c_re, s5_c_im, s5_d, s5_w_glu, s5_b_glu, cd_w_out, ln1_g, ln1_b, ln2_g,
           ln2_b, ffn_w_gate, ffn_w_up, ffn_w_down):
    b, s, d = x.shape
    t = b * s
    x2 = x.reshape(t, d)
    vec = lambda a: a.reshape(1, -1)

    p0 = _layer0_params(ab_w_in[0], ab_q_norm[0], ab_w_uq[0], ab_kv_norm[0], ab_w_ukv[0], s)
    qsb, ksb, vsb, qf, kf, vm = _proj_ab(x2, *p0, seq=s)
    sh = lambda a: a.reshape(b, s, a.shape[-1])
    o_sb = _sb_attention(sh(qsb), sh(ksb), sh(vsb)).reshape(t, HEADS_W)
    o_mla = _mla_attention(sh(qf), sh(kf), sh(vm)).reshape(t, HEADS_W)
    w_out = ab_w_out[0].astype(BF16)
    x2 = _out_ln(o_sb, o_mla, x2, w_out[:HEADS_W], w_out[HEADS_W:], vec(ln1_g[0]), vec(ln1_b[0]))
    x2 = _ffn_ln(x2, ffn_w_gate[0].astype(BF16), ffn_w_up[0].astype(BF16), ffn_w_down[0].astype(BF16),
                 vec(ln2_g[0]), vec(ln2_b[0]))

    u, q, k, v = _proj_cd(x2, cd_w_in[0].astype(BF16))
    bmat, cmat, a_re, a_im = _s5_params(s5_lambda_re[0], s5_lambda_im[0], s5_log_dt[0], s5_b_re[0], s5_b_im[0],
                                        s5_c_re[0], s5_c_im[0])
    o_s5 = _s5(sh(u), bmat, cmat, a_re, a_im, vec(s5_d[0]), s5_w_glu[0].astype(BF16), vec(s5_b_glu[0]))
    o_moba = _moba_attention(sh(q), sh(k), sh(v))
    w_out = cd_w_out[0].astype(BF16)
    x2 = _out_ln(o_s5.reshape(t, S5_CHANNELS), o_moba.reshape(t, HEADS_W), x2, w_out[:S5_CHANNELS],
                 w_out[S5_CHANNELS:], vec(ln1_g[1]), vec(ln1_b[1]))
    x2 = _ffn_ln(x2, ffn_w_gate[1].astype(BF16), ffn_w_up[1].astype(BF16), ffn_w_down[1].astype(BF16),
                 vec(ln2_g[1]), vec(ln2_b[1]))
    return x2.reshape(b, s, d)
```

```python
import functools
import math

import jax
import jax.numpy as jnp
from jax import lax
from jax.experimental import pallas as pl
from jax.experimental.pallas import tpu as pltpu

F32 = jnp.float32
BF16 = jnp.bfloat16

D_MODEL = 1024
HEAD_DIM = 64
N_HEADS = 8
HEADS_W = N_HEADS * HEAD_DIM
MLA_RANK = 256
MLA_NOPE = 64
MLA_ROPE = 32
ROPE_BASE = 10000.0
S5_CHANNELS = 512
S5_GROUP = 16
S5_GROUPS = 32
S5_STATE = 64
S5_WIDTH = S5_GROUPS * S5_STATE
MOBA_BLOCK = 256
MOBA_TOPK = 3
D_FF = 2816
DEPTH = 2
DN_ALPHA = (2 * DEPTH) ** 0.25
LN_EPS = 1e-5
RMS_EPS = 1e-6

LANES = 128
NEG_BIG = -1e30
LOG2_E = math.log2(math.e)
VMEM_LIMIT = 56 * 1024 * 1024

ROW_TILE = 512
SB_GROUP = 256
Q_TILE = 512
K_TILE = 512
S5_TIME_TILE = 64


def _cparams(*sem):
    return pltpu.CompilerParams(dimension_semantics=sem, vmem_limit_bytes=VMEM_LIMIT)


def _full_spec(shape):
    nd = len(shape)
    return pl.BlockSpec(shape, lambda *_: (0,) * nd)


def _dot(a, b):
    return jnp.dot(a, b, preferred_element_type=F32)


def _dot_nt(a, b):
    return lax.dot_general(a, b, (((1,), (1,)), ((), ())), preferred_element_type=F32)


def _layer_norm_rows(r, g, b):
    mu = jnp.mean(r, axis=-1, keepdims=True)
    d = r - mu
    var = jnp.mean(d * d, axis=-1, keepdims=True)
    return d * lax.rsqrt(var + LN_EPS) * g + b


def _proj_ab_kernel(x_ref, w1_ref, gq_ref, gkv_ref, wq_ref, wqr_ref, wkn_ref, wv_ref,
                    cosq_ref, sinq_ref, cosk_ref, sink_ref,
                    qsb_ref, ksb_ref, vsb_ref, qf_ref, kf_ref, vm_ref):
    xb = x_ref[...].astype(BF16)
    p = _dot(xb, w1_ref[...])
    qsb_ref[...] = (p[:, 0:512] * (HEAD_DIM ** -0.5)).astype(BF16)
    ksb_ref[...] = p[:, 512:1024].astype(BF16)
    vsb_ref[...] = p[:, 1024:1536].astype(BF16)
    cq = p[:, 1536:1792]
    ckv = p[:, 1792:2048]
    rope = p[:, 2048:2176]
    rope_rot = p[:, 2176:2304]

    cqn = cq * lax.rsqrt(jnp.mean(cq * cq, axis=-1, keepdims=True) + RMS_EPS) * gq_ref[...]
    ckvn = ckv * lax.rsqrt(jnp.mean(ckv * ckv, axis=-1, keepdims=True) + RMS_EPS) * gkv_ref[...]
    cqb = cqn.astype(BF16)
    ckvb = ckvn.astype(BF16)

    cosq = jnp.concatenate([cosq_ref[...]] * N_HEADS, axis=1)
    sinq = jnp.concatenate([sinq_ref[...]] * N_HEADS, axis=1)
    qf = _dot(cqb, wq_ref[...]) * cosq + _dot(cqb, wqr_ref[...]) * sinq
    qf_ref[...] = qf.astype(BF16)

    kpe = rope * cosk_ref[...] + rope_rot * sink_ref[...]
    kf = _dot(ckvb, wkn_ref[...]) + jnp.concatenate([kpe] * N_HEADS, axis=1)
    kf_ref[...] = kf.astype(BF16)
    vm_ref[...] = _dot(ckvb, wv_ref[...]).astype(BF16)


def _proj_ab(x2, w1, gq, gkv, wq, wqr, wkn, wv, cosq, sinq, cosk, sink, seq):
    t = x2.shape[0]
    tm = min(ROW_TILE, seq)
    n_seq_tiles = seq // tm
    row = lambda c: pl.BlockSpec((tm, c), lambda i: (i, 0))
    tab = pl.BlockSpec((tm, LANES), lambda i: (i % n_seq_tiles, 0))
    out_w = (HEADS_W, HEADS_W, HEADS_W, N_HEADS * LANES, N_HEADS * LANES, HEADS_W)
    return pl.pallas_call(
        _proj_ab_kernel,
        grid=(t // tm,),
        in_specs=[row(D_MODEL), _full_spec(w1.shape), _full_spec(gq.shape), _full_spec(gkv.shape),
                  _full_spec(wq.shape), _full_spec(wqr.shape), _full_spec(wkn.shape), _full_spec(wv.shape),
                  tab, tab, tab, tab],
        out_specs=[row(c) for c in out_w],
        out_shape=[jax.ShapeDtypeStruct((t, c), BF16) for c in out_w],
        compiler_params=_cparams("parallel"),
        name="proj_ab",
    )(x2, w1, gq, gkv, wq, wqr, wkn, wv, cosq, sinq, cosk, sink)


def _key_chunks(qi, tq, tk):
    end = (qi + 1) * tq
    return [(ks, min(tk, end - ks)) for ks in range(0, end, tk)]


def _sb_tile(qh, k2, v2, acc, run, past, suffix):
    grp = suffix.shape[0]
    z = _dot_nt(qh, k2)
    soft = jnp.log(1.0 + jnp.exp(-jnp.abs(z)))
    log_beta = jnp.minimum(z, 0.0) - soft
    log_keep = log_beta - z
    if past is not None:
        log_keep = jnp.where(past, log_keep, 0.0)
    hi = log_keep.astype(BF16)
    lo = (log_keep - hi.astype(F32)).astype(BF16)
    laters = []
    for g in reversed(range(z.shape[1] // grp)):
        cols = slice(g * grp, (g + 1) * grp)
        laters.insert(0, _dot(hi[:, cols], suffix) + _dot(lo[:, cols], suffix) + run)
        run = run + jnp.sum(log_keep[:, cols], axis=-1, keepdims=True)
    w = jnp.exp(log_beta + jnp.concatenate(laters, axis=1))
    if past is not None:
        w = jnp.where(past, w, 0.0)
    return acc + _dot(w.astype(BF16), v2), run


def _sb_attn_kernel(q_ref, k_ref, v_ref, o_ref, *, tq, tk):
    seq = q_ref.shape[0]
    grp = min(SB_GROUP, tq)
    lane = lax.broadcasted_iota(jnp.int32, (1, LANES), 1)
    first = lane < HEAD_DIM
    rg = lax.broadcasted_iota(jnp.int32, (grp, grp), 0)
    cg = lax.broadcasted_iota(jnp.int32, (grp, grp), 1)
    suffix = jnp.where(rg > cg, 1.0, 0.0).astype(BF16)

    for qi in range(seq // tq):
        rows = slice(qi * tq, (qi + 1) * tq)
        q2 = q_ref[rows, :]
        zero = jnp.zeros_like(q2)
        q_heads = (jnp.where(first, q2, zero), jnp.where(first, zero, q2))
        chunks = _key_chunks(qi, tq, tk)
        carry = [(jnp.zeros((tq, LANES), F32), jnp.zeros((tq, 1), F32)) for _ in range(2)]
        for ci in reversed(range(len(chunks))):
            ks, w = chunks[ci]
            keys = slice(ks, ks + w)
            past = None
            if ci == len(chunks) - 1:
                r = lax.broadcasted_iota(jnp.int32, (tq, w), 0)
                c = lax.broadcasted_iota(jnp.int32, (tq, w), 1)
                past = c < r + (qi * tq - ks)
            for h in range(2):
                carry[h] = _sb_tile(q_heads[h], k_ref[keys, :], v_ref[keys, :], *carry[h], past, suffix)
        o_ref[rows, :] = jnp.where(first, carry[0][0], carry[1][0]).astype(o_ref.dtype)


def _sb_attention(q, k, v):
    b, s, _ = q.shape
    tq, tk = min(Q_TILE, s), min(K_TILE, s)
    assert s % tq == 0 and tk % tq == 0
    spec = pl.BlockSpec((None, s, LANES), lambda bi, h: (bi, 0, h))
    return pl.pallas_call(
        functools.partial(_sb_attn_kernel, tq=tq, tk=tk),
        grid=(b, N_HEADS // 2),
        in_specs=[spec, spec, spec],
        out_specs=spec,
        out_shape=jax.ShapeDtypeStruct((b, s, HEADS_W), BF16),
        compiler_params=_cparams("parallel", "parallel"),
        name="sb_attn",
    )(q, k, v)


def _softmax_tile(s, v_ones, state):
    m, acc = state
    m_new = jnp.maximum(m, jnp.max(s, axis=-1, keepdims=True))
    p = jnp.exp2(s - m_new)
    acc = jnp.exp2(m - m_new) * acc + _dot(p.astype(BF16), v_ones)
    return m_new, acc


def _softmax_state(tq):
    return jnp.full((tq, 1), NEG_BIG, F32), jnp.zeros((tq, LANES), F32)


def _softmax_finish(state_a, state_b, first):
    acc_a, acc_b = state_a[1], state_b[1]
    out_a = acc_a / pltpu.roll(acc_a, HEAD_DIM, 1)
    out_b = acc_b / pltpu.roll(acc_b, HEAD_DIM, 1)
    return jnp.where(first, out_a, out_b)


def _with_ones(v2, first):
    one = jnp.ones_like(v2)
    return jnp.where(first, v2, one), jnp.where(first, one, v2)


def _mla_attn_kernel(q_ref, k_ref, v_ref, o_ref, *, tq, tk):
    seq = q_ref.shape[0]
    lane = lax.broadcasted_iota(jnp.int32, (1, LANES), 1)
    first = lane < HEAD_DIM

    for qi in range(seq // tq):
        rows = slice(qi * tq, (qi + 1) * tq)
        chunks = _key_chunks(qi, tq, tk)
        last = len(chunks) - 1
        carry = [_softmax_state(tq), _softmax_state(tq)]
        for ci in (last, *range(last)):
            ks, w = chunks[ci]
            keys = slice(ks, ks + w)
            v_ones = _with_ones(v_ref[keys, :], first)
            for h in range(2):
                s = _dot_nt(q_ref[rows, h * LANES:(h + 1) * LANES], k_ref[keys, h * LANES:(h + 1) * LANES])
                if ci == last:
                    r = lax.broadcasted_iota(jnp.int32, (tq, w), 0)
                    c = lax.broadcasted_iota(jnp.int32, (tq, w), 1)
                    s = jnp.where(c <= r + (qi * tq - ks), s, NEG_BIG)
                carry[h] = _softmax_tile(s, v_ones[h], carry[h])
        o_ref[rows, :] = _softmax_finish(carry[0], carry[1], first).astype(o_ref.dtype)


def _mla_attention(qf, kf, v):
    b, s, _ = v.shape
    tq, tk = min(Q_TILE, s), min(K_TILE, s)
    assert s % tq == 0 and tk % tq == 0
    kspec = pl.BlockSpec((None, s, 2 * LANES), lambda bi, h: (bi, 0, h))
    vspec = pl.BlockSpec((None, s, LANES), lambda bi, h: (bi, 0, h))
    return pl.pallas_call(
        functools.partial(_mla_attn_kernel, tq=tq, tk=tk),
        grid=(b, N_HEADS // 2),
        in_specs=[kspec, kspec, vspec],
        out_specs=vspec,
        out_shape=jax.ShapeDtypeStruct((b, s, HEADS_W), BF16),
        compiler_params=_cparams("parallel", "parallel"),
        name="mla_attn",
    )(qf, kf, v)


def _moba_kernel(q_ref, k_ref, v_ref, o_ref, *, tq, tk):
    seq = q_ref.shape[0]
    nblk = seq // MOBA_BLOCK
    lane = lax.broadcasted_iota(jnp.int32, (1, LANES), 1)
    first = lane < HEAD_DIM

    means = [jnp.mean(k_ref[n * MOBA_BLOCK:(n + 1) * MOBA_BLOCK, :].astype(F32), axis=0, keepdims=True)
             for n in range(nblk)]
    km = jnp.concatenate(means + [jnp.zeros((LANES - nblk, LANES), F32)], axis=0)
    k1 = km.astype(BF16)
    rem = km - k1.astype(F32)
    k2 = rem.astype(BF16)
    k3 = (rem - k2.astype(F32)).astype(BF16)

    rowid = lax.broadcasted_iota(jnp.int32, (8, tq), 0)
    for qi in range(seq // tq):
        rows = slice(qi * tq, (qi + 1) * tq)
        q2 = q_ref[rows, :]
        zero = jnp.zeros_like(q2)
        q_heads = (jnp.where(first, q2, zero), jnp.where(first, zero, q2))
        qblk = qi * (tq // MOBA_BLOCK) + lax.broadcasted_iota(jnp.int32, (8, tq), 1) // MOBA_BLOCK
        valid = rowid < qblk
        q_aug = []
        for h in range(2):
            qh = q_heads[h]
            gate_t = _dot_nt(k1, qh) + _dot_nt(k2, qh) + _dot_nt(k3, qh)
            g = jnp.where(valid, gate_t[0:8, :], -jnp.inf)
            beaten = jnp.zeros((8, tq), jnp.int32)
            for m in range(nblk):
                gm = g[m:m + 1, :]
                wins = (gm > g) | ((gm == g) & (rowid > m))
                beaten = beaten + jnp.where(wins, 1, 0)
            usable = (valid & (beaten < MOBA_TOPK)) | (rowid == qblk)
            bias_t = jnp.where(usable, 0.0, NEG_BIG)
            bias_t = jnp.concatenate([bias_t, jnp.zeros((LANES - 8, tq), F32)], axis=0)
            q_aug.append(jnp.concatenate([qh, bias_t.T.astype(BF16)], axis=1))

        chunks = _key_chunks(qi, tq, tk)
        last = len(chunks) - 1
        carry = [_softmax_state(tq), _softmax_state(tq)]
        for ci in (last, *range(last)):
            ks, w = chunks[ci]
            keys = slice(ks, ks + w)
            blk = (ks + lax.broadcasted_iota(jnp.int32, (w, LANES), 0)) // MOBA_BLOCK
            onehot = jnp.where(blk == lax.broadcasted_iota(jnp.int32, (w, LANES), 1), 1.0, 0.0).astype(BF16)
            k_aug = jnp.concatenate([k_ref[keys, :], onehot], axis=1)
            v_ones = _with_ones(v_ref[keys, :], first)
            for h in range(2):
                s = _dot_nt(q_aug[h], k_aug)
                if ci == last:
                    r = lax.broadcasted_iota(jnp.int32, (tq, w), 0)
                    c = lax.broadcasted_iota(jnp.int32, (tq, w), 1)
                    s = jnp.where(c <= r + (qi * tq - ks), s, NEG_BIG)
                carry[h] = _softmax_tile(s, v_ones[h], carry[h])
        o_ref[rows, :] = _softmax_finish(carry[0], carry[1], first).astype(o_ref.dtype)


def _moba_attention(q, k, v):
    b, s, _ = q.shape
    tq, tk = min(Q_TILE, s), min(K_TILE, s)
    assert s % tq == 0 and tk % tq == 0 and tq % MOBA_BLOCK == 0 and s // MOBA_BLOCK <= 8
    spec = pl.BlockSpec((None, s, LANES), lambda bi, h: (bi, 0, h))
    return pl.pallas_call(
        functools.partial(_moba_kernel, tq=tq, tk=tk),
        grid=(b, N_HEADS // 2),
        in_specs=[spec, spec, spec],
        out_specs=spec,
        out_shape=jax.ShapeDtypeStruct((b, s, HEADS_W), BF16),
        compiler_params=_cparams("parallel", "parallel"),
        name="moba_attn",
    )(q, k, v)


def _out_ln_kernel(a_ref, b_ref, x_ref, wa_ref, wb_ref, g_ref, beta_ref, o_ref):
    mix = _dot(a_ref[...], wa_ref[...]) + _dot(b_ref[...], wb_ref[...])
    o_ref[...] = _layer_norm_rows(DN_ALPHA * x_ref[...] + mix, g_ref[...], beta_ref[...])


def _out_ln(a, b, x2, wa, wb, g, beta):
    t = x2.shape[0]
    tm = min(ROW_TILE, t)
    row = lambda c: pl.BlockSpec((tm, c), lambda i: (i, 0))
    return pl.pallas_call(
        _out_ln_kernel,
        grid=(t // tm,),
        in_specs=[row(a.shape[1]), row(b.shape[1]), row(D_MODEL), _full_spec(wa.shape), _full_spec(wb.shape),
                  _full_spec(g.shape), _full_spec(beta.shape)],
        out_specs=row(D_MODEL),
        out_shape=jax.ShapeDtypeStruct((t, D_MODEL), F32),
        compiler_params=_cparams("parallel"),
        name="out_ln",
    )(a, b, x2, wa, wb, g, beta)


def _ffn_ln_kernel(x_ref, wg_ref, wu_ref, wd_ref, g_ref, beta_ref, o_ref):
    x = x_ref[...]
    xb = x.astype(BF16)
    gate = _dot(xb, wg_ref[...])
    up = _dot(xb, wu_ref[...])
    h = (gate * (1.0 / (1.0 + jnp.exp(-gate))) * up).astype(BF16)
    y = _dot(h, wd_ref[...])
    o_ref[...] = _layer_norm_rows(DN_ALPHA * x + y, g_ref[...], beta_ref[...])


def _ffn_ln(x2, wg, wu, wd, g, beta):
    t = x2.shape[0]
    tm = min(ROW_TILE, t)
    row = pl.BlockSpec((tm, D_MODEL), lambda i: (i, 0))
    resident = lambda shape: pl.BlockSpec(shape, lambda i: (0, 0), pipeline_mode=pl.Buffered(1))
    return pl.pallas_call(
        _ffn_ln_kernel,
        grid=(t // tm,),
        in_specs=[row, resident(wg.shape), resident(wu.shape), resident(wd.shape),
                  _full_spec(g.shape), _full_spec(beta.shape)],
        out_specs=row,
        out_shape=jax.ShapeDtypeStruct((t, D_MODEL), F32),
        compiler_params=_cparams("parallel"),
        name="ffn_ln",
    )(x2, wg, wu, wd, g, beta)


def _proj_cd_kernel(x_ref, w_ref, u_ref, q_ref, k_ref, v_ref):
    p = _dot(x_ref[...].astype(BF16), w_ref[...])
    u_ref[...] = p[:, 0:512].astype(BF16)
    q_ref[...] = (p[:, 512:1024] * (HEAD_DIM ** -0.5 * LOG2_E)).astype(BF16)
    k_ref[...] = p[:, 1024:1536].astype(BF16)
    v_ref[...] = p[:, 1536:2048].astype(BF16)


def _proj_cd(x2, w):
    t = x2.shape[0]
    tm = min(ROW_TILE, t)
    row = lambda c: pl.BlockSpec((tm, c), lambda i: (i, 0))
    return pl.pallas_call(
        _proj_cd_kernel,
        grid=(t // tm,),
        in_specs=[row(D_MODEL), _full_spec(w.shape)],
        out_specs=[row(HEADS_W)] * 4,
        out_shape=[jax.ShapeDtypeStruct((t, HEADS_W), BF16)] * 4,
        compiler_params=_cparams("parallel"),
        name="proj_cd",
    )(x2, w)


S5_SCAN_LANES = 1024


def _s5_kernel(u_ref, bmat_ref, cmat_ref, are_ref, aim_ref, d_ref, wglu_ref, bglu_ref, o_ref,
               bu_ref, state_ref):
    nb, ts, _ = u_ref.shape
    rows = ts * nb

    @pl.when(pl.program_id(0) == 0)
    def _():
        state_ref[...] = jnp.zeros_like(state_ref)

    uf = jnp.swapaxes(u_ref[...].astype(F32), 0, 1).reshape(rows, S5_CHANNELS)
    ub = uf.astype(BF16)
    n_chunks = S5_CHANNELS // LANES
    cw = S5_WIDTH // n_chunks
    for j in range(n_chunks):
        bu = _dot(ub[:, j * LANES:(j + 1) * LANES], bmat_ref[j])
        bu_ref[:, j * cw:(j + 1) * cw] = bu[:, :cw]
        bu_ref[:, S5_WIDTH + j * cw:S5_WIDTH + (j + 1) * cw] = bu[:, cw:]

    for w in range(S5_WIDTH // S5_SCAN_LANES):
        re_cols = slice(w * S5_SCAN_LANES, (w + 1) * S5_SCAN_LANES)
        im_cols = slice(S5_WIDTH + w * S5_SCAN_LANES, S5_WIDTH + (w + 1) * S5_SCAN_LANES)
        ar = jnp.broadcast_to(are_ref[:, re_cols], (nb, S5_SCAN_LANES))
        ai = jnp.broadcast_to(aim_ref[:, re_cols], (nb, S5_SCAN_LANES))

        def step(t, carry):
            xr, xi = carry
            r0 = pl.multiple_of(t * nb, nb)
            br = bu_ref[pl.ds(r0, nb), re_cols]
            bi = bu_ref[pl.ds(r0, nb), im_cols]
            nr = ar * xr - ai * xi + br
            ni = ar * xi + ai * xr + bi
            bu_ref[pl.ds(r0, nb), re_cols] = nr
            bu_ref[pl.ds(r0, nb), im_cols] = ni
            return nr, ni

        xr, xi = lax.fori_loop(0, ts, step, (state_ref[:, re_cols], state_ref[:, im_cols]), unroll=8)
        state_ref[:, re_cols] = xr
        state_ref[:, im_cols] = xi

    ys = []
    for j in range(n_chunks):
        xin = jnp.concatenate([bu_ref[:, j * cw:(j + 1) * cw],
                               bu_ref[:, S5_WIDTH + j * cw:S5_WIDTH + (j + 1) * cw]], axis=1).astype(BF16)
        ys.append(_dot(xin, cmat_ref[j]))
    y = jnp.concatenate(ys, axis=1) + d_ref[...] * uf
    z = 0.5 * y * (1.0 + jnp.tanh(math.sqrt(2.0 / math.pi) * (y + 0.044715 * (y * y * y))))
    gate = _dot(z.astype(BF16), wglu_ref[...]) + bglu_ref[...]
    out = z * (1.0 / (1.0 + jnp.exp(-gate)))
    o_ref[...] = jnp.swapaxes(out.reshape(ts, nb, S5_CHANNELS), 0, 1).astype(o_ref.dtype)


def _s5(u, bmat, cmat, are, aim, d, wglu, bglu):
    b, s, _ = u.shape
    ts = min(S5_TIME_TILE, s)
    blk = pl.BlockSpec((b, ts, S5_CHANNELS), lambda i: (0, i, 0))
    return pl.pallas_call(
        _s5_kernel,
        grid=(s // ts,),
        in_specs=[blk, _full_spec(bmat.shape), _full_spec(cmat.shape), _full_spec(are.shape),
                  _full_spec(aim.shape), _full_spec(d.shape), _full_spec(wglu.shape), _full_spec(bglu.shape)],
        out_specs=blk,
        out_shape=jax.ShapeDtypeStruct((b, s, S5_CHANNELS), BF16),
        scratch_shapes=[pltpu.VMEM((ts * b, 2 * S5_WIDTH), F32), pltpu.VMEM((b, 2 * S5_WIDTH), F32)],
        compiler_params=_cparams("arbitrary"),
        name="s5_scan",
    )(u, bmat, cmat, are, aim, d, wglu, bglu)


def _rotate_half_cols(w):
    half = w.shape[-1] // 2
    return jnp.concatenate([-w[..., half:], w[..., :half]], axis=-1)


def _layer0_params(w_in, q_norm, w_uq, kv_norm, w_ukv, seq):
    d = w_in.shape[0]
    w_rope = w_in[:, 2048:2080]
    pad_l = jnp.zeros((d, MLA_NOPE), F32)
    pad_r = jnp.zeros((d, LANES - MLA_NOPE - MLA_ROPE), F32)
    w1 = jnp.concatenate([w_in[:, :2048], pad_l, w_rope, pad_r, pad_l, _rotate_half_cols(w_rope), pad_r], axis=1)

    wq3 = w_uq.reshape(MLA_RANK, N_HEADS, MLA_NOPE + MLA_ROPE)
    nope, rope = wq3[..., :MLA_NOPE], wq3[..., MLA_NOPE:]
    z_nope = jnp.zeros_like(nope)
    z_pad = jnp.zeros((MLA_RANK, N_HEADS, LANES - MLA_NOPE - MLA_ROPE), F32)
    wq = jnp.concatenate([nope, rope, z_pad], axis=-1).reshape(MLA_RANK, N_HEADS * LANES)
    wqr = jnp.concatenate([z_nope, _rotate_half_cols(rope), z_pad], axis=-1).reshape(MLA_RANK, N_HEADS * LANES)

    wkv3 = w_ukv.reshape(MLA_RANK, N_HEADS, 2 * HEAD_DIM)
    wkn = jnp.concatenate([wkv3[..., :MLA_NOPE], jnp.zeros((MLA_RANK, N_HEADS, LANES - MLA_NOPE), F32)],
                          axis=-1).reshape(MLA_RANK, N_HEADS * LANES)
    wv = wkv3[..., MLA_NOPE:].reshape(MLA_RANK, HEADS_W)

    half = MLA_ROPE // 2
    freqs = ROPE_BASE ** (-jnp.arange(half, dtype=F32) / half)
    ang = jnp.arange(seq, dtype=F32)[:, None] * freqs
    cos = jnp.concatenate([jnp.cos(ang)] * 2, axis=1)
    sin = jnp.concatenate([jnp.sin(ang)] * 2, axis=1)
    ones = jnp.ones((seq, MLA_NOPE), F32)
    zl = jnp.zeros((seq, MLA_NOPE), F32)
    zr = jnp.zeros((seq, LANES - MLA_NOPE - MLA_ROPE), F32)
    scale = (MLA_NOPE + MLA_ROPE) ** -0.5 * LOG2_E
    cosq = jnp.concatenate([ones, cos, zr], axis=1) * scale
    sinq = jnp.concatenate([zl, sin, zr], axis=1) * scale
    cosk = jnp.concatenate([zl, cos, zr], axis=1)
    sink = jnp.concatenate([zl, sin, zr], axis=1)
    return (w1.astype(BF16), q_norm.reshape(1, -1), kv_norm.reshape(1, -1), wq.astype(BF16), wqr.astype(BF16),
            wkn.astype(BF16), wv.astype(BF16), cosq, sinq, cosk, sink)


def _s5_params(lam_re, lam_im, log_dt, b_re, b_im, c_re, c_im):
    dt = jnp.exp(log_dt)[:, None]
    mag = jnp.exp(lam_re * dt)
    ab_re, ab_im = mag * jnp.cos(lam_im * dt), mag * jnp.sin(lam_im * dt)
    den = lam_re * lam_re + lam_im * lam_im
    nr, ni = ab_re - 1.0, ab_im
    f_re, f_im = (nr * lam_re + ni * lam_im) / den, (ni * lam_re - nr * lam_im) / den
    bb_re = f_re[..., None] * b_re - f_im[..., None] * b_im
    bb_im = f_re[..., None] * b_im + f_im[..., None] * b_re

    gpc = LANES // S5_GROUP
    n_chunks = S5_GROUPS // gpc
    eye = jnp.eye(gpc, dtype=F32)

    def in_blocks(bb):
        t = bb.reshape(n_chunks, gpc, S5_STATE, S5_GROUP)
        return jnp.einsum('cgph,gk->cghkp', t, eye).reshape(n_chunks, gpc * S5_GROUP, gpc * S5_STATE)

    def out_blocks(cc):
        t = cc.reshape(n_chunks, gpc, S5_GROUP, S5_STATE)
        return jnp.einsum('cghp,gk->cgpkh', t, eye).reshape(n_chunks, gpc * S5_STATE, gpc * S5_GROUP)

    bmat = jnp.concatenate([in_blocks(bb_re), in_blocks(bb_im)], axis=2)
    cmat = jnp.concatenate([out_blocks(c_re), -out_blocks(c_im)], axis=1)
    a_re = ab_re.reshape(1, S5_WIDTH)
    a_im = ab_im.reshape(1, S5_WIDTH)
    return bmat.astype(BF16), cmat.astype(BF16), a_re, a_im


def kernel(x, ab_w_in, ab_q_norm, ab_w_uq, ab_kv_norm, ab_w_ukv, ab_w_out, cd_w_in, s5_lambda_re, s5_lambda_im,
           s5_log_dt, s5_b_re, s5_b_im, s5_c_re, s5_c_im, s5_d, s5_w_glu, s5_b_glu, cd_w_out, ln1_g, ln1_b, ln2_g,
           ln2_b, ffn_w_gate, ffn_w_up, ffn_w_down):
    b, s, d = x.shape
    t = b * s
    x2 = x.reshape(t, d)
    vec = lambda a: a.reshape(1, -1)

    p0 = _layer0_params(ab_w_in[0], ab_q_norm[0], ab_w_uq[0], ab_kv_norm[0], ab_w_ukv[0], s)
    qsb, ksb, vsb, qf, kf, vm = _proj_ab(x2, *p0, seq=s)
    sh = lambda a: a.reshape(b, s, a.shape[-1])
    o_sb = _sb_attention(sh(qsb), sh(ksb), sh(vsb)).reshape(t, HEADS_W)
    o_mla = _mla_attention(sh(qf), sh(kf), sh(vm)).reshape(t, HEADS_W)
    w_out = ab_w_out[0].astype(BF16)
    x2 = _out_ln(o_sb, o_mla, x2, w_out[:HEADS_W], w_out[HEADS_W:], vec(ln1_g[0]), vec(ln1_b[0]))
    x2 = _ffn_ln(x2, ffn_w_gate[0].astype(BF16), ffn_w_up[0].astype(BF16), ffn_w_down[0].astype(BF16),
                 vec(ln2_g[0]), vec(ln2_b[0]))

    u, q, k, v = _proj_cd(x2, cd_w_in[0].astype(BF16))
    bmat, cmat, a_re, a_im = _s5_params(s5_lambda_re[0], s5_lambda_im[0], s5_log_dt[0], s5_b_re[0], s5_b_im[0],
                                        s5_c_re[0], s5_c_im[0])
    o_s5 = _s5(sh(u), bmat, cmat, a_re, a_im, vec(s5_d[0]), s5_w_glu[0].astype(BF16), vec(s5_b_glu[0]))
    o_moba = _moba_attention(sh(q), sh(k), sh(v))
    w_out = cd_w_out[0].astype(BF16)
    x2 = _out_ln(o_s5.reshape(t, S5_CHANNELS), o_moba.reshape(t, HEADS_W), x2, w_out[:S5_CHANNELS],
                 w_out[S5_CHANNELS:], vec(ln1_g[1]), vec(ln1_b[1]))
    x2 = _ffn_ln(x2, ffn_w_gate[1].astype(BF16), ffn_w_up[1].astype(BF16), ffn_w_down[1].astype(BF16),
                 vec(ln2_g[1]), vec(ln2_b[1]))
    return x2.reshape(b, s, d)
```

```python
import functools
import math

import jax
import jax.numpy as jnp
from jax import lax
from jax.experimental import pallas as pl
from jax.experimental.pallas import tpu as pltpu

F32 = jnp.float32
BF16 = jnp.bfloat16

D_MODEL = 1024
HEAD_DIM = 64
N_HEADS = 8
HEADS_W = N_HEADS * HEAD_DIM
MLA_RANK = 256
MLA_NOPE = 64
MLA_ROPE = 32
ROPE_BASE = 10000.0
S5_CHANNELS = 512
S5_GROUP = 16
S5_GROUPS = 32
S5_STATE = 64
S5_WIDTH = S5_GROUPS * S5_STATE
MOBA_BLOCK = 256
MOBA_TOPK = 3
D_FF = 2816
DEPTH = 2
DN_ALPHA = (2 * DEPTH) ** 0.25
LN_EPS = 1e-5
RMS_EPS = 1e-6

LANES = 128
NEG_BIG = -1e30
LOG2_E = math.log2(math.e)
VMEM_LIMIT = 56 * 1024 * 1024

ROW_TILE = 512
SB_GROUP = 256
SB_Q_TILE = 256
SB_EXIT = -176.0
Q_TILE = 512
K_TILE = 512
S5_TIME_TILE = 64


def _cparams(*sem):
    return pltpu.CompilerParams(dimension_semantics=sem, vmem_limit_bytes=VMEM_LIMIT)


def _full_spec(shape):
    nd = len(shape)
    return pl.BlockSpec(shape, lambda *_: (0,) * nd)


def _dot(a, b):
    return jnp.dot(a, b, preferred_element_type=F32)


def _dot_nt(a, b):
    return lax.dot_general(a, b, (((1,), (1,)), ((), ())), preferred_element_type=F32)


def _layer_norm_rows(r, g, b):
    mu = jnp.mean(r, axis=-1, keepdims=True)
    d = r - mu
    var = jnp.mean(d * d, axis=-1, keepdims=True)
    return d * lax.rsqrt(var + LN_EPS) * g + b


def _proj_ab_kernel(x_ref, w1_ref, gq_ref, gkv_ref, wq_ref, wqr_ref, wkn_ref, wv_ref,
                    cosq_ref, sinq_ref, cosk_ref, sink_ref,
                    qsb_ref, ksb_ref, vsb_ref, qf_ref, kf_ref, vm_ref):
    xb = x_ref[...].astype(BF16)
    p = _dot(xb, w1_ref[...])
    qsb_ref[...] = (p[:, 0:512] * (HEAD_DIM ** -0.5 * LOG2_E)).astype(BF16)
    ksb_ref[...] = p[:, 512:1024].astype(BF16)
    vsb_ref[...] = p[:, 1024:1536].astype(BF16)
    cq = p[:, 1536:1792]
    ckv = p[:, 1792:2048]
    rope = p[:, 2048:2176]
    rope_rot = p[:, 2176:2304]

    cqn = cq * lax.rsqrt(jnp.mean(cq * cq, axis=-1, keepdims=True) + RMS_EPS) * gq_ref[...]
    ckvn = ckv * lax.rsqrt(jnp.mean(ckv * ckv, axis=-1, keepdims=True) + RMS_EPS) * gkv_ref[...]
    cqb = cqn.astype(BF16)
    ckvb = ckvn.astype(BF16)

    cosq = jnp.concatenate([cosq_ref[...]] * N_HEADS, axis=1)
    sinq = jnp.concatenate([sinq_ref[...]] * N_HEADS, axis=1)
    qf = _dot(cqb, wq_ref[...]) * cosq + _dot(cqb, wqr_ref[...]) * sinq
    qf_ref[...] = qf.astype(BF16)

    kpe = rope * cosk_ref[...] + rope_rot * sink_ref[...]
    kf = _dot(ckvb, wkn_ref[...]) + jnp.concatenate([kpe] * N_HEADS, axis=1)
    kf_ref[...] = kf.astype(BF16)
    vm_ref[...] = _dot(ckvb, wv_ref[...]).astype(BF16)


def _proj_ab(x2, w1, gq, gkv, wq, wqr, wkn, wv, cosq, sinq, cosk, sink, seq):
    t = x2.shape[0]
    tm = min(ROW_TILE, seq)
    n_seq_tiles = seq // tm
    row = lambda c: pl.BlockSpec((tm, c), lambda i: (i, 0))
    tab = pl.BlockSpec((tm, LANES), lambda i: (i % n_seq_tiles, 0))
    out_w = (HEADS_W, HEADS_W, HEADS_W, N_HEADS * LANES, N_HEADS * LANES, HEADS_W)
    return pl.pallas_call(
        _proj_ab_kernel,
        grid=(t // tm,),
        in_specs=[row(D_MODEL), _full_spec(w1.shape), _full_spec(gq.shape), _full_spec(gkv.shape),
                  _full_spec(wq.shape), _full_spec(wqr.shape), _full_spec(wkn.shape), _full_spec(wv.shape),
                  tab, tab, tab, tab],
        out_specs=[row(c) for c in out_w],
        out_shape=[jax.ShapeDtypeStruct((t, c), BF16) for c in out_w],
        compiler_params=_cparams("parallel"),
        name="proj_ab",
    )(x2, w1, gq, gkv, wq, wqr, wkn, wv, cosq, sinq, cosk, sink)


def _key_chunks(qi, tq, tk):
    end = (qi + 1) * tq
    return [(ks, min(tk, end - ks)) for ks in range(0, end, tk)]


def _neg_abs(x):
    return pltpu.bitcast(pltpu.bitcast(x, jnp.uint32) | jnp.uint32(0x80000000), F32)


def _sb_tile(qh, k2, v2, acc, run, diag, suffix):
    tq = qh.shape[0]
    grp = suffix.shape[0]
    z = _dot_nt(qh, k2)
    soft = jnp.log2(1.0 + jnp.exp2(_neg_abs(z)))
    log_beta = jnp.minimum(z, 0.0) - soft
    log_keep = log_beta - z

    def own_keys_only_past(x):
        r = lax.broadcasted_iota(jnp.int32, (tq, tq), 0)
        c = lax.broadcasted_iota(jnp.int32, (tq, tq), 1)
        own = jnp.where(c < r, x[:, -tq:], 0.0)
        return own if x.shape[1] == tq else jnp.concatenate([x[:, :-tq], own], axis=1)

    if diag:
        log_keep = own_keys_only_past(log_keep)
    hi = log_keep.astype(BF16)
    laters = []
    for g in reversed(range(z.shape[1] // grp)):
        cols = slice(g * grp, (g + 1) * grp)
        laters.insert(0, _dot(hi[:, cols], suffix) + run)
        run = run + jnp.sum(log_keep[:, cols], axis=-1, keepdims=True)
    w = jnp.exp2(log_beta + jnp.concatenate(laters, axis=1))
    if diag:
        w = own_keys_only_past(w)
    return acc + _dot(w.astype(BF16), v2), run


def _sb_attn_kernel(q_ref, k_ref, v_ref, o_ref, acc_ref, run_ref, *, tq):
    seq = q_ref.shape[0]
    n_q = seq // tq
    grp = min(SB_GROUP, tq)
    lane = lax.broadcasted_iota(jnp.int32, (1, LANES), 1)
    first = lane < HEAD_DIM
    rg = lax.broadcasted_iota(jnp.int32, (grp, grp), 0)
    cg = lax.broadcasted_iota(jnp.int32, (grp, grp), 1)
    suffix = jnp.where(rg > cg, 1.0, 0.0).astype(BF16)

    def heads_of(q2):
        zero = jnp.zeros_like(q2)
        return jnp.where(first, q2, zero), jnp.where(first, zero, q2)

    for qi in range(n_q):
        qs = qi * tq
        q_heads = heads_of(q_ref[qs:qs + tq, :])
        b0 = max(0, qs - tq)
        w = qs + tq - b0
        accs = []
        for h in range(2):
            acc, run = _sb_tile(q_heads[h], k_ref[b0:b0 + w, :], v_ref[b0:b0 + w, :],
                                jnp.zeros((tq, LANES), F32), jnp.zeros((tq, 1), F32), True, suffix)
            accs.append(acc)
            if b0 > 0:
                acc_ref[h, qs:qs + tq, :] = acc
                run_ref[h, qs:qs + tq, :] = run
        o_ref[qs:qs + tq, :] = jnp.where(first, accs[0], accs[1]).astype(o_ref.dtype)

    def finish(qi, _):
        rows = pl.ds(pl.multiple_of(qi * tq, tq), tq)
        run_a = run_ref[0, rows, :]
        run_b = run_ref[1, rows, :]

        def unfinished(st):
            j, _, ra, _, rb = st
            return (j >= 0) & (jnp.maximum(jnp.max(ra), jnp.max(rb)) > SB_EXIT)

        @pl.when(unfinished((qi - 2, None, run_a, None, run_b)))
        def _():
            q_heads = heads_of(q_ref[rows, :])

            def earlier(st):
                j, acc_a, ra, acc_b, rb = st
                keys = pl.ds(pl.multiple_of(j * tq, tq), tq)
                k2 = k_ref[keys, :]
                v2 = v_ref[keys, :]
                acc_a, ra = _sb_tile(q_heads[0], k2, v2, acc_a, ra, False, suffix)
                acc_b, rb = _sb_tile(q_heads[1], k2, v2, acc_b, rb, False, suffix)
                return j - 1, acc_a, ra, acc_b, rb

            st = lax.while_loop(unfinished, earlier, (qi - 2, acc_ref[0, rows, :], run_a, acc_ref[1, rows, :], run_b))
            o_ref[rows, :] = jnp.where(first, st[1], st[3]).astype(o_ref.dtype)

        return 0

    lax.fori_loop(2, n_q, finish, 0)


def _sb_attention(q, k, v):
    b, s, _ = q.shape
    tq = min(SB_Q_TILE, s)
    assert s % tq == 0
    spec = pl.BlockSpec((None, s, LANES), lambda bi, h: (bi, 0, h))
    return pl.pallas_call(
        functools.partial(_sb_attn_kernel, tq=tq),
        grid=(b, N_HEADS // 2),
        in_specs=[spec, spec, spec],
        out_specs=spec,
        out_shape=jax.ShapeDtypeStruct((b, s, HEADS_W), BF16),
        scratch_shapes=[pltpu.VMEM((2, s, LANES), F32), pltpu.VMEM((2, s, 1), F32)],
        compiler_params=_cparams("parallel", "parallel"),
        name="sb_attn",
    )(q, k, v)


def _softmax_tile(s, v_ones, state):
    m, acc = state
    m_new = jnp.maximum(m, jnp.max(s, axis=-1, keepdims=True))
    p = jnp.exp2(s - m_new)
    acc = jnp.exp2(m - m_new) * acc + _dot(p.astype(BF16), v_ones)
    return m_new, acc


def _softmax_state(tq):
    return jnp.full((tq, 1), NEG_BIG, F32), jnp.zeros((tq, LANES), F32)


def _softmax_finish(state_a, state_b, first):
    acc_a, acc_b = state_a[1], state_b[1]
    out_a = acc_a / pltpu.roll(acc_a, HEAD_DIM, 1)
    out_b = acc_b / pltpu.roll(acc_b, HEAD_DIM, 1)
    return jnp.where(first, out_a, out_b)


def _with_ones(v2, first):
    one = jnp.ones_like(v2)
    return jnp.where(first, v2, one), jnp.where(first, one, v2)


def _mla_attn_kernel(q_ref, k_ref, v_ref, o_ref, *, tq, tk):
    seq = q_ref.shape[0]
    lane = lax.broadcasted_iota(jnp.int32, (1, LANES), 1)
    first = lane < HEAD_DIM

    for qi in range(seq // tq):
        rows = slice(qi * tq, (qi + 1) * tq)
        chunks = _key_chunks(qi, tq, tk)
        last = len(chunks) - 1
        carry = [_softmax_state(tq), _softmax_state(tq)]
        for ci in (last, *range(last)):
            ks, w = chunks[ci]
            keys = slice(ks, ks + w)
            v_ones = _with_ones(v_ref[keys, :], first)
            for h in range(2):
                s = _dot_nt(q_ref[rows, h * LANES:(h + 1) * LANES], k_ref[keys, h * LANES:(h + 1) * LANES])
                if ci == last:
                    r = lax.broadcasted_iota(jnp.int32, (tq, w), 0)
                    c = lax.broadcasted_iota(jnp.int32, (tq, w), 1)
                    s = jnp.where(c <= r + (qi * tq - ks), s, NEG_BIG)
                carry[h] = _softmax_tile(s, v_ones[h], carry[h])
        o_ref[rows, :] = _softmax_finish(carry[0], carry[1], first).astype(o_ref.dtype)


def _mla_attention(qf, kf, v):
    b, s, _ = v.shape
    tq, tk = min(Q_TILE, s), min(K_TILE, s)
    assert s % tq == 0 and tk % tq == 0
    kspec = pl.BlockSpec((None, s, 2 * LANES), lambda bi, h: (bi, 0, h))
    vspec = pl.BlockSpec((None, s, LANES), lambda bi, h: (bi, 0, h))
    return pl.pallas_call(
        functools.partial(_mla_attn_kernel, tq=tq, tk=tk),
        grid=(b, N_HEADS // 2),
        in_specs=[kspec, kspec, vspec],
        out_specs=vspec,
        out_shape=jax.ShapeDtypeStruct((b, s, HEADS_W), BF16),
        compiler_params=_cparams("parallel", "parallel"),
        name="mla_attn",
    )(qf, kf, v)


def _moba_kernel(q_ref, k_ref, v_ref, o_ref, *, tq, tk):
    seq = q_ref.shape[0]
    nblk = seq // MOBA_BLOCK
    lane = lax.broadcasted_iota(jnp.int32, (1, LANES), 1)
    first = lane < HEAD_DIM

    means = [jnp.mean(k_ref[n * MOBA_BLOCK:(n + 1) * MOBA_BLOCK, :].astype(F32), axis=0, keepdims=True)
             for n in range(nblk)]
    km = jnp.concatenate(means + [jnp.zeros((LANES - nblk, LANES), F32)], axis=0)
    k1 = km.astype(BF16)
    rem = km - k1.astype(F32)
    k2 = rem.astype(BF16)
    k3 = (rem - k2.astype(F32)).astype(BF16)

    rowid = lax.broadcasted_iota(jnp.int32, (8, tq), 0)
    for qi in range(seq // tq):
        rows = slice(qi * tq, (qi + 1) * tq)
        q2 = q_ref[rows, :]
        zero = jnp.zeros_like(q2)
        q_heads = (jnp.where(first, q2, zero), jnp.where(first, zero, q2))
        qblk = qi * (tq // MOBA_BLOCK) + lax.broadcasted_iota(jnp.int32, (8, tq), 1) // MOBA_BLOCK
        valid = rowid < qblk
        q_aug = []
        for h in range(2):
            qh = q_heads[h]
            gate_t = _dot_nt(k1, qh) + _dot_nt(k2, qh) + _dot_nt(k3, qh)
            g = jnp.where(valid, gate_t[0:8, :], -jnp.inf)
            beaten = jnp.zeros((8, tq), jnp.int32)
            for m in range(nblk):
                gm = g[m:m + 1, :]
                wins = (gm > g) | ((gm == g) & (rowid > m))
                beaten = beaten + jnp.where(wins, 1, 0)
            usable = (valid & (beaten < MOBA_TOPK)) | (rowid == qblk)
            bias_t = jnp.where(usable, 0.0, NEG_BIG)
            bias_t = jnp.concatenate([bias_t, jnp.zeros((LANES - 8, tq), F32)], axis=0)
            q_aug.append(jnp.concatenate([qh, bias_t.T.astype(BF16)], axis=1))

        chunks = _key_chunks(qi, tq, tk)
        last = len(chunks) - 1
        carry = [_softmax_state(tq), _softmax_state(tq)]
        for ci in (last, *range(last)):
            ks, w = chunks[ci]
            keys = slice(ks, ks + w)
            blk = (ks + lax.broadcasted_iota(jnp.int32, (w, LANES), 0)) // MOBA_BLOCK
            onehot = jnp.where(blk == lax.broadcasted_iota(jnp.int32, (w, LANES), 1), 1.0, 0.0).astype(BF16)
            k_aug = jnp.concatenate([k_ref[keys, :], onehot], axis=1)
            v_ones = _with_ones(v_ref[keys, :], first)
            for h in range(2):
                s = _dot_nt(q_aug[h], k_aug)
                if ci == last:
                    r = lax.broadcasted_iota(jnp.int32, (tq, w), 0)
                    c = lax.broadcasted_iota(jnp.int32, (tq, w), 1)
                    s = jnp.where(c <= r + (qi * tq - ks), s, NEG_BIG)
                carry[h] = _softmax_tile(s, v_ones[h], carry[h])
        o_ref[rows, :] = _softmax_finish(carry[0], carry[1], first).astype(o_ref.dtype)


def _moba_attention(q, k, v):
    b, s, _ = q.shape
    tq, tk = min(Q_TILE, s), min(K_TILE, s)
    assert s % tq == 0 and tk % tq == 0 and tq % MOBA_BLOCK == 0 and s // MOBA_BLOCK <= 8
    spec = pl.BlockSpec((None, s, LANES), lambda bi, h: (bi, 0, h))
    return pl.pallas_call(
        functools.partial(_moba_kernel, tq=tq, tk=tk),
        grid=(b, N_HEADS // 2),
        in_specs=[spec, spec, spec],
        out_specs=spec,
        out_shape=jax.ShapeDtypeStruct((b, s, HEADS_W), BF16),
        compiler_params=_cparams("parallel", "parallel"),
        name="moba_attn",
    )(q, k, v)


def _out_ln_kernel(a_ref, b_ref, x_ref, wa_ref, wb_ref, g_ref, beta_ref, o_ref):
    mix = _dot(a_ref[...], wa_ref[...]) + _dot(b_ref[...], wb_ref[...])
    o_ref[...] = _layer_norm_rows(DN_ALPHA * x_ref[...] + mix, g_ref[...], beta_ref[...])


def _out_ln(a, b, x2, wa, wb, g, beta):
    t = x2.shape[0]
    tm = min(ROW_TILE, t)
    row = lambda c: pl.BlockSpec((tm, c), lambda i: (i, 0))
    return pl.pallas_call(
        _out_ln_kernel,
        grid=(t // tm,),
        in_specs=[row(a.shape[1]), row(b.shape[1]), row(D_MODEL), _full_spec(wa.shape), _full_spec(wb.shape),
                  _full_spec(g.shape), _full_spec(beta.shape)],
        out_specs=row(D_MODEL),
        out_shape=jax.ShapeDtypeStruct((t, D_MODEL), F32),
        compiler_params=_cparams("parallel"),
        name="out_ln",
    )(a, b, x2, wa, wb, g, beta)


def _ffn_ln_kernel(x_ref, wg_ref, wu_ref, wd_ref, g_ref, beta_ref, o_ref):
    x = x_ref[...]
    xb = x.astype(BF16)
    gate = _dot(xb, wg_ref[...])
    up = _dot(xb, wu_ref[...])
    h = (gate * (1.0 / (1.0 + jnp.exp(-gate))) * up).astype(BF16)
    y = _dot(h, wd_ref[...])
    o_ref[...] = _layer_norm_rows(DN_ALPHA * x + y, g_ref[...], beta_ref[...])


def _ffn_ln(x2, wg, wu, wd, g, beta):
    t = x2.shape[0]
    tm = min(ROW_TILE, t)
    row = pl.BlockSpec((tm, D_MODEL), lambda i: (i, 0))
    resident = lambda shape: pl.BlockSpec(shape, lambda i: (0, 0), pipeline_mode=pl.Buffered(1))
    return pl.pallas_call(
        _ffn_ln_kernel,
        grid=(t // tm,),
        in_specs=[row, resident(wg.shape), resident(wu.shape), resident(wd.shape),
                  _full_spec(g.shape), _full_spec(beta.shape)],
        out_specs=row,
        out_shape=jax.ShapeDtypeStruct((t, D_MODEL), F32),
        compiler_params=_cparams("parallel"),
        name="ffn_ln",
    )(x2, wg, wu, wd, g, beta)


def _proj_cd_kernel(x_ref, w_ref, u_ref, q_ref, k_ref, v_ref):
    p = _dot(x_ref[...].astype(BF16), w_ref[...])
    u_ref[...] = p[:, 0:512].astype(BF16)
    q_ref[...] = (p[:, 512:1024] * (HEAD_DIM ** -0.5 * LOG2_E)).astype(BF16)
    k_ref[...] = p[:, 1024:1536].astype(BF16)
    v_ref[...] = p[:, 1536:2048].astype(BF16)


def _proj_cd(x2, w):
    t = x2.shape[0]
    tm = min(ROW_TILE, t)
    row = lambda c: pl.BlockSpec((tm, c), lambda i: (i, 0))
    return pl.pallas_call(
        _proj_cd_kernel,
        grid=(t // tm,),
        in_specs=[row(D_MODEL), _full_spec(w.shape)],
        out_specs=[row(HEADS_W)] * 4,
        out_shape=[jax.ShapeDtypeStruct((t, HEADS_W), BF16)] * 4,
        compiler_params=_cparams("parallel"),
        name="proj_cd",
    )(x2, w)


S5_SCAN_LANES = 1024


def _s5_kernel(u_ref, bmat_ref, cmat_ref, are_ref, aim_ref, d_ref, wglu_ref, bglu_ref, o_ref,
               bu_ref, state_ref):
    nb, ts, _ = u_ref.shape
    rows = ts * nb

    @pl.when(pl.program_id(0) == 0)
    def _():
        state_ref[...] = jnp.zeros_like(state_ref)

    uf = jnp.swapaxes(u_ref[...].astype(F32), 0, 1).reshape(rows, S5_CHANNELS)
    ub = uf.astype(BF16)
    n_chunks = S5_CHANNELS // LANES
    cw = S5_WIDTH // n_chunks
    for j in range(n_chunks):
        bu = _dot(ub[:, j * LANES:(j + 1) * LANES], bmat_ref[j])
        bu_ref[:, j * cw:(j + 1) * cw] = bu[:, :cw]
        bu_ref[:, S5_WIDTH + j * cw:S5_WIDTH + (j + 1) * cw] = bu[:, cw:]

    for w in range(S5_WIDTH // S5_SCAN_LANES):
        re_cols = slice(w * S5_SCAN_LANES, (w + 1) * S5_SCAN_LANES)
        im_cols = slice(S5_WIDTH + w * S5_SCAN_LANES, S5_WIDTH + (w + 1) * S5_SCAN_LANES)
        ar = jnp.broadcast_to(are_ref[:, re_cols], (nb, S5_SCAN_LANES))
        ai = jnp.broadcast_to(aim_ref[:, re_cols], (nb, S5_SCAN_LANES))

        def step(t, carry):
            xr, xi = carry
            r0 = pl.multiple_of(t * nb, nb)
            br = bu_ref[pl.ds(r0, nb), re_cols]
            bi = bu_ref[pl.ds(r0, nb), im_cols]
            nr = ar * xr - ai * xi + br
            ni = ar * xi + ai * xr + bi
            bu_ref[pl.ds(r0, nb), re_cols] = nr
            bu_ref[pl.ds(r0, nb), im_cols] = ni
            return nr, ni

        xr, xi = lax.fori_loop(0, ts, step, (state_ref[:, re_cols], state_ref[:, im_cols]), unroll=8)
        state_ref[:, re_cols] = xr
        state_ref[:, im_cols] = xi

    ys = []
    for j in range(n_chunks):
        xin = jnp.concatenate([bu_ref[:, j * cw:(j + 1) * cw],
                               bu_ref[:, S5_WIDTH + j * cw:S5_WIDTH + (j + 1) * cw]], axis=1).astype(BF16)
        ys.append(_dot(xin, cmat_ref[j]))
    y = jnp.concatenate(ys, axis=1) + d_ref[...] * uf
    z = 0.5 * y * (1.0 + jnp.tanh(math.sqrt(2.0 / math.pi) * (y + 0.044715 * (y * y * y))))
    gate = _dot(z.astype(BF16), wglu_ref[...]) + bglu_ref[...]
    out = z * (1.0 / (1.0 + jnp.exp(-gate)))
    o_ref[...] = jnp.swapaxes(out.reshape(ts, nb, S5_CHANNELS), 0, 1).astype(o_ref.dtype)


def _s5(u, bmat, cmat, are, aim, d, wglu, bglu):
    b, s, _ = u.shape
    ts = min(S5_TIME_TILE, s)
    blk = pl.BlockSpec((b, ts, S5_CHANNELS), lambda i: (0, i, 0))
    return pl.pallas_call(
        _s5_kernel,
        grid=(s // ts,),
        in_specs=[blk, _full_spec(bmat.shape), _full_spec(cmat.shape), _full_spec(are.shape),
                  _full_spec(aim.shape), _full_spec(d.shape), _full_spec(wglu.shape), _full_spec(bglu.shape)],
        out_specs=blk,
        out_shape=jax.ShapeDtypeStruct((b, s, S5_CHANNELS), BF16),
        scratch_shapes=[pltpu.VMEM((ts * b, 2 * S5_WIDTH), F32), pltpu.VMEM((b, 2 * S5_WIDTH), F32)],
        compiler_params=_cparams("arbitrary"),
        name="s5_scan",
    )(u, bmat, cmat, are, aim, d, wglu, bglu)


def _rotate_half_cols(w):
    half = w.shape[-1] // 2
    return jnp.concatenate([-w[..., half:], w[..., :half]], axis=-1)


def _layer0_params(w_in, q_norm, w_uq, kv_norm, w_ukv, seq):
    d = w_in.shape[0]
    w_rope = w_in[:, 2048:2080]
    pad_l = jnp.zeros((d, MLA_NOPE), F32)
    pad_r = jnp.zeros((d, LANES - MLA_NOPE - MLA_ROPE), F32)
    w1 = jnp.concatenate([w_in[:, :2048], pad_l, w_rope, pad_r, pad_l, _rotate_half_cols(w_rope), pad_r], axis=1)

    wq3 = w_uq.reshape(MLA_RANK, N_HEADS, MLA_NOPE + MLA_ROPE)
    nope, rope = wq3[..., :MLA_NOPE], wq3[..., MLA_NOPE:]
    z_nope = jnp.zeros_like(nope)
    z_pad = jnp.zeros((MLA_RANK, N_HEADS, LANES - MLA_NOPE - MLA_ROPE), F32)
    wq = jnp.concatenate([nope, rope, z_pad], axis=-1).reshape(MLA_RANK, N_HEADS * LANES)
    wqr = jnp.concatenate([z_nope, _rotate_half_cols(rope), z_pad], axis=-1).reshape(MLA_RANK, N_HEADS * LANES)

    wkv3 = w_ukv.reshape(MLA_RANK, N_HEADS, 2 * HEAD_DIM)
    wkn = jnp.concatenate([wkv3[..., :MLA_NOPE], jnp.zeros((MLA_RANK, N_HEADS, LANES - MLA_NOPE), F32)],
                          axis=-1).reshape(MLA_RANK, N_HEADS * LANES)
    wv = wkv3[..., MLA_NOPE:].reshape(MLA_RANK, HEADS_W)

    half = MLA_ROPE // 2
    freqs = ROPE_BASE ** (-jnp.arange(half, dtype=F32) / half)
    ang = jnp.arange(seq, dtype=F32)[:, None] * freqs
    cos = jnp.concatenate([jnp.cos(ang)] * 2, axis=1)
    sin = jnp.concatenate([jnp.sin(ang)] * 2, axis=1)
    ones = jnp.ones((seq, MLA_NOPE), F32)
    zl = jnp.zeros((seq, MLA_NOPE), F32)
    zr = jnp.zeros((seq, LANES - MLA_NOPE - MLA_ROPE), F32)
    scale = (MLA_NOPE + MLA_ROPE) ** -0.5 * LOG2_E
    cosq = jnp.concatenate([ones, cos, zr], axis=1) * scale
    sinq = jnp.concatenate([zl, sin, zr], axis=1) * scale
    cosk = jnp.concatenate([zl, cos, zr], axis=1)
    sink = jnp.concatenate([zl, sin, zr], axis=1)
    return (w1.astype(BF16), q_norm.reshape(1, -1), kv_norm.reshape(1, -1), wq.astype(BF16), wqr.astype(BF16),
            wkn.astype(BF16), wv.astype(BF16), cosq, sinq, cosk, sink)


def _s5_params(lam_re, lam_im, log_dt, b_re, b_im, c_re, c_im):
    dt = jnp.exp(log_dt)[:, None]
    mag = jnp.exp(lam_re * dt)
    ab_re, ab_im = mag * jnp.cos(lam_im * dt), mag * jnp.sin(lam_im * dt)
    den = lam_re * lam_re + lam_im * lam_im
    nr, ni = ab_re - 1.0, ab_im
    f_re, f_im = (nr * lam_re + ni * lam_im) / den, (ni * lam_re - nr * lam_im) / den
    bb_re = f_re[..., None] * b_re - f_im[..., None] * b_im
    bb_im = f_re[..., None] * b_im + f_im[..., None] * b_re

    gpc = LANES // S5_GROUP
    n_chunks = S5_GROUPS // gpc
    eye = jnp.eye(gpc, dtype=F32)

    def in_blocks(bb):
        t = bb.reshape(n_chunks, gpc, S5_STATE, S5_GROUP)
        return jnp.einsum('cgph,gk->cghkp', t, eye).reshape(n_chunks, gpc * S5_GROUP, gpc * S5_STATE)

    def out_blocks(cc):
        t = cc.reshape(n_chunks, gpc, S5_GROUP, S5_STATE)
        return jnp.einsum('cghp,gk->cgpkh', t, eye).reshape(n_chunks, gpc * S5_STATE, gpc * S5_GROUP)

    bmat = jnp.concatenate([in_blocks(bb_re), in_blocks(bb_im)], axis=2)
    cmat = jnp.concatenate([out_blocks(c_re), -out_blocks(c_im)], axis=1)
    a_re = ab_re.reshape(1, S5_WIDTH)
    a_im = ab_im.reshape(1, S5_WIDTH)
    return bmat.astype(BF16), cmat.astype(BF16), a_re, a_im


def kernel(x, ab_w_in, ab_q_norm, ab_w_uq, ab_kv_norm, ab_w_ukv, ab_w_out, cd_w_in, s5_lambda_re, s5_lambda_im,
           s5_log_dt, s5_b_re, s5_b_im, s5_c_re, s5_c_im, s5_d, s5_w_glu, s5_b_glu, cd_w_out, ln1_g, ln1_b, ln2_g,
           ln2_b, ffn_w_gate, ffn_w_up, ffn_w_down):
    b, s, d = x.shape
    t = b * s
    x2 = x.reshape(t, d)
    vec = lambda a: a.reshape(1, -1)

    p0 = _layer0_params(ab_w_in[0], ab_q_norm[0], ab_w_uq[0], ab_kv_norm[0], ab_w_ukv[0], s)
    qsb, ksb, vsb, qf, kf, vm = _proj_ab(x2, *p0, seq=s)
    sh = lambda a: a.reshape(b, s, a.shape[-1])
    o_sb = _sb_attention(sh(qsb), sh(ksb), sh(vsb)).reshape(t, HEADS_W)
    o_mla = _mla_attention(sh(qf), sh(kf), sh(vm)).reshape(t, HEADS_W)
    w_out = ab_w_out[0].astype(BF16)
    x2 = _out_ln(o_sb, o_mla, x2, w_out[:HEADS_W], w_out[HEADS_W:], vec(ln1_g[0]), vec(ln1_b[0]))
    x2 = _ffn_ln(x2, ffn_w_gate[0].astype(BF16), ffn_w_up[0].astype(BF16), ffn_w_down[0].astype(BF16),
                 vec(ln2_g[0]), vec(ln2_b[0]))

    u, q, k, v = _proj_cd(x2, cd_w_in[0].astype(BF16))
    bmat, cmat, a_re, a_im = _s5_params(s5_lambda_re[0], s5_lambda_im[0], s5_log_dt[0], s5_b_re[0], s5_b_im[0],
                                        s5_c_re[0], s5_c_im[0])
    o_s5 = _s5(sh(u), bmat, cmat, a_re, a_im, vec(s5_d[0]), s5_w_glu[0].astype(BF16), vec(s5_b_glu[0]))
    o_moba = _moba_attention(sh(q), sh(k), sh(v))
    w_out = cd_w_out[0].astype(BF16)
    x2 = _out_ln(o_s5.reshape(t, S5_CHANNELS), o_moba.reshape(t, HEADS_W), x2, w_out[:S5_CHANNELS],
                 w_out[S5_CHANNELS:], vec(ln1_g[1]), vec(ln1_b[1]))
    x2 = _ffn_ln(x2, ffn_w_gate[1].astype(BF16), ffn_w_up[1].astype(BF16), ffn_w_down[1].astype(BF16),
                 vec(ln2_g[1]), vec(ln2_b[1]))
    return x2.reshape(b, s, d)
```

```python
import functools
import math

import jax
import jax.numpy as jnp
from jax import lax
from jax.experimental import pallas as pl
from jax.experimental.pallas import tpu as pltpu

F32 = jnp.float32
BF16 = jnp.bfloat16

D_MODEL = 1024
HEAD_DIM = 64
N_HEADS = 8
HEADS_W = N_HEADS * HEAD_DIM
MLA_RANK = 256
MLA_NOPE = 64
MLA_ROPE = 32
ROPE_BASE = 10000.0
S5_CHANNELS = 512
S5_GROUP = 16
S5_GROUPS = 32
S5_STATE = 64
S5_WIDTH = S5_GROUPS * S5_STATE
MOBA_BLOCK = 256
MOBA_TOPK = 3
D_FF = 2816
DEPTH = 2
DN_ALPHA = (2 * DEPTH) ** 0.25
LN_EPS = 1e-5
RMS_EPS = 1e-6

LANES = 128
NEG_BIG = -1e30
LOG2_E = math.log2(math.e)
VMEM_LIMIT = 56 * 1024 * 1024

ROW_TILE = 512
SB_GROUP = 256
SB_Q_TILE = 256
SB_EXIT = -176.0
Q_TILE = 512
K_TILE = 512
S5_TIME_TILE = 64


def _cparams(*sem):
    return pltpu.CompilerParams(dimension_semantics=sem, vmem_limit_bytes=VMEM_LIMIT)


def _full_spec(shape):
    nd = len(shape)
    return pl.BlockSpec(shape, lambda *_: (0,) * nd)


def _dot(a, b):
    return jnp.dot(a, b, preferred_element_type=F32)


def _dot_nt(a, b):
    return lax.dot_general(a, b, (((1,), (1,)), ((), ())), preferred_element_type=F32)


def _layer_norm_rows(r, g, b):
    mu = jnp.mean(r, axis=-1, keepdims=True)
    d = r - mu
    var = jnp.mean(d * d, axis=-1, keepdims=True)
    return d * lax.rsqrt(var + LN_EPS) * g + b


def _proj_ab_kernel(x_ref, w1_ref, gq_ref, gkv_ref, wq_ref, wqr_ref, wkn_ref, wv_ref,
                    cosq_ref, sinq_ref, cosk_ref, sink_ref,
                    qsb_ref, ksb_ref, vsb_ref, qf_ref, kf_ref, vm_ref):
    xb = x_ref[...].astype(BF16)
    p = _dot(xb, w1_ref[...])
    qsb_ref[...] = (p[:, 0:512] * (HEAD_DIM ** -0.5 * LOG2_E)).astype(BF16)
    ksb_ref[...] = p[:, 512:1024].astype(BF16)
    vsb_ref[...] = p[:, 1024:1536].astype(BF16)
    cq = p[:, 1536:1792]
    ckv = p[:, 1792:2048]
    rope = p[:, 2048:2176]
    rope_rot = p[:, 2176:2304]

    cqn = cq * lax.rsqrt(jnp.mean(cq * cq, axis=-1, keepdims=True) + RMS_EPS) * gq_ref[...]
    ckvn = ckv * lax.rsqrt(jnp.mean(ckv * ckv, axis=-1, keepdims=True) + RMS_EPS) * gkv_ref[...]
    cqb = cqn.astype(BF16)
    ckvb = ckvn.astype(BF16)

    cosq = jnp.concatenate([cosq_ref[...]] * N_HEADS, axis=1)
    sinq = jnp.concatenate([sinq_ref[...]] * N_HEADS, axis=1)
    qf = _dot(cqb, wq_ref[...]) * cosq + _dot(cqb, wqr_ref[...]) * sinq
    qf_ref[...] = qf.astype(BF16)

    kpe = rope * cosk_ref[...] + rope_rot * sink_ref[...]
    kf = _dot(ckvb, wkn_ref[...]) + jnp.concatenate([kpe] * N_HEADS, axis=1)
    kf_ref[...] = kf.astype(BF16)
    vm_ref[...] = _dot(ckvb, wv_ref[...]).astype(BF16)


def _proj_ab(x2, w1, gq, gkv, wq, wqr, wkn, wv, cosq, sinq, cosk, sink, seq):
    t = x2.shape[0]
    tm = min(ROW_TILE, seq)
    n_seq_tiles = seq // tm
    row = lambda c: pl.BlockSpec((tm, c), lambda i: (i, 0))
    tab = pl.BlockSpec((tm, LANES), lambda i: (i % n_seq_tiles, 0))
    out_w = (HEADS_W, HEADS_W, HEADS_W, N_HEADS * LANES, N_HEADS * LANES, HEADS_W)
    return pl.pallas_call(
        _proj_ab_kernel,
        grid=(t // tm,),
        in_specs=[row(D_MODEL), _full_spec(w1.shape), _full_spec(gq.shape), _full_spec(gkv.shape),
                  _full_spec(wq.shape), _full_spec(wqr.shape), _full_spec(wkn.shape), _full_spec(wv.shape),
                  tab, tab, tab, tab],
        out_specs=[row(c) for c in out_w],
        out_shape=[jax.ShapeDtypeStruct((t, c), BF16) for c in out_w],
        compiler_params=_cparams("parallel"),
        name="proj_ab",
    )(x2, w1, gq, gkv, wq, wqr, wkn, wv, cosq, sinq, cosk, sink)


def _key_chunks(qi, tq, tk):
    end = (qi + 1) * tq
    return [(ks, min(tk, end - ks)) for ks in range(0, end, tk)]


def _sb_tile(qh, k2, v2, acc, run, diag, suffix):
    tq = qh.shape[0]
    grp = suffix.shape[0]
    z = _dot_nt(qh, k2)
    soft = jnp.log2(1.0 + jnp.exp2(-jnp.abs(z)))
    log_beta = jnp.minimum(z, 0.0) - soft
    log_keep = log_beta - z

    def own_keys_only_past(x):
        r = lax.broadcasted_iota(jnp.int32, (tq, tq), 0)
        c = lax.broadcasted_iota(jnp.int32, (tq, tq), 1)
        own = jnp.where(c < r, x[:, -tq:], 0.0)
        return own if x.shape[1] == tq else jnp.concatenate([x[:, :-tq], own], axis=1)

    if diag:
        log_keep = own_keys_only_past(log_keep)
    hi = log_keep.astype(BF16)
    laters = []
    for g in reversed(range(z.shape[1] // grp)):
        cols = slice(g * grp, (g + 1) * grp)
        laters.insert(0, _dot(hi[:, cols], suffix) + run)
        run = run + jnp.sum(log_keep[:, cols], axis=-1, keepdims=True)
    w = jnp.exp2(log_beta + jnp.concatenate(laters, axis=1))
    if diag:
        w = own_keys_only_past(w)
    return acc + _dot(w.astype(BF16), v2), run


def _sb_attn_kernel(q_ref, k_ref, v_ref, o_ref, acc_ref, run_ref, *, tq):
    seq = q_ref.shape[0]
    n_q = seq // tq
    grp = min(SB_GROUP, tq)
    lane = lax.broadcasted_iota(jnp.int32, (1, LANES), 1)
    first = lane < HEAD_DIM
    rg = lax.broadcasted_iota(jnp.int32, (grp, grp), 0)
    cg = lax.broadcasted_iota(jnp.int32, (grp, grp), 1)
    suffix = jnp.where(rg > cg, 1.0, 0.0).astype(BF16)

    def heads_of(q2):
        zero = jnp.zeros_like(q2)
        return jnp.where(first, q2, zero), jnp.where(first, zero, q2)

    for qi in range(n_q):
        qs = qi * tq
        q_heads = heads_of(q_ref[qs:qs + tq, :])
        b0 = max(0, qs - tq)
        w = qs + tq - b0
        accs = []
        for h in range(2):
            acc, run = _sb_tile(q_heads[h], k_ref[b0:b0 + w, :], v_ref[b0:b0 + w, :],
                                jnp.zeros((tq, LANES), F32), jnp.zeros((tq, 1), F32), True, suffix)
            accs.append(acc)
            if b0 > 0:
                acc_ref[h, qs:qs + tq, :] = acc
                run_ref[h, qs:qs + tq, :] = run
        o_ref[qs:qs + tq, :] = jnp.where(first, accs[0], accs[1]).astype(o_ref.dtype)

    def finish(qi, _):
        rows = pl.ds(pl.multiple_of(qi * tq, tq), tq)
        run_a = run_ref[0, rows, :]
        run_b = run_ref[1, rows, :]

        def unfinished(st):
            j, _, ra, _, rb = st
            return (j >= 0) & (jnp.maximum(jnp.max(ra), jnp.max(rb)) > SB_EXIT)

        @pl.when(unfinished((qi - 2, None, run_a, None, run_b)))
        def _():
            q_heads = heads_of(q_ref[rows, :])

            def earlier(st):
                j, acc_a, ra, acc_b, rb = st
                keys = pl.ds(pl.multiple_of(j * tq, tq), tq)
                k2 = k_ref[keys, :]
                v2 = v_ref[keys, :]
                acc_a, ra = _sb_tile(q_heads[0], k2, v2, acc_a, ra, False, suffix)
                acc_b, rb = _sb_tile(q_heads[1], k2, v2, acc_b, rb, False, suffix)
                return j - 1, acc_a, ra, acc_b, rb

            st = lax.while_loop(unfinished, earlier, (qi - 2, acc_ref[0, rows, :], run_a, acc_ref[1, rows, :], run_b))
            o_ref[rows, :] = jnp.where(first, st[1], st[3]).astype(o_ref.dtype)

        return 0

    lax.fori_loop(2, n_q, finish, 0)


def _sb_attention(q, k, v):
    b, s, _ = q.shape
    tq = min(SB_Q_TILE, s)
    assert s % tq == 0
    spec = pl.BlockSpec((None, s, LANES), lambda bi, h: (bi, 0, h))
    return pl.pallas_call(
        functools.partial(_sb_attn_kernel, tq=tq),
        grid=(b, N_HEADS // 2),
        in_specs=[spec, spec, spec],
        out_specs=spec,
        out_shape=jax.ShapeDtypeStruct((b, s, HEADS_W), BF16),
        scratch_shapes=[pltpu.VMEM((2, s, LANES), F32), pltpu.VMEM((2, s, 1), F32)],
        compiler_params=_cparams("parallel", "parallel"),
        name="sb_attn",
    )(q, k, v)


def _softmax_tile(s, v_ones, state):
    m, acc = state
    m_new = jnp.maximum(m, jnp.max(s, axis=-1, keepdims=True))
    p = jnp.exp2(s - m_new)
    acc = jnp.exp2(m - m_new) * acc + _dot(p.astype(BF16), v_ones)
    return m_new, acc


def _softmax_state(tq):
    return jnp.full((tq, 1), NEG_BIG, F32), jnp.zeros((tq, LANES), F32)


def _softmax_finish(state_a, state_b, first):
    acc_a, acc_b = state_a[1], state_b[1]
    out_a = acc_a / pltpu.roll(acc_a, HEAD_DIM, 1)
    out_b = acc_b / pltpu.roll(acc_b, HEAD_DIM, 1)
    return jnp.where(first, out_a, out_b)


def _with_ones(v2, first):
    one = jnp.ones_like(v2)
    return jnp.where(first, v2, one), jnp.where(first, one, v2)


def _mla_attn_kernel(q_ref, k_ref, v_ref, o_ref, *, tq, tk):
    seq = q_ref.shape[0]
    lane = lax.broadcasted_iota(jnp.int32, (1, LANES), 1)
    first = lane < HEAD_DIM

    for qi in range(seq // tq):
        rows = slice(qi * tq, (qi + 1) * tq)
        chunks = _key_chunks(qi, tq, tk)
        last = len(chunks) - 1
        carry = [_softmax_state(tq), _softmax_state(tq)]
        for ci in (last, *range(last)):
            ks, w = chunks[ci]
            keys = slice(ks, ks + w)
            v_ones = _with_ones(v_ref[keys, :], first)
            for h in range(2):
                s = _dot_nt(q_ref[rows, h * LANES:(h + 1) * LANES], k_ref[keys, h * LANES:(h + 1) * LANES])
                if ci == last:
                    r = lax.broadcasted_iota(jnp.int32, (tq, w), 0)
                    c = lax.broadcasted_iota(jnp.int32, (tq, w), 1)
                    s = jnp.where(c <= r + (qi * tq - ks), s, NEG_BIG)
                carry[h] = _softmax_tile(s, v_ones[h], carry[h])
        o_ref[rows, :] = _softmax_finish(carry[0], carry[1], first).astype(o_ref.dtype)


def _mla_attention(qf, kf, v):
    b, s, _ = v.shape
    tq, tk = min(Q_TILE, s), min(K_TILE, s)
    assert s % tq == 0 and tk % tq == 0
    kspec = pl.BlockSpec((None, s, 2 * LANES), lambda bi, h: (bi, 0, h))
    vspec = pl.BlockSpec((None, s, LANES), lambda bi, h: (bi, 0, h))
    return pl.pallas_call(
        functools.partial(_mla_attn_kernel, tq=tq, tk=tk),
        grid=(b, N_HEADS // 2),
        in_specs=[kspec, kspec, vspec],
        out_specs=vspec,
        out_shape=jax.ShapeDtypeStruct((b, s, HEADS_W), BF16),
        compiler_params=_cparams("parallel", "parallel"),
        name="mla_attn",
    )(qf, kf, v)


def _moba_kernel(q_ref, k_ref, v_ref, o_ref, *, tq, tk):
    seq = q_ref.shape[0]
    nblk = seq // MOBA_BLOCK
    lane = lax.broadcasted_iota(jnp.int32, (1, LANES), 1)
    first = lane < HEAD_DIM

    means = [jnp.mean(k_ref[n * MOBA_BLOCK:(n + 1) * MOBA_BLOCK, :].astype(F32), axis=0, keepdims=True)
             for n in range(nblk)]
    km = jnp.concatenate(means + [jnp.zeros((LANES - nblk, LANES), F32)], axis=0)
    k1 = km.astype(BF16)
    rem = km - k1.astype(F32)
    k2 = rem.astype(BF16)
    k3 = (rem - k2.astype(F32)).astype(BF16)

    rowid = lax.broadcasted_iota(jnp.int32, (8, tq), 0)
    for qi in range(seq // tq):
        rows = slice(qi * tq, (qi + 1) * tq)
        q2 = q_ref[rows, :]
        zero = jnp.zeros_like(q2)
        q_heads = (jnp.where(first, q2, zero), jnp.where(first, zero, q2))
        qblk = qi * (tq // MOBA_BLOCK) + lax.broadcasted_iota(jnp.int32, (8, tq), 1) // MOBA_BLOCK
        valid = rowid < qblk
        q_aug = []
        for h in range(2):
            qh = q_heads[h]
            gate_t = _dot_nt(k1, qh) + _dot_nt(k2, qh) + _dot_nt(k3, qh)
            g = jnp.where(valid, gate_t[0:8, :], -jnp.inf)
            beaten = jnp.zeros((8, tq), jnp.int32)
            for m in range(nblk):
                gm = g[m:m + 1, :]
                wins = (gm > g) | ((gm == g) & (rowid > m))
                beaten = beaten + jnp.where(wins, 1, 0)
            usable = (valid & (beaten < MOBA_TOPK)) | (rowid == qblk)
            bias_t = jnp.where(usable, 0.0, NEG_BIG)
            bias_t = jnp.concatenate([bias_t, jnp.zeros((LANES - 8, tq), F32)], axis=0)
            q_aug.append(jnp.concatenate([qh, bias_t.T.astype(BF16)], axis=1))

        chunks = _key_chunks(qi, tq, tk)
        last = len(chunks) - 1
        carry = [_softmax_state(tq), _softmax_state(tq)]
        for ci in (last, *range(last)):
            ks, w = chunks[ci]
            keys = slice(ks, ks + w)
            blk = (ks + lax.broadcasted_iota(jnp.int32, (w, LANES), 0)) // MOBA_BLOCK
            onehot = jnp.where(blk == lax.broadcasted_iota(jnp.int32, (w, LANES), 1), 1.0, 0.0).astype(BF16)
            k_aug = jnp.concatenate([k_ref[keys, :], onehot], axis=1)
            v_ones = _with_ones(v_ref[keys, :], first)
            for h in range(2):
                s = _dot_nt(q_aug[h], k_aug)
                if ci == last:
                    r = lax.broadcasted_iota(jnp.int32, (tq, w), 0)
                    c = lax.broadcasted_iota(jnp.int32, (tq, w), 1)
                    s = jnp.where(c <= r + (qi * tq - ks), s, NEG_BIG)
                carry[h] = _softmax_tile(s, v_ones[h], carry[h])
        o_ref[rows, :] = _softmax_finish(carry[0], carry[1], first).astype(o_ref.dtype)


def _moba_attention(q, k, v):
    b, s, _ = q.shape
    tq, tk = min(Q_TILE, s), min(K_TILE, s)
    assert s % tq == 0 and tk % tq == 0 and tq % MOBA_BLOCK == 0 and s // MOBA_BLOCK <= 8
    spec = pl.BlockSpec((None, s, LANES), lambda bi, h: (bi, 0, h))
    return pl.pallas_call(
        functools.partial(_moba_kernel, tq=tq, tk=tk),
        grid=(b, N_HEADS // 2),
        in_specs=[spec, spec, spec],
        out_specs=spec,
        out_shape=jax.ShapeDtypeStruct((b, s, HEADS_W), BF16),
        compiler_params=_cparams("parallel", "parallel"),
        name="moba_attn",
    )(q, k, v)


def _tail_kernel(a_ref, b_ref, x_ref, wa_ref, wb_ref, g1_ref, beta1_ref, wg_ref, wu_ref, wd_ref, g2_ref, beta2_ref,
                 o_ref):
    mix = _dot(a_ref[...], wa_ref[...]) + _dot(b_ref[...], wb_ref[...])
    x1 = _layer_norm_rows(DN_ALPHA * x_ref[...] + mix, g1_ref[...], beta1_ref[...])
    xb = x1.astype(BF16)
    gate = _dot(xb, wg_ref[...])
    up = _dot(xb, wu_ref[...])
    h = (gate * (1.0 / (1.0 + jnp.exp(-gate))) * up).astype(BF16)
    y = _dot(h, wd_ref[...])
    o_ref[...] = _layer_norm_rows(DN_ALPHA * x1 + y, g2_ref[...], beta2_ref[...])


def _layer_tail(a, b, x2, wa, wb, g1, beta1, wg, wu, wd, g2, beta2):
    t = x2.shape[0]
    tm = min(ROW_TILE, t)
    row = lambda c: pl.BlockSpec((tm, c), lambda i: (i, 0))
    resident = lambda arr: pl.BlockSpec(arr.shape, lambda i: (0, 0), pipeline_mode=pl.Buffered(1))
    return pl.pallas_call(
        _tail_kernel,
        grid=(t // tm,),
        in_specs=[row(a.shape[1]), row(b.shape[1]), row(D_MODEL), resident(wa), resident(wb), resident(g1),
                  resident(beta1), resident(wg), resident(wu), resident(wd), resident(g2), resident(beta2)],
        out_specs=row(D_MODEL),
        out_shape=jax.ShapeDtypeStruct((t, D_MODEL), F32),
        compiler_params=_cparams("parallel"),
        name="layer_tail",
    )(a, b, x2, wa, wb, g1, beta1, wg, wu, wd, g2, beta2)


def _proj_cd_kernel(x_ref, w_ref, u_ref, q_ref, k_ref, v_ref):
    p = _dot(x_ref[...].astype(BF16), w_ref[...])
    u_ref[...] = p[:, 0:512].astype(BF16)
    q_ref[...] = (p[:, 512:1024] * (HEAD_DIM ** -0.5 * LOG2_E)).astype(BF16)
    k_ref[...] = p[:, 1024:1536].astype(BF16)
    v_ref[...] = p[:, 1536:2048].astype(BF16)


def _proj_cd(x2, w):
    t = x2.shape[0]
    tm = min(ROW_TILE, t)
    row = lambda c: pl.BlockSpec((tm, c), lambda i: (i, 0))
    return pl.pallas_call(
        _proj_cd_kernel,
        grid=(t // tm,),
        in_specs=[row(D_MODEL), _full_spec(w.shape)],
        out_specs=[row(HEADS_W)] * 4,
        out_shape=[jax.ShapeDtypeStruct((t, HEADS_W), BF16)] * 4,
        compiler_params=_cparams("parallel"),
        name="proj_cd",
    )(x2, w)


def _s5_kernel(u_ref, bmat_ref, cmat_ref, are_ref, aim_ref, d_ref, wglu_ref, bglu_ref, o_ref,
               x_ref, state_ref):
    nb, ts, _ = u_ref.shape
    rows = ts * nb

    @pl.when(pl.program_id(0) == 0)
    def _():
        state_ref[...] = jnp.zeros_like(state_ref)

    uf = jnp.swapaxes(u_ref[...].astype(F32), 0, 1).reshape(rows, S5_CHANNELS)
    ub = uf.astype(BF16)
    n_chunks = S5_CHANNELS // LANES
    cw = S5_WIDTH // n_chunks
    ys = []
    for j in range(n_chunks):
        re_cols = slice(2 * j * cw, (2 * j + 1) * cw)
        im_cols = slice((2 * j + 1) * cw, (2 * j + 2) * cw)
        x_ref[:, 2 * j * cw:(2 * j + 2) * cw] = _dot(ub[:, j * LANES:(j + 1) * LANES], bmat_ref[j])
        ar = jnp.broadcast_to(are_ref[:, j * cw:(j + 1) * cw], (nb, cw))
        ai = jnp.broadcast_to(aim_ref[:, j * cw:(j + 1) * cw], (nb, cw))
        xr = state_ref[:, re_cols]
        xi = state_ref[:, im_cols]
        for t in range(ts):
            step = slice(t * nb, (t + 1) * nb)
            nr = ar * xr - ai * xi + x_ref[step, re_cols]
            ni = ar * xi + ai * xr + x_ref[step, im_cols]
            x_ref[step, re_cols] = nr
            x_ref[step, im_cols] = ni
            xr, xi = nr, ni
        state_ref[:, re_cols] = xr
        state_ref[:, im_cols] = xi
        ys.append(_dot(x_ref[:, 2 * j * cw:(2 * j + 2) * cw].astype(BF16), cmat_ref[j]))
    y = jnp.concatenate(ys, axis=1) + d_ref[...] * uf
    z = 0.5 * y * (1.0 + jnp.tanh(math.sqrt(2.0 / math.pi) * (y + 0.044715 * (y * y * y))))
    gate = _dot(z.astype(BF16), wglu_ref[...]) + bglu_ref[...]
    out = z * (1.0 / (1.0 + jnp.exp(-gate)))
    o_ref[...] = jnp.swapaxes(out.reshape(ts, nb, S5_CHANNELS), 0, 1).astype(o_ref.dtype)


def _s5(u, bmat, cmat, are, aim, d, wglu, bglu):
    b, s, _ = u.shape
    ts = min(S5_TIME_TILE, s)
    blk = pl.BlockSpec((b, ts, S5_CHANNELS), lambda i: (0, i, 0))
    return pl.pallas_call(
        _s5_kernel,
        grid=(s // ts,),
        in_specs=[blk, _full_spec(bmat.shape), _full_spec(cmat.shape), _full_spec(are.shape),
                  _full_spec(aim.shape), _full_spec(d.shape), _full_spec(wglu.shape), _full_spec(bglu.shape)],
        out_specs=blk,
        out_shape=jax.ShapeDtypeStruct((b, s, S5_CHANNELS), BF16),
        scratch_shapes=[pltpu.VMEM((ts * b, 2 * S5_WIDTH), F32), pltpu.VMEM((b, 2 * S5_WIDTH), F32)],
        compiler_params=_cparams("arbitrary"),
        name="s5_scan",
    )(u, bmat, cmat, are, aim, d, wglu, bglu)


def _rotate_half_cols(w):
    half = w.shape[-1] // 2
    return jnp.concatenate([-w[..., half:], w[..., :half]], axis=-1)


def _layer0_params(w_in, q_norm, w_uq, kv_norm, w_ukv, seq):
    d = w_in.shape[0]
    w_rope = w_in[:, 2048:2080]
    pad_l = jnp.zeros((d, MLA_NOPE), F32)
    pad_r = jnp.zeros((d, LANES - MLA_NOPE - MLA_ROPE), F32)
    w1 = jnp.concatenate([w_in[:, :2048], pad_l, w_rope, pad_r, pad_l, _rotate_half_cols(w_rope), pad_r], axis=1)

    wq3 = w_uq.reshape(MLA_RANK, N_HEADS, MLA_NOPE + MLA_ROPE)
    nope, rope = wq3[..., :MLA_NOPE], wq3[..., MLA_NOPE:]
    z_nope = jnp.zeros_like(nope)
    z_pad = jnp.zeros((MLA_RANK, N_HEADS, LANES - MLA_NOPE - MLA_ROPE), F32)
    wq = jnp.concatenate([nope, rope, z_pad], axis=-1).reshape(MLA_RANK, N_HEADS * LANES)
    wqr = jnp.concatenate([z_nope, _rotate_half_cols(rope), z_pad], axis=-1).reshape(MLA_RANK, N_HEADS * LANES)

    wkv3 = w_ukv.reshape(MLA_RANK, N_HEADS, 2 * HEAD_DIM)
    wkn = jnp.concatenate([wkv3[..., :MLA_NOPE], jnp.zeros((MLA_RANK, N_HEADS, LANES - MLA_NOPE), F32)],
                          axis=-1).reshape(MLA_RANK, N_HEADS * LANES)
    wv = wkv3[..., MLA_NOPE:].reshape(MLA_RANK, HEADS_W)

    half = MLA_ROPE // 2
    freqs = ROPE_BASE ** (-jnp.arange(half, dtype=F32) / half)
    ang = jnp.arange(seq, dtype=F32)[:, None] * freqs
    cos = jnp.concatenate([jnp.cos(ang)] * 2, axis=1)
    sin = jnp.concatenate([jnp.sin(ang)] * 2, axis=1)
    ones = jnp.ones((seq, MLA_NOPE), F32)
    zl = jnp.zeros((seq, MLA_NOPE), F32)
    zr = jnp.zeros((seq, LANES - MLA_NOPE - MLA_ROPE), F32)
    scale = (MLA_NOPE + MLA_ROPE) ** -0.5 * LOG2_E
    cosq = jnp.concatenate([ones, cos, zr], axis=1) * scale
    sinq = jnp.concatenate([zl, sin, zr], axis=1) * scale
    cosk = jnp.concatenate([zl, cos, zr], axis=1)
    sink = jnp.concatenate([zl, sin, zr], axis=1)
    return (w1.astype(BF16), q_norm.reshape(1, -1), kv_norm.reshape(1, -1), wq.astype(BF16), wqr.astype(BF16),
            wkn.astype(BF16), wv.astype(BF16), cosq, sinq, cosk, sink)


def _s5_params(lam_re, lam_im, log_dt, b_re, b_im, c_re, c_im):
    dt = jnp.exp(log_dt)[:, None]
    mag = jnp.exp(lam_re * dt)
    ab_re, ab_im = mag * jnp.cos(lam_im * dt), mag * jnp.sin(lam_im * dt)
    den = lam_re * lam_re + lam_im * lam_im
    nr, ni = ab_re - 1.0, ab_im
    f_re, f_im = (nr * lam_re + ni * lam_im) / den, (ni * lam_re - nr * lam_im) / den
    bb_re = f_re[..., None] * b_re - f_im[..., None] * b_im
    bb_im = f_re[..., None] * b_im + f_im[..., None] * b_re

    gpc = LANES // S5_GROUP
    n_chunks = S5_GROUPS // gpc
    eye = jnp.eye(gpc, dtype=F32)

    def in_blocks(bb):
        t = bb.reshape(n_chunks, gpc, S5_STATE, S5_GROUP)
        return jnp.einsum('cgph,gk->cghkp', t, eye).reshape(n_chunks, gpc * S5_GROUP, gpc * S5_STATE)

    def out_blocks(cc):
        t = cc.reshape(n_chunks, gpc, S5_GROUP, S5_STATE)
        return jnp.einsum('cghp,gk->cgpkh', t, eye).reshape(n_chunks, gpc * S5_STATE, gpc * S5_GROUP)

    bmat = jnp.concatenate([in_blocks(bb_re), in_blocks(bb_im)], axis=2)
    cmat = jnp.concatenate([out_blocks(c_re), -out_blocks(c_im)], axis=1)
    a_re = ab_re.reshape(1, S5_WIDTH)
    a_im = ab_im.reshape(1, S5_WIDTH)
    return bmat.astype(BF16), cmat.astype(BF16), a_re, a_im


def kernel(x, ab_w_in, ab_q_norm, ab_w_uq, ab_kv_norm, ab_w_ukv, ab_w_out, cd_w_in, s5_lambda_re, s5_lambda_im,
           s5_log_dt, s5_b_re, s5_b_im, s5_c_re, s5_c_im, s5_d, s5_w_glu, s5_b_glu, cd_w_out, ln1_g, ln1_b, ln2_g,
           ln2_b, ffn_w_gate, ffn_w_up, ffn_w_down):
    b, s, d = x.shape
    t = b * s
    x2 = x.reshape(t, d)
    vec = lambda a: a.reshape(1, -1)

    p0 = _layer0_params(ab_w_in[0], ab_q_norm[0], ab_w_uq[0], ab_kv_norm[0], ab_w_ukv[0], s)
    qsb, ksb, vsb, qf, kf, vm = _proj_ab(x2, *p0, seq=s)
    sh = lambda a: a.reshape(b, s, a.shape[-1])
    o_sb = _sb_attention(sh(qsb), sh(ksb), sh(vsb)).reshape(t, HEADS_W)
    o_mla = _mla_attention(sh(qf), sh(kf), sh(vm)).reshape(t, HEADS_W)
    w_out = ab_w_out[0].astype(BF16)
    x2 = _layer_tail(o_sb, o_mla, x2, w_out[:HEADS_W], w_out[HEADS_W:], vec(ln1_g[0]), vec(ln1_b[0]),
                     ffn_w_gate[0].astype(BF16), ffn_w_up[0].astype(BF16), ffn_w_down[0].astype(BF16),
                     vec(ln2_g[0]), vec(ln2_b[0]))

    u, q, k, v = _proj_cd(x2, cd_w_in[0].astype(BF16))
    bmat, cmat, a_re, a_im = _s5_params(s5_lambda_re[0], s5_lambda_im[0], s5_log_dt[0], s5_b_re[0], s5_b_im[0],
                                        s5_c_re[0], s5_c_im[0])
    o_s5 = _s5(sh(u), bmat, cmat, a_re, a_im, vec(s5_d[0]), s5_w_glu[0].astype(BF16), vec(s5_b_glu[0]))
    o_moba = _moba_attention(sh(q), sh(k), sh(v))
    w_out = cd_w_out[0].astype(BF16)
    x2 = _layer_tail(o_s5.reshape(t, S5_CHANNELS), o_moba.reshape(t, HEADS_W), x2, w_out[:S5_CHANNELS],
                     w_out[S5_CHANNELS:], vec(ln1_g[1]), vec(ln1_b[1]),
                     ffn_w_gate[1].astype(BF16), ffn_w_up[1].astype(BF16), ffn_w_down[1].astype(BF16),
                     vec(ln2_g[1]), vec(ln2_b[1]))
    return x2.reshape(b, s, d)
```

```python
import functools
import math

import jax
import jax.numpy as jnp
from jax import lax
from jax.experimental import pallas as pl
from jax.experimental.pallas import tpu as pltpu

F32 = jnp.float32
BF16 = jnp.bfloat16

D_MODEL = 1024
HEAD_DIM = 64
N_HEADS = 8
HEADS_W = N_HEADS * HEAD_DIM
MLA_RANK = 256
MLA_NOPE = 64
MLA_ROPE = 32
ROPE_BASE = 10000.0
S5_CHANNELS = 512
S5_GROUP = 16
S5_GROUPS = 32
S5_STATE = 64
S5_WIDTH = S5_GROUPS * S5_STATE
MOBA_BLOCK = 256
MOBA_TOPK = 3
D_FF = 2816
DEPTH = 2
DN_ALPHA = (2 * DEPTH) ** 0.25
LN_EPS = 1e-5
RMS_EPS = 1e-6

LANES = 128
NEG_BIG = -1e30
LOG2_E = math.log2(math.e)
VMEM_LIMIT = 56 * 1024 * 1024

ROW_TILE = 512
TAIL_ROW_TILE = 1024
TAIL_STRIPS = 4
SB_GROUP = 256
SB_Q_TILE = 256
SB_EXIT = -176.0
Q_TILE = 512
K_TILE = 512
S5_TIME_TILE = 128


def _cparams(*sem):
    return pltpu.CompilerParams(dimension_semantics=sem, vmem_limit_bytes=VMEM_LIMIT)


def _full_spec(shape):
    nd = len(shape)
    return pl.BlockSpec(shape, lambda *_: (0,) * nd)


def _dot(a, b):
    return jnp.dot(a, b, preferred_element_type=F32)


def _dot_nt(a, b):
    return lax.dot_general(a, b, (((1,), (1,)), ((), ())), preferred_element_type=F32)


def _layer_norm_rows(r, g, b):
    mu = jnp.mean(r, axis=-1, keepdims=True)
    d = r - mu
    var = jnp.mean(d * d, axis=-1, keepdims=True)
    return d * lax.rsqrt(var + LN_EPS) * g + b


def _proj_ab_kernel(x_ref, w1_ref, gq_ref, gkv_ref, wq_ref, wqr_ref, wkn_ref, wv_ref,
                    cosq_ref, sinq_ref, cosk_ref, sink_ref,
                    qsb_ref, ksb_ref, vsb_ref, qf_ref, kf_ref, vm_ref):
    xb = x_ref[...].astype(BF16)
    p = _dot(xb, w1_ref[...])
    qsb_ref[...] = (p[:, 0:512] * (HEAD_DIM ** -0.5 * LOG2_E)).astype(BF16)
    ksb_ref[...] = p[:, 512:1024].astype(BF16)
    vsb_ref[...] = p[:, 1024:1536].astype(BF16)
    cq = p[:, 1536:1792]
    ckv = p[:, 1792:2048]
    rope = p[:, 2048:2176]
    rope_rot = p[:, 2176:2304]

    cqn = cq * lax.rsqrt(jnp.mean(cq * cq, axis=-1, keepdims=True) + RMS_EPS) * gq_ref[...]
    ckvn = ckv * lax.rsqrt(jnp.mean(ckv * ckv, axis=-1, keepdims=True) + RMS_EPS) * gkv_ref[...]
    cqb = cqn.astype(BF16)
    ckvb = ckvn.astype(BF16)

    cosq = jnp.concatenate([cosq_ref[...]] * N_HEADS, axis=1)
    sinq = jnp.concatenate([sinq_ref[...]] * N_HEADS, axis=1)
    qf = _dot(cqb, wq_ref[...]) * cosq + _dot(cqb, wqr_ref[...]) * sinq
    qf_ref[...] = qf.astype(BF16)

    kpe = rope * cosk_ref[...] + rope_rot * sink_ref[...]
    kf = _dot(ckvb, wkn_ref[...]) + jnp.concatenate([kpe] * N_HEADS, axis=1)
    kf_ref[...] = kf.astype(BF16)
    vm_ref[...] = _dot(ckvb, wv_ref[...]).astype(BF16)


def _proj_ab(x2, w1, gq, gkv, wq, wqr, wkn, wv, cosq, sinq, cosk, sink, seq):
    t = x2.shape[0]
    tm = min(ROW_TILE, seq)
    n_seq_tiles = seq // tm
    row = lambda c: pl.BlockSpec((tm, c), lambda i: (i, 0))
    tab = pl.BlockSpec((tm, LANES), lambda i: (i % n_seq_tiles, 0))
    out_w = (HEADS_W, HEADS_W, HEADS_W, N_HEADS * LANES, N_HEADS * LANES, HEADS_W)
    return pl.pallas_call(
        _proj_ab_kernel,
        grid=(t // tm,),
        in_specs=[row(D_MODEL), _full_spec(w1.shape), _full_spec(gq.shape), _full_spec(gkv.shape),
                  _full_spec(wq.shape), _full_spec(wqr.shape), _full_spec(wkn.shape), _full_spec(wv.shape),
                  tab, tab, tab, tab],
        out_specs=[row(c) for c in out_w],
        out_shape=[jax.ShapeDtypeStruct((t, c), BF16) for c in out_w],
        compiler_params=_cparams("parallel"),
        name="proj_ab",
    )(x2, w1, gq, gkv, wq, wqr, wkn, wv, cosq, sinq, cosk, sink)


def _key_chunks(qi, tq, tk):
    end = (qi + 1) * tq
    return [(ks, min(tk, end - ks)) for ks in range(0, end, tk)]


def _sb_tile(qh, k2, v2, acc, run, diag, suffix):
    tq = qh.shape[0]
    grp = suffix.shape[0]
    z = _dot_nt(qh, k2)
    soft = jnp.log2(1.0 + jnp.exp2(-jnp.abs(z)))
    log_beta = jnp.minimum(z, 0.0) - soft
    log_keep = log_beta - z

    def own_keys_only_past(x):
        r = lax.broadcasted_iota(jnp.int32, (tq, tq), 0)
        c = lax.broadcasted_iota(jnp.int32, (tq, tq), 1)
        own = jnp.where(c < r, x[:, -tq:], 0.0)
        return own if x.shape[1] == tq else jnp.concatenate([x[:, :-tq], own], axis=1)

    if diag:
        log_keep = own_keys_only_past(log_keep)
    hi = log_keep.astype(BF16)
    laters = []
    for g in reversed(range(z.shape[1] // grp)):
        cols = slice(g * grp, (g + 1) * grp)
        laters.insert(0, _dot(hi[:, cols], suffix) + run)
        run = run + jnp.sum(log_keep[:, cols], axis=-1, keepdims=True)
    w = jnp.exp2(log_beta + jnp.concatenate(laters, axis=1))
    if diag:
        w = own_keys_only_past(w)
    return acc + _dot(w.astype(BF16), v2), run


def _sb_attn_kernel(q_ref, k_ref, v_ref, o_ref, acc_ref, run_ref, *, tq):
    seq = q_ref.shape[0]
    n_q = seq // tq
    grp = min(SB_GROUP, tq)
    lane = lax.broadcasted_iota(jnp.int32, (1, LANES), 1)
    first = lane < HEAD_DIM
    rg = lax.broadcasted_iota(jnp.int32, (grp, grp), 0)
    cg = lax.broadcasted_iota(jnp.int32, (grp, grp), 1)
    suffix = jnp.where(rg > cg, 1.0, 0.0).astype(BF16)

    def heads_of(q2):
        zero = jnp.zeros_like(q2)
        return jnp.where(first, q2, zero), jnp.where(first, zero, q2)

    for qi in range(n_q):
        qs = qi * tq
        q_heads = heads_of(q_ref[qs:qs + tq, :])
        b0 = max(0, qs - tq)
        w = qs + tq - b0
        accs = []
        for h in range(2):
            acc, run = _sb_tile(q_heads[h], k_ref[b0:b0 + w, :], v_ref[b0:b0 + w, :],
                                jnp.zeros((tq, LANES), F32), jnp.zeros((tq, 1), F32), True, suffix)
            accs.append(acc)
            if b0 > 0:
                acc_ref[h, qs:qs + tq, :] = acc
                run_ref[h, qs:qs + tq, :] = run
        o_ref[qs:qs + tq, :] = jnp.where(first, accs[0], accs[1]).astype(o_ref.dtype)

    def finish(qi, _):
        rows = pl.ds(pl.multiple_of(qi * tq, tq), tq)
        run_a = run_ref[0, rows, :]
        run_b = run_ref[1, rows, :]

        def unfinished(st):
            j, _, ra, _, rb = st
            return (j >= 0) & (jnp.maximum(jnp.max(ra), jnp.max(rb)) > SB_EXIT)

        @pl.when(unfinished((qi - 2, None, run_a, None, run_b)))
        def _():
            q_heads = heads_of(q_ref[rows, :])

            def earlier(st):
                j, acc_a, ra, acc_b, rb = st
                keys = pl.ds(pl.multiple_of(j * tq, tq), tq)
                k2 = k_ref[keys, :]
                v2 = v_ref[keys, :]
                acc_a, ra = _sb_tile(q_heads[0], k2, v2, acc_a, ra, False, suffix)
                acc_b, rb = _sb_tile(q_heads[1], k2, v2, acc_b, rb, False, suffix)
                return j - 1, acc_a, ra, acc_b, rb

            st = lax.while_loop(unfinished, earlier, (qi - 2, acc_ref[0, rows, :], run_a, acc_ref[1, rows, :], run_b))
            o_ref[rows, :] = jnp.where(first, st[1], st[3]).astype(o_ref.dtype)

        return 0

    lax.fori_loop(2, n_q, finish, 0)


def _sb_attention(q, k, v):
    b, s, _ = q.shape
    tq = min(SB_Q_TILE, s)
    assert s % tq == 0
    spec = pl.BlockSpec((None, s, LANES), lambda bi, h: (bi, 0, h))
    return pl.pallas_call(
        functools.partial(_sb_attn_kernel, tq=tq),
        grid=(b, N_HEADS // 2),
        in_specs=[spec, spec, spec],
        out_specs=spec,
        out_shape=jax.ShapeDtypeStruct((b, s, HEADS_W), BF16),
        scratch_shapes=[pltpu.VMEM((2, s, LANES), F32), pltpu.VMEM((2, s, 1), F32)],
        compiler_params=_cparams("parallel", "parallel"),
        name="sb_attn",
    )(q, k, v)


def _softmax_tile(s, v_ones, state):
    m, acc = state
    m_new = jnp.maximum(m, jnp.max(s, axis=-1, keepdims=True))
    p = jnp.exp2(s - m_new)
    acc = jnp.exp2(m - m_new) * acc + _dot(p.astype(BF16), v_ones)
    return m_new, acc


def _softmax_state(tq):
    return jnp.full((tq, 1), NEG_BIG, F32), jnp.zeros((tq, LANES), F32)


def _softmax_finish(state_a, state_b, first):
    acc_a, acc_b = state_a[1], state_b[1]
    out_a = acc_a / pltpu.roll(acc_a, HEAD_DIM, 1)
    out_b = acc_b / pltpu.roll(acc_b, HEAD_DIM, 1)
    return jnp.where(first, out_a, out_b)


def _with_ones(v2, first):
    one = jnp.ones_like(v2)
    return jnp.where(first, v2, one), jnp.where(first, one, v2)


def _mla_attn_kernel(q_ref, k_ref, v_ref, o_ref, *, tq, tk):
    seq = q_ref.shape[0]
    lane = lax.broadcasted_iota(jnp.int32, (1, LANES), 1)
    first = lane < HEAD_DIM

    for qi in reversed(range(seq // tq)):
        rows = slice(qi * tq, (qi + 1) * tq)
        chunks = _key_chunks(qi, tq, tk)
        last = len(chunks) - 1
        carry = [_softmax_state(tq), _softmax_state(tq)]
        for ci in (last, *range(last)):
            ks, w = chunks[ci]
            keys = slice(ks, ks + w)
            v_ones = _with_ones(v_ref[keys, :], first)
            for h in range(2):
                s = _dot_nt(q_ref[rows, h * LANES:(h + 1) * LANES], k_ref[keys, h * LANES:(h + 1) * LANES])
                if ci == last:
                    r = lax.broadcasted_iota(jnp.int32, (tq, w), 0)
                    c = lax.broadcasted_iota(jnp.int32, (tq, w), 1)
                    s = jnp.where(c <= r + (qi * tq - ks), s, NEG_BIG)
                carry[h] = _softmax_tile(s, v_ones[h], carry[h])
        o_ref[rows, :] = _softmax_finish(carry[0], carry[1], first).astype(o_ref.dtype)


def _mla_attention(qf, kf, v):
    b, s, _ = v.shape
    tq, tk = min(Q_TILE, s), min(K_TILE, s)
    assert s % tq == 0 and tk % tq == 0
    kspec = pl.BlockSpec((None, s, 2 * LANES), lambda bi, h: (bi, 0, h))
    vspec = pl.BlockSpec((None, s, LANES), lambda bi, h: (bi, 0, h))
    return pl.pallas_call(
        functools.partial(_mla_attn_kernel, tq=tq, tk=tk),
        grid=(b, N_HEADS // 2),
        in_specs=[kspec, kspec, vspec],
        out_specs=vspec,
        out_shape=jax.ShapeDtypeStruct((b, s, HEADS_W), BF16),
        compiler_params=_cparams("parallel", "parallel"),
        name="mla_attn",
    )(qf, kf, v)


def _moba_kernel(q_ref, k_ref, v_ref, o_ref, *, tq, tk):
    seq = q_ref.shape[0]
    nblk = seq // MOBA_BLOCK
    lane = lax.broadcasted_iota(jnp.int32, (1, LANES), 1)
    first = lane < HEAD_DIM

    means = [jnp.mean(k_ref[n * MOBA_BLOCK:(n + 1) * MOBA_BLOCK, :].astype(F32), axis=0, keepdims=True)
             for n in range(nblk)]
    km = jnp.concatenate(means + [jnp.zeros((LANES - nblk, LANES), F32)], axis=0)
    k1 = km.astype(BF16)
    rem = km - k1.astype(F32)
    k2 = rem.astype(BF16)
    k3 = (rem - k2.astype(F32)).astype(BF16)

    rowid = lax.broadcasted_iota(jnp.int32, (8, tq), 0)
    for qi in reversed(range(seq // tq)):
        rows = slice(qi * tq, (qi + 1) * tq)
        q2 = q_ref[rows, :]
        zero = jnp.zeros_like(q2)
        q_heads = (jnp.where(first, q2, zero), jnp.where(first, zero, q2))
        qblk = qi * (tq // MOBA_BLOCK) + lax.broadcasted_iota(jnp.int32, (8, tq), 1) // MOBA_BLOCK
        valid = rowid < qblk
        q_aug = []
        for h in range(2):
            qh = q_heads[h]
            gate_t = _dot_nt(k1, qh) + _dot_nt(k2, qh) + _dot_nt(k3, qh)
            g = jnp.where(valid, gate_t[0:8, :], -jnp.inf)
            beaten = jnp.zeros((8, tq), jnp.int32)
            for m in range(nblk):
                gm = g[m:m + 1, :]
                wins = (gm > g) | ((gm == g) & (rowid > m))
                beaten = beaten + jnp.where(wins, 1, 0)
            usable = (valid & (beaten < MOBA_TOPK)) | (rowid == qblk)
            bias_t = jnp.where(usable, 0.0, NEG_BIG)
            bias_t = jnp.concatenate([bias_t, jnp.zeros((LANES - 8, tq), F32)], axis=0)
            q_aug.append(jnp.concatenate([qh, bias_t.T.astype(BF16)], axis=1))

        chunks = _key_chunks(qi, tq, tk)
        last = len(chunks) - 1
        carry = [_softmax_state(tq), _softmax_state(tq)]
        for ci in (last, *range(last)):
            ks, w = chunks[ci]
            keys = slice(ks, ks + w)
            blk = (ks + lax.broadcasted_iota(jnp.int32, (w, LANES), 0)) // MOBA_BLOCK
            onehot = jnp.where(blk == lax.broadcasted_iota(jnp.int32, (w, LANES), 1), 1.0, 0.0).astype(BF16)
            k_aug = jnp.concatenate([k_ref[keys, :], onehot], axis=1)
            v_ones = _with_ones(v_ref[keys, :], first)
            for h in range(2):
                s = _dot_nt(q_aug[h], k_aug)
                if ci == last:
                    r = lax.broadcasted_iota(jnp.int32, (tq, w), 0)
                    c = lax.broadcasted_iota(jnp.int32, (tq, w), 1)
                    s = jnp.where(c <= r + (qi * tq - ks), s, NEG_BIG)
                carry[h] = _softmax_tile(s, v_ones[h], carry[h])
        o_ref[rows, :] = _softmax_finish(carry[0], carry[1], first).astype(o_ref.dtype)


def _moba_attention(q, k, v):
    b, s, _ = q.shape
    tq, tk = min(Q_TILE, s), min(K_TILE, s)
    assert s % tq == 0 and tk % tq == 0 and tq % MOBA_BLOCK == 0 and s // MOBA_BLOCK <= 8
    spec = pl.BlockSpec((None, s, LANES), lambda bi, h: (bi, 0, h))
    return pl.pallas_call(
        functools.partial(_moba_kernel, tq=tq, tk=tk),
        grid=(b, N_HEADS // 2),
        in_specs=[spec, spec, spec],
        out_specs=spec,
        out_shape=jax.ShapeDtypeStruct((b, s, HEADS_W), BF16),
        compiler_params=_cparams("parallel", "parallel"),
        name="moba_attn",
    )(q, k, v)


def _tail_kernel(a_ref, b_ref, x_ref, wo_ref, g1_ref, beta1_ref, wg_ref, wu_ref, wd_ref, g2_ref, beta2_ref, o_ref, *,
                 layer):
    tm = x_ref.shape[0]
    strip = tm // TAIL_STRIPS
    rows = [slice(r0, r0 + strip) for r0 in range(0, tm, strip)]
    wa = a_ref.shape[1]
    ln = slice(layer, layer + 1)

    def mix(r):
        return _dot(a_ref[r, :], wo_ref[:wa, :]) + _dot(b_ref[r, :], wo_ref[wa:, :])

    def norm1(r, m):
        return _layer_norm_rows(DN_ALPHA * x_ref[r, :] + m, g1_ref[ln, :], beta1_ref[ln, :])

    def ffn(x1):
        xb = x1.astype(BF16)
        gate = _dot(xb, wg_ref[...])
        up = _dot(xb, wu_ref[...])
        h = (gate * (1.0 / (1.0 + jnp.exp(-gate))) * up).astype(BF16)
        return _dot(h, wd_ref[...])

    def norm2(r, x1, y):
        o_ref[r, :] = _layer_norm_rows(DN_ALPHA * x1 + y, g2_ref[ln, :], beta2_ref[ln, :])

    n = len(rows)
    m = [None] * n
    x1 = [None] * n
    y = [None] * n
    m[0] = mix(rows[0])
    for k in range(n):
        if k + 1 < n:
            m[k + 1] = mix(rows[k + 1])
        x1[k] = norm1(rows[k], m[k])
        if k > 0:
            norm2(rows[k - 1], x1[k - 1], y[k - 1])
        y[k] = ffn(x1[k])
    norm2(rows[n - 1], x1[n - 1], y[n - 1])


def _layer_tail(a, b, x2, w_out, g1, beta1, wg, wu, wd, g2, beta2, layer):
    t = x2.shape[0]
    tm = min(TAIL_ROW_TILE, t)
    row = lambda c: pl.BlockSpec((tm, c), lambda i: (i, 0))
    whole = lambda arr: pl.BlockSpec(arr.shape, lambda i: (0, 0), pipeline_mode=pl.Buffered(1))
    of_layer = lambda arr, l: pl.BlockSpec((None,) + arr.shape[1:], lambda i: (l, 0, 0),
                                           pipeline_mode=pl.Buffered(1))
    return pl.pallas_call(
        functools.partial(_tail_kernel, layer=layer),
        grid=(t // tm,),
        in_specs=[row(a.shape[1]), row(b.shape[1]), row(D_MODEL), of_layer(w_out, 0), whole(g1), whole(beta1),
                  of_layer(wg, layer), of_layer(wu, layer), of_layer(wd, layer), whole(g2), whole(beta2)],
        out_specs=row(D_MODEL),
        out_shape=jax.ShapeDtypeStruct((t, D_MODEL), F32),
        compiler_params=_cparams("parallel"),
        name="layer_tail",
    )(a, b, x2, w_out, g1, beta1, wg, wu, wd, g2, beta2)


def _proj_cd_kernel(x_ref, w_ref, u_ref, q_ref, k_ref, v_ref):
    p = _dot(x_ref[...].astype(BF16), w_ref[...])
    u_ref[...] = p[:, 0:512].astype(BF16)
    q_ref[...] = (p[:, 512:1024] * (HEAD_DIM ** -0.5 * LOG2_E)).astype(BF16)
    k_ref[...] = p[:, 1024:1536].astype(BF16)
    v_ref[...] = p[:, 1536:2048].astype(BF16)


def _proj_cd(x2, w):
    t = x2.shape[0]
    tm = min(ROW_TILE, t)
    row = lambda c: pl.BlockSpec((tm, c), lambda i: (i, 0))
    return pl.pallas_call(
        _proj_cd_kernel,
        grid=(t // tm,),
        in_specs=[row(D_MODEL), _full_spec(w.shape)],
        out_specs=[row(HEADS_W)] * 4,
        out_shape=[jax.ShapeDtypeStruct((t, HEADS_W), BF16)] * 4,
        compiler_params=_cparams("parallel"),
        name="proj_cd",
    )(x2, w)


def _s5_kernel(u_ref, bmat_ref, cmat_ref, are_ref, aim_ref, d_ref, wglu_ref, bglu_ref, o_ref,
               x_ref, state_ref):
    nb, ts, _ = u_ref.shape
    rows = ts * nb

    @pl.when(pl.program_id(0) == 0)
    def _():
        state_ref[...] = jnp.zeros_like(state_ref)

    uf = jnp.swapaxes(u_ref[...].astype(F32), 0, 1).reshape(rows, S5_CHANNELS)
    ub = uf.astype(BF16)
    n_chunks = S5_CHANNELS // LANES
    cw = S5_WIDTH // n_chunks
    ys = []
    for j in range(n_chunks):
        re_cols = slice(2 * j * cw, (2 * j + 1) * cw)
        im_cols = slice((2 * j + 1) * cw, (2 * j + 2) * cw)
        x_ref[:, 2 * j * cw:(2 * j + 2) * cw] = _dot(ub[:, j * LANES:(j + 1) * LANES], bmat_ref[j])
        ar = jnp.broadcast_to(are_ref[:, j * cw:(j + 1) * cw], (nb, cw))
        ai = jnp.broadcast_to(aim_ref[:, j * cw:(j + 1) * cw], (nb, cw))
        xr = state_ref[:, re_cols]
        xi = state_ref[:, im_cols]
        for t in range(ts):
            step = slice(t * nb, (t + 1) * nb)
            nr = ar * xr - ai * xi + x_ref[step, re_cols]
            ni = ar * xi + ai * xr + x_ref[step, im_cols]
            x_ref[step, re_cols] = nr
            x_ref[step, im_cols] = ni
            xr, xi = nr, ni
        state_ref[:, re_cols] = xr
        state_ref[:, im_cols] = xi
        ys.append(_dot(x_ref[:, 2 * j * cw:(2 * j + 2) * cw].astype(BF16), cmat_ref[j]))
    y = jnp.concatenate(ys, axis=1) + d_ref[...] * uf
    z = 0.5 * y * (1.0 + jnp.tanh(math.sqrt(2.0 / math.pi) * (y + 0.044715 * (y * y * y))))
    gate = _dot(z.astype(BF16), wglu_ref[...]) + bglu_ref[...]
    out = z * (1.0 / (1.0 + jnp.exp(-gate)))
    o_ref[...] = jnp.swapaxes(out.reshape(ts, nb, S5_CHANNELS), 0, 1).astype(o_ref.dtype)


def _s5(u, bmat, cmat, are, aim, d, wglu, bglu):
    b, s, _ = u.shape
    ts = min(S5_TIME_TILE, s)
    blk = pl.BlockSpec((b, ts, S5_CHANNELS), lambda i: (0, i, 0))
    return pl.pallas_call(
        _s5_kernel,
        grid=(s // ts,),
        in_specs=[blk, _full_spec(bmat.shape), _full_spec(cmat.shape), _full_spec(are.shape),
                  _full_spec(aim.shape), _full_spec(d.shape), _full_spec(wglu.shape), _full_spec(bglu.shape)],
        out_specs=blk,
        out_shape=jax.ShapeDtypeStruct((b, s, S5_CHANNELS), BF16),
        scratch_shapes=[pltpu.VMEM((ts * b, 2 * S5_WIDTH), F32), pltpu.VMEM((b, 2 * S5_WIDTH), F32)],
        compiler_params=_cparams("arbitrary"),
        name="s5_scan",
    )(u, bmat, cmat, are, aim, d, wglu, bglu)


def _rotate_half_cols(w):
    half = w.shape[-1] // 2
    return jnp.concatenate([-w[..., half:], w[..., :half]], axis=-1)


def _layer0_params(w_in, q_norm, w_uq, kv_norm, w_ukv, seq):
    d = w_in.shape[0]
    w_rope = w_in[:, 2048:2080]
    pad_l = jnp.zeros((d, MLA_NOPE), F32)
    pad_r = jnp.zeros((d, LANES - MLA_NOPE - MLA_ROPE), F32)
    w1 = jnp.concatenate([w_in[:, :2048], pad_l, w_rope, pad_r, pad_l, _rotate_half_cols(w_rope), pad_r], axis=1)

    wq3 = w_uq.reshape(MLA_RANK, N_HEADS, MLA_NOPE + MLA_ROPE)
    nope, rope = wq3[..., :MLA_NOPE], wq3[..., MLA_NOPE:]
    z_nope = jnp.zeros_like(nope)
    z_pad = jnp.zeros((MLA_RANK, N_HEADS, LANES - MLA_NOPE - MLA_ROPE), F32)
    wq = jnp.concatenate([nope, rope, z_pad], axis=-1).reshape(MLA_RANK, N_HEADS * LANES)
    wqr = jnp.concatenate([z_nope, _rotate_half_cols(rope), z_pad], axis=-1).reshape(MLA_RANK, N_HEADS * LANES)

    wkv3 = w_ukv.reshape(MLA_RANK, N_HEADS, 2 * HEAD_DIM)
    wkn = jnp.concatenate([wkv3[..., :MLA_NOPE], jnp.zeros((MLA_RANK, N_HEADS, LANES - MLA_NOPE), F32)],
                          axis=-1).reshape(MLA_RANK, N_HEADS * LANES)
    wv = wkv3[..., MLA_NOPE:].reshape(MLA_RANK, HEADS_W)

    half = MLA_ROPE // 2
    freqs = ROPE_BASE ** (-jnp.arange(half, dtype=F32) / half)
    ang = jnp.arange(seq, dtype=F32)[:, None] * freqs
    cos = jnp.concatenate([jnp.cos(ang)] * 2, axis=1)
    sin = jnp.concatenate([jnp.sin(ang)] * 2, axis=1)
    ones = jnp.ones((seq, MLA_NOPE), F32)
    zl = jnp.zeros((seq, MLA_NOPE), F32)
    zr = jnp.zeros((seq, LANES - MLA_NOPE - MLA_ROPE), F32)
    scale = (MLA_NOPE + MLA_ROPE) ** -0.5 * LOG2_E
    cosq = jnp.concatenate([ones, cos, zr], axis=1) * scale
    sinq = jnp.concatenate([zl, sin, zr], axis=1) * scale
    cosk = jnp.concatenate([zl, cos, zr], axis=1)
    sink = jnp.concatenate([zl, sin, zr], axis=1)
    return (w1.astype(BF16), q_norm.reshape(1, -1), kv_norm.reshape(1, -1), wq.astype(BF16), wqr.astype(BF16),
            wkn.astype(BF16), wv.astype(BF16), cosq, sinq, cosk, sink)


def _s5_params(lam_re, lam_im, log_dt, b_re, b_im, c_re, c_im):
    dt = jnp.exp(log_dt)[:, None]
    mag = jnp.exp(lam_re * dt)
    ab_re, ab_im = mag * jnp.cos(lam_im * dt), mag * jnp.sin(lam_im * dt)
    den = lam_re * lam_re + lam_im * lam_im
    nr, ni = ab_re - 1.0, ab_im
    f_re, f_im = (nr * lam_re + ni * lam_im) / den, (ni * lam_re - nr * lam_im) / den
    bb_re = f_re[..., None] * b_re - f_im[..., None] * b_im
    bb_im = f_re[..., None] * b_im + f_im[..., None] * b_re

    gpc = LANES // S5_GROUP
    n_chunks = S5_GROUPS // gpc
    eye = jnp.eye(gpc, dtype=F32)

    def in_blocks(bb):
        t = bb.reshape(n_chunks, gpc, S5_STATE, S5_GROUP)
        return jnp.einsum('cgph,gk->cghkp', t, eye).reshape(n_chunks, gpc * S5_GROUP, gpc * S5_STATE)

    def out_blocks(cc):
        t = cc.reshape(n_chunks, gpc, S5_GROUP, S5_STATE)
        return jnp.einsum('cghp,gk->cgpkh', t, eye).reshape(n_chunks, gpc * S5_STATE, gpc * S5_GROUP)

    bmat = jnp.concatenate([in_blocks(bb_re), in_blocks(bb_im)], axis=2)
    cmat = jnp.concatenate([out_blocks(c_re), -out_blocks(c_im)], axis=1)
    a_re = ab_re.reshape(1, S5_WIDTH)
    a_im = ab_im.reshape(1, S5_WIDTH)
    return bmat.astype(BF16), cmat.astype(BF16), a_re, a_im


def kernel(x, ab_w_in, ab_q_norm, ab_w_uq, ab_kv_norm, ab_w_ukv, ab_w_out, cd_w_in, s5_lambda_re, s5_lambda_im,
           s5_log_dt, s5_b_re, s5_b_im, s5_c_re, s5_c_im, s5_d, s5_w_glu, s5_b_glu, cd_w_out, ln1_g, ln1_b, ln2_g,
           ln2_b, ffn_w_gate, ffn_w_up, ffn_w_down):
    b, s, d = x.shape
    t = b * s
    x2 = x.reshape(t, d)
    vec = lambda a: a.reshape(1, -1)

    p0 = _layer0_params(ab_w_in[0], ab_q_norm[0], ab_w_uq[0], ab_kv_norm[0], ab_w_ukv[0], s)
    qsb, ksb, vsb, qf, kf, vm = _proj_ab(x2, *p0, seq=s)
    sh = lambda a: a.reshape(b, s, a.shape[-1])
    o_sb = _sb_attention(sh(qsb), sh(ksb), sh(vsb)).reshape(t, HEADS_W)
    o_mla = _mla_attention(sh(qf), sh(kf), sh(vm)).reshape(t, HEADS_W)
    wg, wu, wd = ffn_w_gate.astype(BF16), ffn_w_up.astype(BF16), ffn_w_down.astype(BF16)
    x2 = _layer_tail(o_sb, o_mla, x2, ab_w_out.astype(BF16), ln1_g, ln1_b, wg, wu, wd, ln2_g, ln2_b, layer=0)

    u, q, k, v = _proj_cd(x2, cd_w_in[0].astype(BF16))
    bmat, cmat, a_re, a_im = _s5_params(s5_lambda_re[0], s5_lambda_im[0], s5_log_dt[0], s5_b_re[0], s5_b_im[0],
                                        s5_c_re[0], s5_c_im[0])
    o_s5 = _s5(sh(u), bmat, cmat, a_re, a_im, vec(s5_d[0]), s5_w_glu[0].astype(BF16), vec(s5_b_glu[0]))
    o_moba = _moba_attention(sh(q), sh(k), sh(v))
    x2 = _layer_tail(o_s5.reshape(t, S5_CHANNELS), o_moba.reshape(t, HEADS_W), x2, cd_w_out.astype(BF16),
                     ln1_g, ln1_b, wg, wu, wd, ln2_g, ln2_b, layer=1)
    return x2.reshape(b, s, d)
```

```python
import functools
import math

import jax
import jax.numpy as jnp
from jax import lax
from jax.experimental import pallas as pl
from jax.experimental.pallas import tpu as pltpu

F32 = jnp.float32
BF16 = jnp.bfloat16

D_MODEL = 1024
HEAD_DIM = 64
N_HEADS = 8
HEADS_W = N_HEADS * HEAD_DIM
MLA_RANK = 256
MLA_NOPE = 64
MLA_ROPE = 32
ROPE_BASE = 10000.0
S5_CHANNELS = 512
S5_GROUP = 16
S5_GROUPS = 32
S5_STATE = 64
S5_WIDTH = S5_GROUPS * S5_STATE
MOBA_BLOCK = 256
MOBA_TOPK = 3
D_FF = 2816
DEPTH = 2
DN_ALPHA = (2 * DEPTH) ** 0.25
LN_EPS = 1e-5
RMS_EPS = 1e-6

LANES = 128
NEG_BIG = -1e30
LOG2_E = math.log2(math.e)
VMEM_LIMIT = 56 * 1024 * 1024

ROW_TILE = 512
TAIL_ROW_TILE = 1024
TAIL_STRIPS = 4
SB_GROUP = 256
SB_Q_TILE = 256
SB_EXIT = -176.0
Q_TILE = 512
S5_TIME_TILE = 128


def _cparams(*sem):
    return pltpu.CompilerParams(dimension_semantics=sem, vmem_limit_bytes=VMEM_LIMIT)


def _full_spec(shape):
    nd = len(shape)
    return pl.BlockSpec(shape, lambda *_: (0,) * nd)


def _dot(a, b):
    return jnp.dot(a, b, preferred_element_type=F32)


def _dot_nt(a, b):
    return lax.dot_general(a, b, (((1,), (1,)), ((), ())), preferred_element_type=F32)


def _layer_norm_rows(r, g, b):
    mu = jnp.mean(r, axis=-1, keepdims=True)
    d = r - mu
    var = jnp.mean(d * d, axis=-1, keepdims=True)
    return d * lax.rsqrt(var + LN_EPS) * g + b


def _proj_ab_kernel(x_ref, w1_ref, gq_ref, gkv_ref, wq_ref, wqr_ref, wkn_ref, wv_ref,
                    cosq_ref, sinq_ref, cosk_ref, sink_ref,
                    qsb_ref, ksb_ref, vsb_ref, qf_ref, kf_ref, vm_ref):
    xb = x_ref[...].astype(BF16)
    p = _dot(xb, w1_ref[...])
    qsb_ref[...] = (p[:, 0:512] * (HEAD_DIM ** -0.5 * LOG2_E)).astype(BF16)
    ksb_ref[...] = p[:, 512:1024].astype(BF16)
    vsb_ref[...] = p[:, 1024:1536].astype(BF16)
    cq = p[:, 1536:1792]
    ckv = p[:, 1792:2048]
    rope = p[:, 2048:2176]
    rope_rot = p[:, 2176:2304]

    cqn = cq * lax.rsqrt(jnp.mean(cq * cq, axis=-1, keepdims=True) + RMS_EPS) * gq_ref[...]
    ckvn = ckv * lax.rsqrt(jnp.mean(ckv * ckv, axis=-1, keepdims=True) + RMS_EPS) * gkv_ref[...]
    cqb = cqn.astype(BF16)
    ckvb = ckvn.astype(BF16)

    cosq = jnp.concatenate([cosq_ref[...]] * N_HEADS, axis=1)
    sinq = jnp.concatenate([sinq_ref[...]] * N_HEADS, axis=1)
    qf = _dot(cqb, wq_ref[...]) * cosq + _dot(cqb, wqr_ref[...]) * sinq
    qf_ref[...] = qf.astype(BF16)

    kpe = rope * cosk_ref[...] + rope_rot * sink_ref[...]
    kf = _dot(ckvb, wkn_ref[...]) + jnp.concatenate([kpe] * N_HEADS, axis=1)
    kf_ref[...] = kf.astype(BF16)
    vm_ref[...] = _dot(ckvb, wv_ref[...]).astype(BF16)


def _proj_ab(x2, w1, gq, gkv, wq, wqr, wkn, wv, cosq, sinq, cosk, sink, seq):
    t = x2.shape[0]
    tm = min(ROW_TILE, seq)
    n_seq_tiles = seq // tm
    row = lambda c: pl.BlockSpec((tm, c), lambda i: (i, 0))
    tab = pl.BlockSpec((tm, LANES), lambda i: (i % n_seq_tiles, 0))
    out_w = (HEADS_W, HEADS_W, HEADS_W, N_HEADS * LANES, N_HEADS * LANES, HEADS_W)
    return pl.pallas_call(
        _proj_ab_kernel,
        grid=(t // tm,),
        in_specs=[row(D_MODEL), _full_spec(w1.shape), _full_spec(gq.shape), _full_spec(gkv.shape),
                  _full_spec(wq.shape), _full_spec(wqr.shape), _full_spec(wkn.shape), _full_spec(wv.shape),
                  tab, tab, tab, tab],
        out_specs=[row(c) for c in out_w],
        out_shape=[jax.ShapeDtypeStruct((t, c), BF16) for c in out_w],
        compiler_params=_cparams("parallel"),
        name="proj_ab",
    )(x2, w1, gq, gkv, wq, wqr, wkn, wv, cosq, sinq, cosk, sink)


def _sb_tile(qh, k2, v2, acc, run, diag, suffix):
    tq = qh.shape[0]
    grp = suffix.shape[0]
    z = _dot_nt(qh, k2)
    soft = jnp.log2(1.0 + jnp.exp2(-jnp.abs(z)))
    log_beta = jnp.minimum(z, 0.0) - soft
    log_keep = log_beta - z

    def own_keys_only_past(x):
        r = lax.broadcasted_iota(jnp.int32, (tq, tq), 0)
        c = lax.broadcasted_iota(jnp.int32, (tq, tq), 1)
        own = jnp.where(c < r, x[:, -tq:], 0.0)
        return own if x.shape[1] == tq else jnp.concatenate([x[:, :-tq], own], axis=1)

    if diag:
        log_keep = own_keys_only_past(log_keep)
    hi = log_keep.astype(BF16)
    laters = []
    for g in reversed(range(z.shape[1] // grp)):
        cols = slice(g * grp, (g + 1) * grp)
        laters.insert(0, _dot(hi[:, cols], suffix) + run)
        run = run + jnp.sum(log_keep[:, cols], axis=-1, keepdims=True)
    w = jnp.exp2(log_beta + jnp.concatenate(laters, axis=1))
    if diag:
        w = own_keys_only_past(w)
    return acc + _dot(w.astype(BF16), v2), run


def _sb_attn_kernel(q_ref, k_ref, v_ref, o_ref, acc_ref, run_ref, *, tq):
    seq = q_ref.shape[0]
    n_q = seq // tq
    grp = min(SB_GROUP, tq)
    lane = lax.broadcasted_iota(jnp.int32, (1, LANES), 1)
    first = lane < HEAD_DIM
    rg = lax.broadcasted_iota(jnp.int32, (grp, grp), 0)
    cg = lax.broadcasted_iota(jnp.int32, (grp, grp), 1)
    suffix = jnp.where(rg > cg, 1.0, 0.0).astype(BF16)

    def heads_of(q2):
        zero = jnp.zeros_like(q2)
        return jnp.where(first, q2, zero), jnp.where(first, zero, q2)

    for qi in range(n_q):
        qs = qi * tq
        q_heads = heads_of(q_ref[qs:qs + tq, :])
        b0 = max(0, qs - tq)
        w = qs + tq - b0
        accs = []
        for h in range(2):
            acc, run = _sb_tile(q_heads[h], k_ref[b0:b0 + w, :], v_ref[b0:b0 + w, :],
                                jnp.zeros((tq, LANES), F32), jnp.zeros((tq, 1), F32), True, suffix)
            accs.append(acc)
            if b0 > 0:
                acc_ref[h, qs:qs + tq, :] = acc
                run_ref[h, qs:qs + tq, :] = run
        o_ref[qs:qs + tq, :] = jnp.where(first, accs[0], accs[1]).astype(o_ref.dtype)

    def finish(qi, _):
        rows = pl.ds(pl.multiple_of(qi * tq, tq), tq)
        run_a = run_ref[0, rows, :]
        run_b = run_ref[1, rows, :]

        def unfinished(st):
            j, _, ra, _, rb = st
            return (j >= 0) & (jnp.maximum(jnp.max(ra), jnp.max(rb)) > SB_EXIT)

        @pl.when(unfinished((qi - 2, None, run_a, None, run_b)))
        def _():
            q_heads = heads_of(q_ref[rows, :])

            def earlier(st):
                j, acc_a, ra, acc_b, rb = st
                keys = pl.ds(pl.multiple_of(j * tq, tq), tq)
                k2 = k_ref[keys, :]
                v2 = v_ref[keys, :]
                acc_a, ra = _sb_tile(q_heads[0], k2, v2, acc_a, ra, False, suffix)
                acc_b, rb = _sb_tile(q_heads[1], k2, v2, acc_b, rb, False, suffix)
                return j - 1, acc_a, ra, acc_b, rb

            st = lax.while_loop(unfinished, earlier, (qi - 2, acc_ref[0, rows, :], run_a, acc_ref[1, rows, :], run_b))
            o_ref[rows, :] = jnp.where(first, st[1], st[3]).astype(o_ref.dtype)

        return 0

    lax.fori_loop(2, n_q, finish, 0)


def _sb_attention(q, k, v):
    b, s, _ = q.shape
    tq = min(SB_Q_TILE, s)
    assert s % tq == 0
    spec = pl.BlockSpec((None, s, LANES), lambda bi, h: (bi, 0, h))
    return pl.pallas_call(
        functools.partial(_sb_attn_kernel, tq=tq),
        grid=(b, N_HEADS // 2),
        in_specs=[spec, spec, spec],
        out_specs=spec,
        out_shape=jax.ShapeDtypeStruct((b, s, HEADS_W), BF16),
        scratch_shapes=[pltpu.VMEM((2, s, LANES), F32), pltpu.VMEM((2, s, 1), F32)],
        compiler_params=_cparams("parallel", "parallel"),
        name="sb_attn",
    )(q, k, v)


def _causal_softmax_pv(s, v_ones, first):
    tq = s.shape[0]
    r = lax.broadcasted_iota(jnp.int32, (tq, tq), 0)
    c = lax.broadcasted_iota(jnp.int32, (tq, tq), 1)
    own = jnp.where(c <= r, s[:, -tq:], NEG_BIG)
    s = own if s.shape[1] == tq else jnp.concatenate([s[:, :-tq], own], axis=1)
    p = jnp.exp2(s - jnp.max(s, axis=-1, keepdims=True))
    acc = _dot(p.astype(BF16), v_ones)
    return acc / pltpu.roll(acc, HEAD_DIM, 1)


def _with_ones(v2, first):
    one = jnp.ones_like(v2)
    return jnp.where(first, v2, one), jnp.where(first, one, v2)


def _mla_attn_kernel(q_ref, k_ref, v_ref, o_ref, *, tq):
    seq = q_ref.shape[0]
    lane = lax.broadcasted_iota(jnp.int32, (1, LANES), 1)
    first = lane < HEAD_DIM
    for qi in reversed(range(seq // tq)):
        rows = slice(qi * tq, (qi + 1) * tq)
        end = (qi + 1) * tq
        v_ones = _with_ones(v_ref[:end, :], first)
        out = [_causal_softmax_pv(_dot_nt(q_ref[rows, h * LANES:(h + 1) * LANES], k_ref[:end, h * LANES:(h + 1) * LANES]),
                                  v_ones[h], first) for h in range(2)]
        o_ref[rows, :] = jnp.where(first, out[0], out[1]).astype(o_ref.dtype)


def _mla_attention(qf, kf, v):
    b, s, _ = v.shape
    tq = min(Q_TILE, s)
    assert s % tq == 0
    kspec = pl.BlockSpec((None, s, 2 * LANES), lambda bi, h: (bi, 0, h))
    vspec = pl.BlockSpec((None, s, LANES), lambda bi, h: (bi, 0, h))
    return pl.pallas_call(
        functools.partial(_mla_attn_kernel, tq=tq),
        grid=(b, N_HEADS // 2),
        in_specs=[kspec, kspec, vspec],
        out_specs=vspec,
        out_shape=jax.ShapeDtypeStruct((b, s, HEADS_W), BF16),
        compiler_params=_cparams("parallel", "parallel"),
        name="mla_attn",
    )(qf, kf, v)


def _moba_kernel(q_ref, k_ref, v_ref, o_ref, *, tq):
    seq = q_ref.shape[0]
    nblk = seq // MOBA_BLOCK
    lane = lax.broadcasted_iota(jnp.int32, (1, LANES), 1)
    first = lane < HEAD_DIM

    means = [jnp.mean(k_ref[n * MOBA_BLOCK:(n + 1) * MOBA_BLOCK, :].astype(F32), axis=0, keepdims=True)
             for n in range(nblk)]
    km = jnp.concatenate(means + [jnp.zeros((16 - nblk, LANES), F32)], axis=0)
    k1 = km.astype(BF16)
    rem = km - k1.astype(F32)
    k2 = rem.astype(BF16)
    k3 = (rem - k2.astype(F32)).astype(BF16)

    rowid = lax.broadcasted_iota(jnp.int32, (8, tq), 0)
    for qi in reversed(range(seq // tq)):
        rows = slice(qi * tq, (qi + 1) * tq)
        q2 = q_ref[rows, :]
        zero = jnp.zeros_like(q2)
        q_heads = (jnp.where(first, q2, zero), jnp.where(first, zero, q2))
        qblk = qi * (tq // MOBA_BLOCK) + lax.broadcasted_iota(jnp.int32, (8, tq), 1) // MOBA_BLOCK
        valid = rowid < qblk
        q_aug = []
        for h in range(2):
            qh = q_heads[h]
            gate_t = _dot_nt(k1, qh) + _dot_nt(k2, qh) + _dot_nt(k3, qh)
            g = jnp.where(valid, gate_t[0:8, :], -jnp.inf)
            beaten = jnp.zeros((8, tq), jnp.int32)
            for m in range(nblk):
                gm = g[m:m + 1, :]
                wins = (gm > g) | ((gm == g) & (rowid > m))
                beaten = beaten + jnp.where(wins, 1, 0)
            usable = (valid & (beaten < MOBA_TOPK)) | (rowid == qblk)
            bias_t = jnp.where(usable, 0.0, NEG_BIG)
            bias_t = jnp.concatenate([bias_t, jnp.zeros((LANES - 8, tq), F32)], axis=0)
            q_aug.append(jnp.concatenate([qh, bias_t.T.astype(BF16)], axis=1))

        end = (qi + 1) * tq
        blk = lax.broadcasted_iota(jnp.int32, (end, LANES), 0) // MOBA_BLOCK
        onehot = jnp.where(blk == lax.broadcasted_iota(jnp.int32, (end, LANES), 1), 1.0, 0.0).astype(BF16)
        k_aug = jnp.concatenate([k_ref[:end, :], onehot], axis=1)
        v_ones = _with_ones(v_ref[:end, :], first)
        out = [_causal_softmax_pv(_dot_nt(q_aug[h], k_aug), v_ones[h], first) for h in range(2)]
        o_ref[rows, :] = jnp.where(first, out[0], out[1]).astype(o_ref.dtype)


def _moba_attention(q, k, v):
    b, s, _ = q.shape
    tq = min(Q_TILE, s)
    assert s % tq == 0 and tq % MOBA_BLOCK == 0 and s // MOBA_BLOCK <= 8
    spec = pl.BlockSpec((None, s, LANES), lambda bi, h: (bi, 0, h))
    return pl.pallas_call(
        functools.partial(_moba_kernel, tq=tq),
        grid=(b, N_HEADS // 2),
        in_specs=[spec, spec, spec],
        out_specs=spec,
        out_shape=jax.ShapeDtypeStruct((b, s, HEADS_W), BF16),
        compiler_params=_cparams("parallel", "parallel"),
        name="moba_attn",
    )(q, k, v)


def _tail_kernel(a_ref, b_ref, x_ref, wo_ref, g1_ref, beta1_ref, wg_ref, wu_ref, wd_ref, g2_ref, beta2_ref, o_ref, *,
                 layer):
    tm = x_ref.shape[0]
    strip = tm // TAIL_STRIPS
    rows = [slice(r0, r0 + strip) for r0 in range(0, tm, strip)]
    wa = a_ref.shape[1]
    ln = slice(layer, layer + 1)

    def mix(r):
        return _dot(a_ref[r, :], wo_ref[:wa, :]) + _dot(b_ref[r, :], wo_ref[wa:, :])

    def norm1(r, m):
        return _layer_norm_rows(DN_ALPHA * x_ref[r, :] + m, g1_ref[ln, :], beta1_ref[ln, :])

    def ffn(x1):
        xb = x1.astype(BF16)
        gate = _dot(xb, wg_ref[...])
        up = _dot(xb, wu_ref[...])
        h = (gate * (1.0 / (1.0 + jnp.exp(-gate))) * up).astype(BF16)
        return _dot(h, wd_ref[...])

    def norm2(r, x1, y):
        o_ref[r, :] = _layer_norm_rows(DN_ALPHA * x1 + y, g2_ref[ln, :], beta2_ref[ln, :])

    n = len(rows)
    m = [None] * n
    x1 = [None] * n
    y = [None] * n
    m[0] = mix(rows[0])
    for k in range(n):
        if k + 1 < n:
            m[k + 1] = mix(rows[k + 1])
        x1[k] = norm1(rows[k], m[k])
        if k > 0:
            norm2(rows[k - 1], x1[k - 1], y[k - 1])
        y[k] = ffn(x1[k])
    norm2(rows[n - 1], x1[n - 1], y[n - 1])


def _layer_tail(a, b, x2, w_out, g1, beta1, wg, wu, wd, g2, beta2, layer):
    t = x2.shape[0]
    tm = min(TAIL_ROW_TILE, t)
    row = lambda c: pl.BlockSpec((tm, c), lambda i: (i, 0))
    whole = lambda arr: pl.BlockSpec(arr.shape, lambda i: (0, 0), pipeline_mode=pl.Buffered(1))
    of_layer = lambda arr, l: pl.BlockSpec((None,) + arr.shape[1:], lambda i: (l, 0, 0),
                                           pipeline_mode=pl.Buffered(1))
    return pl.pallas_call(
        functools.partial(_tail_kernel, layer=layer),
        grid=(t // tm,),
        in_specs=[row(a.shape[1]), row(b.shape[1]), row(D_MODEL), of_layer(w_out, 0), whole(g1), whole(beta1),
                  of_layer(wg, layer), of_layer(wu, layer), of_layer(wd, layer), whole(g2), whole(beta2)],
        out_specs=row(D_MODEL),
        out_shape=jax.ShapeDtypeStruct((t, D_MODEL), F32),
        compiler_params=_cparams("parallel"),
        name="layer_tail",
    )(a, b, x2, w_out, g1, beta1, wg, wu, wd, g2, beta2)


def _proj_cd_kernel(x_ref, w_ref, u_ref, q_ref, k_ref, v_ref):
    p = _dot(x_ref[...].astype(BF16), w_ref[...])
    u_ref[...] = p[:, 0:512].astype(BF16)
    q_ref[...] = (p[:, 512:1024] * (HEAD_DIM ** -0.5 * LOG2_E)).astype(BF16)
    k_ref[...] = p[:, 1024:1536].astype(BF16)
    v_ref[...] = p[:, 1536:2048].astype(BF16)


def _proj_cd(x2, w):
    t = x2.shape[0]
    tm = min(ROW_TILE, t)
    row = lambda c: pl.BlockSpec((tm, c), lambda i: (i, 0))
    return pl.pallas_call(
        _proj_cd_kernel,
        grid=(t // tm,),
        in_specs=[row(D_MODEL), _full_spec(w.shape)],
        out_specs=[row(HEADS_W)] * 4,
        out_shape=[jax.ShapeDtypeStruct((t, HEADS_W), BF16)] * 4,
        compiler_params=_cparams("parallel"),
        name="proj_cd",
    )(x2, w)


def _s5_kernel(u_ref, bmat_ref, cmat_ref, are_ref, aim_ref, d_ref, wglu_ref, bglu_ref, o_ref,
               x_ref, state_ref):
    nb, ts, _ = u_ref.shape
    rows = ts * nb

    @pl.when(pl.program_id(0) == 0)
    def _():
        state_ref[...] = jnp.zeros_like(state_ref)

    uf = jnp.swapaxes(u_ref[...].astype(F32), 0, 1).reshape(rows, S5_CHANNELS)
    ub = uf.astype(BF16)
    n_chunks = S5_CHANNELS // LANES
    cw = S5_WIDTH // n_chunks
    ys = []
    for j in range(n_chunks):
        re_cols = slice(2 * j * cw, (2 * j + 1) * cw)
        im_cols = slice((2 * j + 1) * cw, (2 * j + 2) * cw)
        x_ref[:, 2 * j * cw:(2 * j + 2) * cw] = _dot(ub[:, j * LANES:(j + 1) * LANES], bmat_ref[j])
        ar = jnp.broadcast_to(are_ref[:, j * cw:(j + 1) * cw], (nb, cw))
        ai = jnp.broadcast_to(aim_ref[:, j * cw:(j + 1) * cw], (nb, cw))
        xr = state_ref[:, re_cols]
        xi = state_ref[:, im_cols]
        for t in range(ts):
            step = slice(t * nb, (t + 1) * nb)
            nr = ar * xr - ai * xi + x_ref[step, re_cols]
            ni = ar * xi + ai * xr + x_ref[step, im_cols]
            x_ref[step, re_cols] = nr
            x_ref[step, im_cols] = ni
            xr, xi = nr, ni
        state_ref[:, re_cols] = xr
        state_ref[:, im_cols] = xi
        ys.append(_dot(x_ref[:, 2 * j * cw:(2 * j + 2) * cw].astype(BF16), cmat_ref[j]))
    y = jnp.concatenate(ys, axis=1) + d_ref[...] * uf
    z = 0.5 * y * (1.0 + jnp.tanh(math.sqrt(2.0 / math.pi) * (y + 0.044715 * (y * y * y))))
    gate = _dot(z.astype(BF16), wglu_ref[...]) + bglu_ref[...]
    out = z * (1.0 / (1.0 + jnp.exp(-gate)))
    o_ref[...] = jnp.swapaxes(out.reshape(ts, nb, S5_CHANNELS), 0, 1).astype(o_ref.dtype)


def _s5(u, bmat, cmat, are, aim, d, wglu, bglu):
    b, s, _ = u.shape
    ts = min(S5_TIME_TILE, s)
    blk = pl.BlockSpec((b, ts, S5_CHANNELS), lambda i: (0, i, 0))
    return pl.pallas_call(
        _s5_kernel,
        grid=(s // ts,),
        in_specs=[blk, _full_spec(bmat.shape), _full_spec(cmat.shape), _full_spec(are.shape),
                  _full_spec(aim.shape), _full_spec(d.shape), _full_spec(wglu.shape), _full_spec(bglu.shape)],
        out_specs=blk,
        out_shape=jax.ShapeDtypeStruct((b, s, S5_CHANNELS), BF16),
        scratch_shapes=[pltpu.VMEM((ts * b, 2 * S5_WIDTH), F32), pltpu.VMEM((b, 2 * S5_WIDTH), F32)],
        compiler_params=_cparams("arbitrary"),
        name="s5_scan",
    )(u, bmat, cmat, are, aim, d, wglu, bglu)


def _rotate_half_cols(w):
    half = w.shape[-1] // 2
    return jnp.concatenate([-w[..., half:], w[..., :half]], axis=-1)


def _layer0_params(w_in, q_norm, w_uq, kv_norm, w_ukv, seq):
    d = w_in.shape[0]
    w_rope = w_in[:, 2048:2080]
    pad_l = jnp.zeros((d, MLA_NOPE), F32)
    pad_r = jnp.zeros((d, LANES - MLA_NOPE - MLA_ROPE), F32)
    w1 = jnp.concatenate([w_in[:, :2048], pad_l, w_rope, pad_r, pad_l, _rotate_half_cols(w_rope), pad_r], axis=1)

    wq3 = w_uq.reshape(MLA_RANK, N_HEADS, MLA_NOPE + MLA_ROPE)
    nope, rope = wq3[..., :MLA_NOPE], wq3[..., MLA_NOPE:]
    z_nope = jnp.zeros_like(nope)
    z_pad = jnp.zeros((MLA_RANK, N_HEADS, LANES - MLA_NOPE - MLA_ROPE), F32)
    wq = jnp.concatenate([nope, rope, z_pad], axis=-1).reshape(MLA_RANK, N_HEADS * LANES)
    wqr = jnp.concatenate([z_nope, _rotate_half_cols(rope), z_pad], axis=-1).reshape(MLA_RANK, N_HEADS * LANES)

    wkv3 = w_ukv.reshape(MLA_RANK, N_HEADS, 2 * HEAD_DIM)
    wkn = jnp.concatenate([wkv3[..., :MLA_NOPE], jnp.zeros((MLA_RANK, N_HEADS, LANES - MLA_NOPE), F32)],
                          axis=-1).reshape(MLA_RANK, N_HEADS * LANES)
    wv = wkv3[..., MLA_NOPE:].reshape(MLA_RANK, HEADS_W)

    half = MLA_ROPE // 2
    freqs = ROPE_BASE ** (-jnp.arange(half, dtype=F32) / half)
    ang = jnp.arange(seq, dtype=F32)[:, None] * freqs
    cos = jnp.concatenate([jnp.cos(ang)] * 2, axis=1)
    sin = jnp.concatenate([jnp.sin(ang)] * 2, axis=1)
    ones = jnp.ones((seq, MLA_NOPE), F32)
    zl = jnp.zeros((seq, MLA_NOPE), F32)
    zr = jnp.zeros((seq, LANES - MLA_NOPE - MLA_ROPE), F32)
    scale = (MLA_NOPE + MLA_ROPE) ** -0.5 * LOG2_E
    cosq = jnp.concatenate([ones, cos, zr], axis=1) * scale
    sinq = jnp.concatenate([zl, sin, zr], axis=1) * scale
    cosk = jnp.concatenate([zl, cos, zr], axis=1)
    sink = jnp.concatenate([zl, sin, zr], axis=1)
    return (w1.astype(BF16), q_norm.reshape(1, -1), kv_norm.reshape(1, -1), wq.astype(BF16), wqr.astype(BF16),
            wkn.astype(BF16), wv.astype(BF16), cosq, sinq, cosk, sink)


def _s5_params(lam_re, lam_im, log_dt, b_re, b_im, c_re, c_im):
    dt = jnp.exp(log_dt)[:, None]
    mag = jnp.exp(lam_re * dt)
    ab_re, ab_im = mag * jnp.cos(lam_im * dt), mag * jnp.sin(lam_im * dt)
    den = lam_re * lam_re + lam_im * lam_im
    nr, ni = ab_re - 1.0, ab_im
    f_re, f_im = (nr * lam_re + ni * lam_im) / den, (ni * lam_re - nr * lam_im) / den
    bb_re = f_re[..., None] * b_re - f_im[..., None] * b_im
    bb_im = f_re[..., None] * b_im + f_im[..., None] * b_re

    gpc = LANES // S5_GROUP
    n_chunks = S5_GROUPS // gpc
    eye = jnp.eye(gpc, dtype=F32)

    def in_blocks(bb):
        t = bb.reshape(n_chunks, gpc, S5_STATE, S5_GROUP)
        return jnp.einsum('cgph,gk->cghkp', t, eye).reshape(n_chunks, gpc * S5_GROUP, gpc * S5_STATE)

    def out_blocks(cc):
        t = cc.reshape(n_chunks, gpc, S5_GROUP, S5_STATE)
        return jnp.einsum('cghp,gk->cgpkh', t, eye).reshape(n_chunks, gpc * S5_STATE, gpc * S5_GROUP)

    bmat = jnp.concatenate([in_blocks(bb_re), in_blocks(bb_im)], axis=2)
    cmat = jnp.concatenate([out_blocks(c_re), -out_blocks(c_im)], axis=1)
    a_re = ab_re.reshape(1, S5_WIDTH)
    a_im = ab_im.reshape(1, S5_WIDTH)
    return bmat.astype(BF16), cmat.astype(BF16), a_re, a_im


def kernel(x, ab_w_in, ab_q_norm, ab_w_uq, ab_kv_norm, ab_w_ukv, ab_w_out, cd_w_in, s5_lambda_re, s5_lambda_im,
           s5_log_dt, s5_b_re, s5_b_im, s5_c_re, s5_c_im, s5_d, s5_w_glu, s5_b_glu, cd_w_out, ln1_g, ln1_b, ln2_g,
           ln2_b, ffn_w_gate, ffn_w_up, ffn_w_down):
    b, s, d = x.shape
    t = b * s
    x2 = x.reshape(t, d)
    vec = lambda a: a.reshape(1, -1)

    p0 = _layer0_params(ab_w_in[0], ab_q_norm[0], ab_w_uq[0], ab_kv_norm[0], ab_w_ukv[0], s)
    qsb, ksb, vsb, qf, kf, vm = _proj_ab(x2, *p0, seq=s)
    sh = lambda a: a.reshape(b, s, a.shape[-1])
    o_sb = _sb_attention(sh(qsb), sh(ksb), sh(vsb)).reshape(t, HEADS_W)
    o_mla = _mla_attention(sh(qf), sh(kf), sh(vm)).reshape(t, HEADS_W)
    wg, wu, wd = ffn_w_gate.astype(BF16), ffn_w_up.astype(BF16), ffn_w_down.astype(BF16)
    x2 = _layer_tail(o_sb, o_mla, x2, ab_w_out.astype(BF16), ln1_g, ln1_b, wg, wu, wd, ln2_g, ln2_b, layer=0)

    u, q, k, v = _proj_cd(x2, cd_w_in[0].astype(BF16))
    bmat, cmat, a_re, a_im = _s5_params(s5_lambda_re[0], s5_lambda_im[0], s5_log_dt[0], s5_b_re[0], s5_b_im[0],
                                        s5_c_re[0], s5_c_im[0])
    o_s5 = _s5(sh(u), bmat, cmat, a_re, a_im, vec(s5_d[0]), s5_w_glu[0].astype(BF16), vec(s5_b_glu[0]))
    o_moba = _moba_attention(sh(q), sh(k), sh(v))
    x2 = _layer_tail(o_s5.reshape(t, S5_CHANNELS), o_moba.reshape(t, HEADS_W), x2, cd_w_out.astype(BF16),
                     ln1_g, ln1_b, wg, wu, wd, ln2_g, ln2_b, layer=1)
    return x2.reshape(b, s, d)
```

```python
import functools
import math

import jax
import jax.numpy as jnp
from jax import lax
from jax.experimental import pallas as pl
from jax.experimental.pallas import tpu as pltpu

F32 = jnp.float32
BF16 = jnp.bfloat16

D_MODEL = 1024
HEAD_DIM = 64
N_HEADS = 8
HEADS_W = N_HEADS * HEAD_DIM
MLA_RANK = 256
MLA_NOPE = 64
MLA_ROPE = 32
ROPE_BASE = 10000.0
S5_CHANNELS = 512
S5_GROUP = 16
S5_GROUPS = 32
S5_STATE = 64
S5_WIDTH = S5_GROUPS * S5_STATE
MOBA_BLOCK = 256
MOBA_TOPK = 3
D_FF = 2816
DEPTH = 2
DN_ALPHA = (2 * DEPTH) ** 0.25
LN_EPS = 1e-5
RMS_EPS = 1e-6

LANES = 128
NEG_BIG = -1e30
LOG2_E = math.log2(math.e)
VMEM_LIMIT = 56 * 1024 * 1024

ROW_TILE = 1024
TAIL_ROW_TILE = 1024
TAIL_STRIPS = 4
SB_GROUP = 256
SB_Q_TILE = 256
SB_EXIT = -176.0
Q_TILE = 512
S5_TIME_TILE = 128


def _cparams(*sem):
    return pltpu.CompilerParams(dimension_semantics=sem, vmem_limit_bytes=VMEM_LIMIT)


def _full_spec(shape):
    nd = len(shape)
    return pl.BlockSpec(shape, lambda *_: (0,) * nd)


def _dot(a, b):
    return jnp.dot(a, b, preferred_element_type=F32)


def _dot_nt(a, b):
    return lax.dot_general(a, b, (((1,), (1,)), ((), ())), preferred_element_type=F32)


def _layer_norm_rows(r, g, b):
    mu = jnp.mean(r, axis=-1, keepdims=True)
    d = r - mu
    var = jnp.mean(d * d, axis=-1, keepdims=True)
    return d * lax.rsqrt(var + LN_EPS) * g + b


def _proj_ab_kernel(x_ref, w1_ref, gq_ref, gkv_ref, wq_ref, wqr_ref, wkn_ref, wv_ref,
                    cosq_ref, sinq_ref, cosk_ref, sink_ref,
                    qsb_ref, ksb_ref, vsb_ref, qf_ref, kf_ref, vm_ref):
    xb = x_ref[...].astype(BF16)
    p = _dot(xb, w1_ref[...])
    qsb_ref[...] = (p[:, 0:512] * (HEAD_DIM ** -0.5 * LOG2_E)).astype(BF16)
    ksb_ref[...] = p[:, 512:1024].astype(BF16)
    vsb_ref[...] = p[:, 1024:1536].astype(BF16)
    cq = p[:, 1536:1792]
    ckv = p[:, 1792:2048]
    rope = p[:, 2048:2176]
    rope_rot = p[:, 2176:2304]

    cqn = cq * lax.rsqrt(jnp.mean(cq * cq, axis=-1, keepdims=True) + RMS_EPS) * gq_ref[...]
    ckvn = ckv * lax.rsqrt(jnp.mean(ckv * ckv, axis=-1, keepdims=True) + RMS_EPS) * gkv_ref[...]
    cqb = cqn.astype(BF16)
    ckvb = ckvn.astype(BF16)

    cosq = jnp.concatenate([cosq_ref[...]] * N_HEADS, axis=1)
    sinq = jnp.concatenate([sinq_ref[...]] * N_HEADS, axis=1)
    qf = _dot(cqb, wq_ref[...]) * cosq + _dot(cqb, wqr_ref[...]) * sinq
    qf_ref[...] = qf.astype(BF16)

    kpe = rope * cosk_ref[...] + rope_rot * sink_ref[...]
    kf = _dot(ckvb, wkn_ref[...]) + jnp.concatenate([kpe] * N_HEADS, axis=1)
    kf_ref[...] = kf.astype(BF16)
    vm_ref[...] = _dot(ckvb, wv_ref[...]).astype(BF16)


def _proj_ab(x2, w1, gq, gkv, wq, wqr, wkn, wv, cosq, sinq, cosk, sink, seq):
    t = x2.shape[0]
    tm = min(ROW_TILE, seq)
    n_seq_tiles = seq // tm
    row = lambda c: pl.BlockSpec((tm, c), lambda i: (i, 0))
    tab = pl.BlockSpec((tm, LANES), lambda i: (i % n_seq_tiles, 0))
    out_w = (HEADS_W, HEADS_W, HEADS_W, N_HEADS * LANES, N_HEADS * LANES, HEADS_W)
    return pl.pallas_call(
        _proj_ab_kernel,
        grid=(t // tm,),
        in_specs=[row(D_MODEL), _full_spec(w1.shape), _full_spec(gq.shape), _full_spec(gkv.shape),
                  _full_spec(wq.shape), _full_spec(wqr.shape), _full_spec(wkn.shape), _full_spec(wv.shape),
                  tab, tab, tab, tab],
        out_specs=[row(c) for c in out_w],
        out_shape=[jax.ShapeDtypeStruct((t, c), BF16) for c in out_w],
        compiler_params=_cparams("parallel"),
        name="proj_ab",
    )(x2, w1, gq, gkv, wq, wqr, wkn, wv, cosq, sinq, cosk, sink)


def _sb_tile(qh, k2, v2, acc, run, diag, suffix):
    tq = qh.shape[0]
    grp = suffix.shape[0]
    z = _dot_nt(qh, k2)
    soft = jnp.log2(1.0 + jnp.exp2(-jnp.abs(z)))
    log_beta = jnp.minimum(z, 0.0) - soft
    log_keep = log_beta - z

    def own_keys_only_past(x):
        r = lax.broadcasted_iota(jnp.int32, (tq, tq), 0)
        c = lax.broadcasted_iota(jnp.int32, (tq, tq), 1)
        own = jnp.where(c < r, x[:, -tq:], 0.0)
        return own if x.shape[1] == tq else jnp.concatenate([x[:, :-tq], own], axis=1)

    if diag:
        log_keep = own_keys_only_past(log_keep)
    hi = log_keep.astype(BF16)
    laters = []
    for g in reversed(range(z.shape[1] // grp)):
        cols = slice(g * grp, (g + 1) * grp)
        laters.insert(0, _dot(hi[:, cols], suffix) + run)
        run = run + jnp.sum(log_keep[:, cols], axis=-1, keepdims=True)
    w = jnp.exp2(log_beta + jnp.concatenate(laters, axis=1))
    if diag:
        w = own_keys_only_past(w)
    return acc + _dot(w.astype(BF16), v2), run


def _sb_attn_kernel(q_ref, k_ref, v_ref, o_ref, acc_ref, run_ref, top_ref, *, tq):
    seq = q_ref.shape[0]
    n_q = seq // tq
    grp = min(SB_GROUP, tq)
    lane = lax.broadcasted_iota(jnp.int32, (1, LANES), 1)
    first = lane < HEAD_DIM
    rg = lax.broadcasted_iota(jnp.int32, (grp, grp), 0)
    cg = lax.broadcasted_iota(jnp.int32, (grp, grp), 1)
    suffix = jnp.where(rg > cg, 1.0, 0.0).astype(BF16)

    def heads_of(q2):
        zero = jnp.zeros_like(q2)
        return jnp.where(first, q2, zero), jnp.where(first, zero, q2)

    for qi in range(n_q):
        qs = qi * tq
        q_heads = heads_of(q_ref[qs:qs + tq, :])
        b0 = max(0, qs - tq)
        w = qs + tq - b0
        accs, runs = [], []
        for h in range(2):
            acc, run = _sb_tile(q_heads[h], k_ref[b0:b0 + w, :], v_ref[b0:b0 + w, :],
                                jnp.zeros((tq, LANES), F32), jnp.zeros((tq, 1), F32), True, suffix)
            accs.append(acc)
            runs.append(run)
            if b0 > 0:
                acc_ref[h, qs:qs + tq, :] = acc
                run_ref[h, qs:qs + tq, :] = run
        if b0 > 0:
            top_ref[qi] = jnp.maximum(jnp.max(runs[0]), jnp.max(runs[1]))
        o_ref[qs:qs + tq, :] = jnp.where(first, accs[0], accs[1]).astype(o_ref.dtype)

    def finish(qi, _):
        @pl.when(top_ref[qi] > SB_EXIT)
        def _():
            rows = pl.ds(pl.multiple_of(qi * tq, tq), tq)
            q_heads = heads_of(q_ref[rows, :])

            def unfinished(st):
                j, _, ra, _, rb = st
                return (j >= 0) & (jnp.maximum(jnp.max(ra), jnp.max(rb)) > SB_EXIT)

            def earlier(st):
                j, acc_a, ra, acc_b, rb = st
                keys = pl.ds(pl.multiple_of(j * tq, tq), tq)
                k2 = k_ref[keys, :]
                v2 = v_ref[keys, :]
                acc_a, ra = _sb_tile(q_heads[0], k2, v2, acc_a, ra, False, suffix)
                acc_b, rb = _sb_tile(q_heads[1], k2, v2, acc_b, rb, False, suffix)
                return j - 1, acc_a, ra, acc_b, rb

            st = lax.while_loop(unfinished, earlier, (qi - 2, acc_ref[0, rows, :], run_ref[0, rows, :],
                                                      acc_ref[1, rows, :], run_ref[1, rows, :]))
            o_ref[rows, :] = jnp.where(first, st[1], st[3]).astype(o_ref.dtype)

        return 0

    lax.fori_loop(2, n_q, finish, 0)


def _sb_attention(q, k, v):
    b, s, _ = q.shape
    tq = min(SB_Q_TILE, s)
    assert s % tq == 0
    spec = pl.BlockSpec((None, s, LANES), lambda bi, h: (bi, 0, h))
    return pl.pallas_call(
        functools.partial(_sb_attn_kernel, tq=tq),
        grid=(b, N_HEADS // 2),
        in_specs=[spec, spec, spec],
        out_specs=spec,
        out_shape=jax.ShapeDtypeStruct((b, s, HEADS_W), BF16),
        scratch_shapes=[pltpu.VMEM((2, s, LANES), F32), pltpu.VMEM((2, s, 1), F32), pltpu.SMEM((s // tq,), F32)],
        compiler_params=_cparams("parallel", "parallel"),
        name="sb_attn",
    )(q, k, v)


def _causal_softmax_pv(s, v_ones, first):
    tq = s.shape[0]
    r = lax.broadcasted_iota(jnp.int32, (tq, tq), 0)
    c = lax.broadcasted_iota(jnp.int32, (tq, tq), 1)
    own = jnp.where(c <= r, s[:, -tq:], NEG_BIG)
    s = own if s.shape[1] == tq else jnp.concatenate([s[:, :-tq], own], axis=1)
    p = jnp.exp2(s - jnp.max(s, axis=-1, keepdims=True))
    acc = _dot(p.astype(BF16), v_ones)
    return acc / pltpu.roll(acc, HEAD_DIM, 1)


def _with_ones(v2, first):
    one = jnp.ones_like(v2)
    return jnp.where(first, v2, one), jnp.where(first, one, v2)


def _mla_attn_kernel(q_ref, k_ref, v_ref, o_ref, *, tq):
    seq = q_ref.shape[0]
    lane = lax.broadcasted_iota(jnp.int32, (1, LANES), 1)
    first = lane < HEAD_DIM
    for qi in reversed(range(seq // tq)):
        rows = slice(qi * tq, (qi + 1) * tq)
        end = (qi + 1) * tq
        v_ones = _with_ones(v_ref[:end, :], first)
        out = [_causal_softmax_pv(_dot_nt(q_ref[rows, h * LANES:(h + 1) * LANES], k_ref[:end, h * LANES:(h + 1) * LANES]),
                                  v_ones[h], first) for h in range(2)]
        o_ref[rows, :] = jnp.where(first, out[0], out[1]).astype(o_ref.dtype)


def _mla_attention(qf, kf, v):
    b, s, _ = v.shape
    tq = min(Q_TILE, s)
    assert s % tq == 0
    kspec = pl.BlockSpec((None, s, 2 * LANES), lambda bi, h: (bi, 0, h))
    vspec = pl.BlockSpec((None, s, LANES), lambda bi, h: (bi, 0, h))
    return pl.pallas_call(
        functools.partial(_mla_attn_kernel, tq=tq),
        grid=(b, N_HEADS // 2),
        in_specs=[kspec, kspec, vspec],
        out_specs=vspec,
        out_shape=jax.ShapeDtypeStruct((b, s, HEADS_W), BF16),
        compiler_params=_cparams("parallel", "parallel"),
        name="mla_attn",
    )(qf, kf, v)


def _moba_kernel(q_ref, k_ref, v_ref, o_ref, *, tq):
    seq = q_ref.shape[0]
    nblk = seq // MOBA_BLOCK
    lane = lax.broadcasted_iota(jnp.int32, (1, LANES), 1)
    first = lane < HEAD_DIM

    means = [jnp.mean(k_ref[n * MOBA_BLOCK:(n + 1) * MOBA_BLOCK, :].astype(F32), axis=0, keepdims=True)
             for n in range(nblk)]
    km = jnp.concatenate(means + [jnp.zeros((16 - nblk, LANES), F32)], axis=0)
    k1 = km.astype(BF16)
    rem = km - k1.astype(F32)
    k2 = rem.astype(BF16)
    k3 = (rem - k2.astype(F32)).astype(BF16)

    rowid = lax.broadcasted_iota(jnp.int32, (8, tq), 0)
    for qi in reversed(range(seq // tq)):
        rows = slice(qi * tq, (qi + 1) * tq)
        q2 = q_ref[rows, :]
        zero = jnp.zeros_like(q2)
        q_heads = (jnp.where(first, q2, zero), jnp.where(first, zero, q2))
        qblk = qi * (tq // MOBA_BLOCK) + lax.broadcasted_iota(jnp.int32, (8, tq), 1) // MOBA_BLOCK
        valid = rowid < qblk
        q_aug = []
        for h in range(2):
            qh = q_heads[h]
            gate_t = _dot_nt(k1, qh) + _dot_nt(k2, qh) + _dot_nt(k3, qh)
            g = jnp.where(valid, gate_t[0:8, :], -jnp.inf)
            beaten = jnp.zeros((8, tq), jnp.int32)
            for m in range(nblk):
                gm = g[m:m + 1, :]
                wins = (gm > g) | ((gm == g) & (rowid > m))
                beaten = beaten + jnp.where(wins, 1, 0)
            usable = (valid & (beaten < MOBA_TOPK)) | (rowid == qblk)
            bias_t = jnp.where(usable, 0.0, NEG_BIG)
            bias_t = jnp.concatenate([bias_t, jnp.zeros((LANES - 8, tq), F32)], axis=0)
            q_aug.append(jnp.concatenate([qh, bias_t.T.astype(BF16)], axis=1))

        end = (qi + 1) * tq
        blk = lax.broadcasted_iota(jnp.int32, (end, LANES), 0) // MOBA_BLOCK
        onehot = jnp.where(blk == lax.broadcasted_iota(jnp.int32, (end, LANES), 1), 1.0, 0.0).astype(BF16)
        k_aug = jnp.concatenate([k_ref[:end, :], onehot], axis=1)
        v_ones = _with_ones(v_ref[:end, :], first)
        out = [_causal_softmax_pv(_dot_nt(q_aug[h], k_aug), v_ones[h], first) for h in range(2)]
        o_ref[rows, :] = jnp.where(first, out[0], out[1]).astype(o_ref.dtype)


def _moba_attention(q, k, v):
    b, s, _ = q.shape
    tq = min(Q_TILE, s)
    assert s % tq == 0 and tq % MOBA_BLOCK == 0 and s // MOBA_BLOCK <= 8
    spec = pl.BlockSpec((None, s, LANES), lambda bi, h: (bi, 0, h))
    return pl.pallas_call(
        functools.partial(_moba_kernel, tq=tq),
        grid=(b, N_HEADS // 2),
        in_specs=[spec, spec, spec],
        out_specs=spec,
        out_shape=jax.ShapeDtypeStruct((b, s, HEADS_W), BF16),
        compiler_params=_cparams("parallel", "parallel"),
        name="moba_attn",
    )(q, k, v)


def _tail_kernel(a_ref, b_ref, x_ref, wo_ref, g1_ref, beta1_ref, wg_ref, wu_ref, wd_ref, g2_ref, beta2_ref, o_ref, *,
                 layer):
    tm = x_ref.shape[0]
    strip = tm // TAIL_STRIPS
    rows = [slice(r0, r0 + strip) for r0 in range(0, tm, strip)]
    wa = a_ref.shape[1]
    ln = slice(layer, layer + 1)

    def mix(r):
        return _dot(a_ref[r, :], wo_ref[:wa, :]) + _dot(b_ref[r, :], wo_ref[wa:, :])

    def norm1(r, m):
        return _layer_norm_rows(DN_ALPHA * x_ref[r, :] + m, g1_ref[ln, :], beta1_ref[ln, :])

    def ffn(x1):
        xb = x1.astype(BF16)
        gate = _dot(xb, wg_ref[...])
        up = _dot(xb, wu_ref[...])
        h = (gate * (1.0 / (1.0 + jnp.exp(-gate))) * up).astype(BF16)
        return _dot(h, wd_ref[...])

    def norm2(r, x1, y):
        o_ref[r, :] = _layer_norm_rows(DN_ALPHA * x1 + y, g2_ref[ln, :], beta2_ref[ln, :])

    n = len(rows)
    m = [None] * n
    x1 = [None] * n
    y = [None] * n
    m[0] = mix(rows[0])
    for k in range(n):
        if k + 1 < n:
            m[k + 1] = mix(rows[k + 1])
        x1[k] = norm1(rows[k], m[k])
        if k > 0:
            norm2(rows[k - 1], x1[k - 1], y[k - 1])
        y[k] = ffn(x1[k])
    norm2(rows[n - 1], x1[n - 1], y[n - 1])


def _layer_tail(a, b, x2, w_out, g1, beta1, wg, wu, wd, g2, beta2, layer):
    t = x2.shape[0]
    tm = min(TAIL_ROW_TILE, t)
    row = lambda c: pl.BlockSpec((tm, c), lambda i: (i, 0))
    whole = lambda arr: pl.BlockSpec(arr.shape, lambda i: (0, 0), pipeline_mode=pl.Buffered(1))
    of_layer = lambda arr, l: pl.BlockSpec((None,) + arr.shape[1:], lambda i: (l, 0, 0),
                                           pipeline_mode=pl.Buffered(1))
    return pl.pallas_call(
        functools.partial(_tail_kernel, layer=layer),
        grid=(t // tm,),
        in_specs=[row(a.shape[1]), row(b.shape[1]), row(D_MODEL), of_layer(w_out, 0), whole(g1), whole(beta1),
                  of_layer(wg, layer), of_layer(wu, layer), of_layer(wd, layer), whole(g2), whole(beta2)],
        out_specs=row(D_MODEL),
        out_shape=jax.ShapeDtypeStruct((t, D_MODEL), F32),
        compiler_params=_cparams("parallel"),
        name="layer_tail",
    )(a, b, x2, w_out, g1, beta1, wg, wu, wd, g2, beta2)


def _proj_cd_kernel(x_ref, w_ref, u_ref, q_ref, k_ref, v_ref):
    p = _dot(x_ref[...].astype(BF16), w_ref[...])
    u_ref[...] = p[:, 0:512].astype(BF16)
    q_ref[...] = (p[:, 512:1024] * (HEAD_DIM ** -0.5 * LOG2_E)).astype(BF16)
    k_ref[...] = p[:, 1024:1536].astype(BF16)
    v_ref[...] = p[:, 1536:2048].astype(BF16)


def _proj_cd(x2, w):
    t = x2.shape[0]
    tm = min(ROW_TILE, t)
    row = lambda c: pl.BlockSpec((tm, c), lambda i: (i, 0))
    return pl.pallas_call(
        _proj_cd_kernel,
        grid=(t // tm,),
        in_specs=[row(D_MODEL), _full_spec(w.shape)],
        out_specs=[row(HEADS_W)] * 4,
        out_shape=[jax.ShapeDtypeStruct((t, HEADS_W), BF16)] * 4,
        compiler_params=_cparams("parallel"),
        name="proj_cd",
    )(x2, w)


def _s5_kernel(u_ref, bmat_ref, cmat_ref, are_ref, aim_ref, d_ref, wglu_ref, bglu_ref, o_ref,
               x_ref, state_ref):
    nb, ts, _ = u_ref.shape
    rows = ts * nb

    @pl.when(pl.program_id(0) == 0)
    def _():
        state_ref[...] = jnp.zeros_like(state_ref)

    uf = jnp.swapaxes(u_ref[...].astype(F32), 0, 1).reshape(rows, S5_CHANNELS)
    ub = uf.astype(BF16)
    n_chunks = S5_CHANNELS // LANES
    cw = S5_WIDTH // n_chunks
    ys = []
    for j in range(n_chunks):
        re_cols = slice(2 * j * cw, (2 * j + 1) * cw)
        im_cols = slice((2 * j + 1) * cw, (2 * j + 2) * cw)
        x_ref[:, 2 * j * cw:(2 * j + 2) * cw] = _dot(ub[:, j * LANES:(j + 1) * LANES], bmat_ref[j])
        ar = jnp.broadcast_to(are_ref[:, j * cw:(j + 1) * cw], (nb, cw))
        ai = jnp.broadcast_to(aim_ref[:, j * cw:(j + 1) * cw], (nb, cw))
        xr = state_ref[:, re_cols]
        xi = state_ref[:, im_cols]
        for t in range(ts):
            step = slice(t * nb, (t + 1) * nb)
            nr = ar * xr - ai * xi + x_ref[step, re_cols]
            ni = ar * xi + ai * xr + x_ref[step, im_cols]
            x_ref[step, re_cols] = nr
            x_ref[step, im_cols] = ni
            xr, xi = nr, ni
        state_ref[:, re_cols] = xr
        state_ref[:, im_cols] = xi
        ys.append(_dot(x_ref[:, 2 * j * cw:(2 * j + 2) * cw].astype(BF16), cmat_ref[j]))
    y = jnp.concatenate(ys, axis=1) + d_ref[...] * uf
    z = 0.5 * y * (1.0 + jnp.tanh(math.sqrt(2.0 / math.pi) * (y + 0.044715 * (y * y * y))))
    gate = _dot(z.astype(BF16), wglu_ref[...]) + bglu_ref[...]
    out = z * (1.0 / (1.0 + jnp.exp(-gate)))
    o_ref[...] = jnp.swapaxes(out.reshape(ts, nb, S5_CHANNELS), 0, 1).astype(o_ref.dtype)


def _s5(u, bmat, cmat, are, aim, d, wglu, bglu):
    b, s, _ = u.shape
    ts = min(S5_TIME_TILE, s)
    blk = pl.BlockSpec((b, ts, S5_CHANNELS), lambda i: (0, i, 0))
    return pl.pallas_call(
        _s5_kernel,
        grid=(s // ts,),
        in_specs=[blk, _full_spec(bmat.shape), _full_spec(cmat.shape), _full_spec(are.shape),
                  _full_spec(aim.shape), _full_spec(d.shape), _full_spec(wglu.shape), _full_spec(bglu.shape)],
        out_specs=blk,
        out_shape=jax.ShapeDtypeStruct((b, s, S5_CHANNELS), BF16),
        scratch_shapes=[pltpu.VMEM((ts * b, 2 * S5_WIDTH), F32), pltpu.VMEM((b, 2 * S5_WIDTH), F32)],
        compiler_params=_cparams("arbitrary"),
        name="s5_scan",
    )(u, bmat, cmat, are, aim, d, wglu, bglu)


def _rotate_half_cols(w):
    half = w.shape[-1] // 2
    return jnp.concatenate([-w[..., half:], w[..., :half]], axis=-1)


def _layer0_params(w_in, q_norm, w_uq, kv_norm, w_ukv, seq):
    d = w_in.shape[0]
    w_rope = w_in[:, 2048:2080]
    pad_l = jnp.zeros((d, MLA_NOPE), F32)
    pad_r = jnp.zeros((d, LANES - MLA_NOPE - MLA_ROPE), F32)
    w1 = jnp.concatenate([w_in[:, :2048], pad_l, w_rope, pad_r, pad_l, _rotate_half_cols(w_rope), pad_r], axis=1)

    wq3 = w_uq.reshape(MLA_RANK, N_HEADS, MLA_NOPE + MLA_ROPE)
    nope, rope = wq3[..., :MLA_NOPE], wq3[..., MLA_NOPE:]
    z_nope = jnp.zeros_like(nope)
    z_pad = jnp.zeros((MLA_RANK, N_HEADS, LANES - MLA_NOPE - MLA_ROPE), F32)
    wq = jnp.concatenate([nope, rope, z_pad], axis=-1).reshape(MLA_RANK, N_HEADS * LANES)
    wqr = jnp.concatenate([z_nope, _rotate_half_cols(rope), z_pad], axis=-1).reshape(MLA_RANK, N_HEADS * LANES)

    wkv3 = w_ukv.reshape(MLA_RANK, N_HEADS, 2 * HEAD_DIM)
    wkn = jnp.concatenate([wkv3[..., :MLA_NOPE], jnp.zeros((MLA_RANK, N_HEADS, LANES - MLA_NOPE), F32)],
                          axis=-1).reshape(MLA_RANK, N_HEADS * LANES)
    wv = wkv3[..., MLA_NOPE:].reshape(MLA_RANK, HEADS_W)

    half = MLA_ROPE // 2
    freqs = ROPE_BASE ** (-jnp.arange(half, dtype=F32) / half)
    ang = jnp.arange(seq, dtype=F32)[:, None] * freqs
    cos = jnp.concatenate([jnp.cos(ang)] * 2, axis=1)
    sin = jnp.concatenate([jnp.sin(ang)] * 2, axis=1)
    ones = jnp.ones((seq, MLA_NOPE), F32)
    zl = jnp.zeros((seq, MLA_NOPE), F32)
    zr = jnp.zeros((seq, LANES - MLA_NOPE - MLA_ROPE), F32)
    scale = (MLA_NOPE + MLA_ROPE) ** -0.5 * LOG2_E
    cosq = jnp.concatenate([ones, cos, zr], axis=1) * scale
    sinq = jnp.concatenate([zl, sin, zr], axis=1) * scale
    cosk = jnp.concatenate([zl, cos, zr], axis=1)
    sink = jnp.concatenate([zl, sin, zr], axis=1)
    return (w1.astype(BF16), q_norm.reshape(1, -1), kv_norm.reshape(1, -1), wq.astype(BF16), wqr.astype(BF16),
            wkn.astype(BF16), wv.astype(BF16), cosq, sinq, cosk, sink)


def _s5_params(lam_re, lam_im, log_dt, b_re, b_im, c_re, c_im):
    dt = jnp.exp(log_dt)[:, None]
    mag = jnp.exp(lam_re * dt)
    ab_re, ab_im = mag * jnp.cos(lam_im * dt), mag * jnp.sin(lam_im * dt)
    den = lam_re * lam_re + lam_im * lam_im
    nr, ni = ab_re - 1.0, ab_im
    f_re, f_im = (nr * lam_re + ni * lam_im) / den, (ni * lam_re - nr * lam_im) / den
    bb_re = f_re[..., None] * b_re - f_im[..., None] * b_im
    bb_im = f_re[..., None] * b_im + f_im[..., None] * b_re

    gpc = LANES // S5_GROUP
    n_chunks = S5_GROUPS // gpc
    eye = jnp.eye(gpc, dtype=F32)

    def in_blocks(bb):
        t = bb.reshape(n_chunks, gpc, S5_STATE, S5_GROUP)
        return jnp.einsum('cgph,gk->cghkp', t, eye).reshape(n_chunks, gpc * S5_GROUP, gpc * S5_STATE)

    def out_blocks(cc):
        t = cc.reshape(n_chunks, gpc, S5_GROUP, S5_STATE)
        return jnp.einsum('cghp,gk->cgpkh', t, eye).reshape(n_chunks, gpc * S5_STATE, gpc * S5_GROUP)

    bmat = jnp.concatenate([in_blocks(bb_re), in_blocks(bb_im)], axis=2)
    cmat = jnp.concatenate([out_blocks(c_re), -out_blocks(c_im)], axis=1)
    a_re = ab_re.reshape(1, S5_WIDTH)
    a_im = ab_im.reshape(1, S5_WIDTH)
    return bmat.astype(BF16), cmat.astype(BF16), a_re, a_im


def kernel(x, ab_w_in, ab_q_norm, ab_w_uq, ab_kv_norm, ab_w_ukv, ab_w_out, cd_w_in, s5_lambda_re, s5_lambda_im,
           s5_log_dt, s5_b_re, s5_b_im, s5_c_re, s5_c_im, s5_d, s5_w_glu, s5_b_glu, cd_w_out, ln1_g, ln1_b, ln2_g,
           ln2_b, ffn_w_gate, ffn_w_up, ffn_w_down):
    b, s, d = x.shape
    t = b * s
    x2 = x.reshape(t, d)
    vec = lambda a: a.reshape(1, -1)

    p0 = _layer0_params(ab_w_in[0], ab_q_norm[0], ab_w_uq[0], ab_kv_norm[0], ab_w_ukv[0], s)
    qsb, ksb, vsb, qf, kf, vm = _proj_ab(x2, *p0, seq=s)
    sh = lambda a: a.reshape(b, s, a.shape[-1])
    o_sb = _sb_attention(sh(qsb), sh(ksb), sh(vsb)).reshape(t, HEADS_W)
    o_mla = _mla_attention(sh(qf), sh(kf), sh(vm)).reshape(t, HEADS_W)
    wg, wu, wd = ffn_w_gate.astype(BF16), ffn_w_up.astype(BF16), ffn_w_down.astype(BF16)
    x2 = _layer_tail(o_sb, o_mla, x2, ab_w_out.astype(BF16), ln1_g, ln1_b, wg, wu, wd, ln2_g, ln2_b, layer=0)

    u, q, k, v = _proj_cd(x2, cd_w_in[0].astype(BF16))
    bmat, cmat, a_re, a_im = _s5_params(s5_lambda_re[0], s5_lambda_im[0], s5_log_dt[0], s5_b_re[0], s5_b_im[0],
                                        s5_c_re[0], s5_c_im[0])
    o_s5 = _s5(sh(u), bmat, cmat, a_re, a_im, vec(s5_d[0]), s5_w_glu[0].astype(BF16), vec(s5_b_glu[0]))
    o_moba = _moba_attention(sh(q), sh(k), sh(v))
    x2 = _layer_tail(o_s5.reshape(t, S5_CHANNELS), o_moba.reshape(t, HEADS_W), x2, cd_w_out.astype(BF16),
                     ln1_g, ln1_b, wg, wu, wd, ln2_g, ln2_b, layer=1)
    return x2.reshape(b, s, d)
```

```python
import functools
import math

import jax
import jax.numpy as jnp
from jax import lax
from jax.experimental import pallas as pl
from jax.experimental.pallas import tpu as pltpu

F32 = jnp.float32
BF16 = jnp.bfloat16

D_MODEL = 1024
HEAD_DIM = 64
N_HEADS = 8
HEADS_W = N_HEADS * HEAD_DIM
MLA_RANK = 256
MLA_NOPE = 64
MLA_ROPE = 32
ROPE_BASE = 10000.0
S5_CHANNELS = 512
S5_GROUP = 16
S5_GROUPS = 32
S5_STATE = 64
S5_WIDTH = S5_GROUPS * S5_STATE
MOBA_BLOCK = 256
MOBA_TOPK = 3
D_FF = 2816
DEPTH = 2
DN_ALPHA = (2 * DEPTH) ** 0.25
LN_EPS = 1e-5
RMS_EPS = 1e-6

LANES = 128
SUBLANES = 8
BF16_ROWS = 16
NEG_BIG = -1e30
LOG2_E = math.log2(math.e)
VMEM_LIMIT = 56 * 1024 * 1024

ROW_TILE = 1024
TAIL_ROW_TILE = 1024
TAIL_STRIPS = 4
SB_GROUP = 256
SB_Q_TILE = 256
SB_EXIT = -176.0
Q_TILE = 512
S5_TIME_TILE = 128


def _cparams(*sem):
    return pltpu.CompilerParams(dimension_semantics=sem, vmem_limit_bytes=VMEM_LIMIT)


def _full_spec(shape):
    nd = len(shape)
    return pl.BlockSpec(shape, lambda *_: (0,) * nd)


def _dot(a, b):
    return jnp.dot(a, b, preferred_element_type=F32)


def _dot_nt(a, b):
    return lax.dot_general(a, b, (((1,), (1,)), ((), ())), preferred_element_type=F32)


def _layer_norm_rows(r, g, b):
    mu = jnp.mean(r, axis=-1, keepdims=True)
    d = r - mu
    var = jnp.mean(d * d, axis=-1, keepdims=True)
    return d * lax.rsqrt(var + LN_EPS) * g + b


def _proj_ab_kernel(x_ref, w1_ref, gq_ref, gkv_ref, wq_ref, wqr_ref, wkn_ref, wv_ref,
                    cosq_ref, sinq_ref, cosk_ref, sink_ref,
                    qsb_ref, ksb_ref, vsb_ref, qf_ref, kf_ref, vm_ref):
    xb = x_ref[...].astype(BF16)
    p = _dot(xb, w1_ref[...])
    qsb_ref[...] = (p[:, 0:512] * (HEAD_DIM ** -0.5 * LOG2_E)).astype(BF16)
    ksb_ref[...] = p[:, 512:1024].astype(BF16)
    vsb_ref[...] = p[:, 1024:1536].astype(BF16)
    cq = p[:, 1536:1792]
    ckv = p[:, 1792:2048]
    rope = p[:, 2048:2176]
    rope_rot = p[:, 2176:2304]

    cqn = cq * lax.rsqrt(jnp.mean(cq * cq, axis=-1, keepdims=True) + RMS_EPS) * gq_ref[...]
    ckvn = ckv * lax.rsqrt(jnp.mean(ckv * ckv, axis=-1, keepdims=True) + RMS_EPS) * gkv_ref[...]
    cqb = cqn.astype(BF16)
    ckvb = ckvn.astype(BF16)

    cosq = jnp.concatenate([cosq_ref[...]] * N_HEADS, axis=1)
    sinq = jnp.concatenate([sinq_ref[...]] * N_HEADS, axis=1)
    qf = _dot(cqb, wq_ref[...]) * cosq + _dot(cqb, wqr_ref[...]) * sinq
    qf_ref[...] = qf.astype(BF16)

    kpe = rope * cosk_ref[...] + rope_rot * sink_ref[...]
    kf = _dot(ckvb, wkn_ref[...]) + jnp.concatenate([kpe] * N_HEADS, axis=1)
    kf_ref[...] = kf.astype(BF16)
    vm_ref[...] = _dot(ckvb, wv_ref[...]).astype(BF16)


def _proj_ab(x2, w1, gq, gkv, wq, wqr, wkn, wv, cosq, sinq, cosk, sink, seq):
    t = x2.shape[0]
    tm = min(ROW_TILE, seq)
    n_seq_tiles = seq // tm
    row = lambda c: pl.BlockSpec((tm, c), lambda i: (i, 0))
    tab = pl.BlockSpec((tm, LANES), lambda i: (i % n_seq_tiles, 0))
    out_w = (HEADS_W, HEADS_W, HEADS_W, N_HEADS * LANES, N_HEADS * LANES, HEADS_W)
    return pl.pallas_call(
        _proj_ab_kernel,
        grid=(t // tm,),
        in_specs=[row(D_MODEL), _full_spec(w1.shape), _full_spec(gq.shape), _full_spec(gkv.shape),
                  _full_spec(wq.shape), _full_spec(wqr.shape), _full_spec(wkn.shape), _full_spec(wv.shape),
                  tab, tab, tab, tab],
        out_specs=[row(c) for c in out_w],
        out_shape=[jax.ShapeDtypeStruct((t, c), BF16) for c in out_w],
        compiler_params=_cparams("parallel"),
        name="proj_ab",
    )(x2, w1, gq, gkv, wq, wqr, wkn, wv, cosq, sinq, cosk, sink)


def _sb_tile(qh, k2, v2, acc, run, diag, suffix):
    tq = qh.shape[0]
    grp = suffix.shape[0]
    z = _dot_nt(qh, k2)
    if diag:
        r = lax.broadcasted_iota(jnp.int32, (tq, tq), 0)
        c = lax.broadcasted_iota(jnp.int32, (tq, tq), 1)
        own = jnp.where(c < r, z[:, -tq:], NEG_BIG)
        z = own if z.shape[1] == tq else jnp.concatenate([z[:, :-tq], own], axis=1)
    soft = jnp.log2(1.0 + jnp.exp2(-jnp.abs(z)))
    log_beta = jnp.minimum(z, 0.0) - soft
    log_keep = log_beta - z
    hi = log_keep.astype(BF16)
    laters = []
    for g in reversed(range(z.shape[1] // grp)):
        cols = slice(g * grp, (g + 1) * grp)
        laters.insert(0, _dot(hi[:, cols], suffix) + run)
        run = run + jnp.sum(log_keep[:, cols], axis=-1, keepdims=True)
    w = jnp.exp2(log_beta + jnp.concatenate(laters, axis=1))
    return acc + _dot(w.astype(BF16), v2), run


def _sb_attn_kernel(q_ref, k_ref, v_ref, o_ref, acc_ref, run_ref, top_ref, *, tq):
    seq = q_ref.shape[0]
    n_q = seq // tq
    grp = min(SB_GROUP, tq)
    lane = lax.broadcasted_iota(jnp.int32, (1, LANES), 1)
    first = lane < HEAD_DIM
    rg = lax.broadcasted_iota(jnp.int32, (grp, grp), 0)
    cg = lax.broadcasted_iota(jnp.int32, (grp, grp), 1)
    suffix = jnp.where(rg > cg, 1.0, 0.0).astype(BF16)

    def heads_of(q2):
        zero = jnp.zeros_like(q2)
        return jnp.where(first, q2, zero), jnp.where(first, zero, q2)

    for qi in range(n_q):
        qs = qi * tq
        q_heads = heads_of(q_ref[qs:qs + tq, :])
        b0 = max(0, qs - tq)
        w = qs + tq - b0
        accs, runs = [], []
        for h in range(2):
            acc, run = _sb_tile(q_heads[h], k_ref[b0:b0 + w, :], v_ref[b0:b0 + w, :],
                                jnp.zeros((tq, LANES), F32), jnp.zeros((tq, 1), F32), True, suffix)
            accs.append(acc)
            runs.append(run)
            if b0 > 0:
                acc_ref[h, qs:qs + tq, :] = acc
                run_ref[h, qs:qs + tq, :] = run
        if b0 > 0:
            top_ref[qi] = jnp.maximum(jnp.max(runs[0]), jnp.max(runs[1]))
        o_ref[qs:qs + tq, :] = jnp.where(first, accs[0], accs[1]).astype(o_ref.dtype)

    def finish(qi, _):
        @pl.when(top_ref[qi] > SB_EXIT)
        def _():
            rows = pl.ds(pl.multiple_of(qi * tq, tq), tq)
            q_heads = heads_of(q_ref[rows, :])

            def unfinished(st):
                j, _, ra, _, rb = st
                return (j >= 0) & (jnp.maximum(jnp.max(ra), jnp.max(rb)) > SB_EXIT)

            def earlier(st):
                j, acc_a, ra, acc_b, rb = st
                keys = pl.ds(pl.multiple_of(j * tq, tq), tq)
                k2 = k_ref[keys, :]
                v2 = v_ref[keys, :]
                acc_a, ra = _sb_tile(q_heads[0], k2, v2, acc_a, ra, False, suffix)
                acc_b, rb = _sb_tile(q_heads[1], k2, v2, acc_b, rb, False, suffix)
                return j - 1, acc_a, ra, acc_b, rb

            st = lax.while_loop(unfinished, earlier, (qi - 2, acc_ref[0, rows, :], run_ref[0, rows, :],
                                                      acc_ref[1, rows, :], run_ref[1, rows, :]))
            o_ref[rows, :] = jnp.where(first, st[1], st[3]).astype(o_ref.dtype)

        return 0

    lax.fori_loop(2, n_q, finish, 0)


def _sb_attention(q, k, v):
    b, s, _ = q.shape
    tq = min(SB_Q_TILE, s)
    assert s % tq == 0
    spec = pl.BlockSpec((None, s, LANES), lambda bi, h: (bi, 0, h))
    return pl.pallas_call(
        functools.partial(_sb_attn_kernel, tq=tq),
        grid=(b, N_HEADS // 2),
        in_specs=[spec, spec, spec],
        out_specs=spec,
        out_shape=jax.ShapeDtypeStruct((b, s, HEADS_W), BF16),
        scratch_shapes=[pltpu.VMEM((2, s, LANES), F32), pltpu.VMEM((2, s, 1), F32), pltpu.SMEM((s // tq,), F32)],
        compiler_params=_cparams("parallel", "parallel"),
        name="sb_attn",
    )(q, k, v)


def _causal_softmax_pv(s, v_ones, first):
    tq = s.shape[0]
    r = lax.broadcasted_iota(jnp.int32, (tq, tq), 0)
    c = lax.broadcasted_iota(jnp.int32, (tq, tq), 1)
    own = jnp.where(c <= r, s[:, -tq:], NEG_BIG)
    s = own if s.shape[1] == tq else jnp.concatenate([s[:, :-tq], own], axis=1)
    p = jnp.exp2(s - jnp.max(s, axis=-1, keepdims=True))
    acc = _dot(p.astype(BF16), v_ones)
    return acc / pltpu.roll(acc, HEAD_DIM, 1)


def _with_ones(v2, first):
    one = jnp.ones_like(v2)
    return jnp.where(first, v2, one), jnp.where(first, one, v2)


def _mla_attn_kernel(q_ref, k_ref, v_ref, o_ref, *, tq):
    seq = q_ref.shape[0]
    lane = lax.broadcasted_iota(jnp.int32, (1, LANES), 1)
    first = lane < HEAD_DIM
    for qi in reversed(range(seq // tq)):
        rows = slice(qi * tq, (qi + 1) * tq)
        end = (qi + 1) * tq
        v_ones = _with_ones(v_ref[:end, :], first)
        out = [_causal_softmax_pv(_dot_nt(q_ref[rows, h * LANES:(h + 1) * LANES], k_ref[:end, h * LANES:(h + 1) * LANES]),
                                  v_ones[h], first) for h in range(2)]
        o_ref[rows, :] = jnp.where(first, out[0], out[1]).astype(o_ref.dtype)


def _mla_attention(qf, kf, v):
    b, s, _ = v.shape
    tq = min(Q_TILE, s)
    assert s % tq == 0
    kspec = pl.BlockSpec((None, s, 2 * LANES), lambda bi, h: (bi, 0, h))
    vspec = pl.BlockSpec((None, s, LANES), lambda bi, h: (bi, 0, h))
    return pl.pallas_call(
        functools.partial(_mla_attn_kernel, tq=tq),
        grid=(b, N_HEADS // 2),
        in_specs=[kspec, kspec, vspec],
        out_specs=vspec,
        out_shape=jax.ShapeDtypeStruct((b, s, HEADS_W), BF16),
        compiler_params=_cparams("parallel", "parallel"),
        name="mla_attn",
    )(qf, kf, v)


def _moba_kernel(q_ref, k_ref, v_ref, o_ref, *, tq):
    seq = q_ref.shape[0]
    nblk = seq // MOBA_BLOCK
    lane = lax.broadcasted_iota(jnp.int32, (1, LANES), 1)
    first = lane < HEAD_DIM

    means = [jnp.mean(k_ref[n * MOBA_BLOCK:(n + 1) * MOBA_BLOCK, :].astype(F32), axis=0, keepdims=True)
             for n in range(nblk)]
    km = jnp.concatenate(means + [jnp.zeros((BF16_ROWS - nblk, LANES), F32)], axis=0)
    k1 = km.astype(BF16)
    rem = km - k1.astype(F32)
    k2 = rem.astype(BF16)
    k3 = (rem - k2.astype(F32)).astype(BF16)

    rowid = lax.broadcasted_iota(jnp.int32, (SUBLANES, tq), 0)

    def biased_queries(qi):
        q2 = q_ref[qi * tq:(qi + 1) * tq, :]
        zero = jnp.zeros_like(q2)
        qblk = qi * (tq // MOBA_BLOCK) + lax.broadcasted_iota(jnp.int32, (SUBLANES, tq), 1) // MOBA_BLOCK
        valid = rowid < qblk
        q_aug = []
        for qh in (jnp.where(first, q2, zero), jnp.where(first, zero, q2)):
            gate_t = _dot_nt(k1, qh) + _dot_nt(k2, qh) + _dot_nt(k3, qh)
            g = jnp.where(valid, gate_t[0:SUBLANES, :], -jnp.inf)
            beaten = jnp.zeros((SUBLANES, tq), jnp.int32)
            for m in range(nblk):
                gm = g[m:m + 1, :]
                wins = (gm > g) | ((gm == g) & (rowid > m))
                beaten = beaten + jnp.where(wins, 1, 0)
            usable = (valid & (beaten < MOBA_TOPK)) | (rowid == qblk)
            bias_t = jnp.where(usable, 0.0, NEG_BIG)
            bias_t = jnp.concatenate([bias_t, jnp.zeros((LANES - SUBLANES, tq), F32)], axis=0)
            q_aug.append(jnp.concatenate([qh, bias_t.T.astype(BF16)], axis=1))
        return q_aug

    order = list(reversed(range(seq // tq)))
    q_next = biased_queries(order[0])
    for i, qi in enumerate(order):
        q_aug = q_next
        if i + 1 < len(order):
            q_next = biased_queries(order[i + 1])
        rows = slice(qi * tq, (qi + 1) * tq)
        end = (qi + 1) * tq
        blk = lax.broadcasted_iota(jnp.int32, (end, LANES), 0) // MOBA_BLOCK
        onehot = jnp.where(blk == lax.broadcasted_iota(jnp.int32, (end, LANES), 1), 1.0, 0.0).astype(BF16)
        k_aug = jnp.concatenate([k_ref[:end, :], onehot], axis=1)
        v_ones = _with_ones(v_ref[:end, :], first)
        out = [_causal_softmax_pv(_dot_nt(q_aug[h], k_aug), v_ones[h], first) for h in range(2)]
        o_ref[rows, :] = jnp.where(first, out[0], out[1]).astype(o_ref.dtype)


def _moba_attention(q, k, v):
    b, s, _ = q.shape
    tq = min(Q_TILE, s)
    assert s % tq == 0 and tq % MOBA_BLOCK == 0 and s // MOBA_BLOCK <= SUBLANES
    spec = pl.BlockSpec((None, s, LANES), lambda bi, h: (bi, 0, h))
    return pl.pallas_call(
        functools.partial(_moba_kernel, tq=tq),
        grid=(b, N_HEADS // 2),
        in_specs=[spec, spec, spec],
        out_specs=spec,
        out_shape=jax.ShapeDtypeStruct((b, s, HEADS_W), BF16),
        compiler_params=_cparams("parallel", "parallel"),
        name="moba_attn",
    )(q, k, v)


def _tail_kernel(a_ref, b_ref, x_ref, wo_ref, g1_ref, beta1_ref, wg_ref, wu_ref, wd_ref, g2_ref, beta2_ref, o_ref, *,
                 layer):
    tm = x_ref.shape[0]
    strip = tm // TAIL_STRIPS
    rows = [slice(r0, r0 + strip) for r0 in range(0, tm, strip)]
    wa = a_ref.shape[1]
    ln = slice(layer, layer + 1)

    def mix(r):
        return _dot(a_ref[r, :], wo_ref[:wa, :]) + _dot(b_ref[r, :], wo_ref[wa:, :])

    def norm1(r, m):
        return _layer_norm_rows(DN_ALPHA * x_ref[r, :] + m, g1_ref[ln, :], beta1_ref[ln, :])

    def ffn(x1):
        xb = x1.astype(BF16)
        gate = _dot(xb, wg_ref[...])
        up = _dot(xb, wu_ref[...])
        h = (gate * (1.0 / (1.0 + jnp.exp(-gate))) * up).astype(BF16)
        return _dot(h, wd_ref[...])

    def norm2(r, x1, y):
        o_ref[r, :] = _layer_norm_rows(DN_ALPHA * x1 + y, g2_ref[ln, :], beta2_ref[ln, :])

    n = len(rows)
    m = [None] * n
    x1 = [None] * n
    y = [None] * n
    m[0] = mix(rows[0])
    for k in range(n):
        if k + 1 < n:
            m[k + 1] = mix(rows[k + 1])
        x1[k] = norm1(rows[k], m[k])
        if k > 0:
            norm2(rows[k - 1], x1[k - 1], y[k - 1])
        y[k] = ffn(x1[k])
    norm2(rows[n - 1], x1[n - 1], y[n - 1])


def _layer_tail(a, b, x2, w_out, g1, beta1, wg, wu, wd, g2, beta2, layer):
    t = x2.shape[0]
    tm = min(TAIL_ROW_TILE, t)
    row = lambda c: pl.BlockSpec((tm, c), lambda i: (i, 0))
    whole = lambda arr: pl.BlockSpec(arr.shape, lambda i: (0, 0), pipeline_mode=pl.Buffered(1))
    of_layer = lambda arr, l: pl.BlockSpec((None,) + arr.shape[1:], lambda i: (l, 0, 0),
                                           pipeline_mode=pl.Buffered(1))
    return pl.pallas_call(
        functools.partial(_tail_kernel, layer=layer),
        grid=(t // tm,),
        in_specs=[row(a.shape[1]), row(b.shape[1]), row(D_MODEL), of_layer(w_out, 0), whole(g1), whole(beta1),
                  of_layer(wg, layer), of_layer(wu, layer), of_layer(wd, layer), whole(g2), whole(beta2)],
        out_specs=row(D_MODEL),
        out_shape=jax.ShapeDtypeStruct((t, D_MODEL), F32),
        compiler_params=_cparams("parallel"),
        name="layer_tail",
    )(a, b, x2, w_out, g1, beta1, wg, wu, wd, g2, beta2)


def _proj_cd_kernel(x_ref, w_ref, u_ref, q_ref, k_ref, v_ref):
    p = _dot(x_ref[...].astype(BF16), w_ref[...])
    u_ref[...] = p[:, 0:512].astype(BF16)
    q_ref[...] = (p[:, 512:1024] * (HEAD_DIM ** -0.5 * LOG2_E)).astype(BF16)
    k_ref[...] = p[:, 1024:1536].astype(BF16)
    v_ref[...] = p[:, 1536:2048].astype(BF16)


def _proj_cd(x2, w):
    t = x2.shape[0]
    tm = min(ROW_TILE, t)
    row = lambda c: pl.BlockSpec((tm, c), lambda i: (i, 0))
    return pl.pallas_call(
        _proj_cd_kernel,
        grid=(t // tm,),
        in_specs=[row(D_MODEL), _full_spec(w.shape)],
        out_specs=[row(HEADS_W)] * 4,
        out_shape=[jax.ShapeDtypeStruct((t, HEADS_W), BF16)] * 4,
        compiler_params=_cparams("parallel"),
        name="proj_cd",
    )(x2, w)


def _s5_kernel(u_ref, bmat_ref, cmat_ref, are_ref, aim_ref, d_ref, wglu_ref, bglu_ref, o_ref,
               x_ref, state_ref):
    nb, ts, _ = u_ref.shape
    rows = ts * nb

    @pl.when(pl.program_id(0) == 0)
    def _():
        state_ref[...] = jnp.zeros_like(state_ref)

    uf = jnp.swapaxes(u_ref[...].astype(F32), 0, 1).reshape(rows, S5_CHANNELS)
    ub = uf.astype(BF16)
    n_chunks = S5_CHANNELS // LANES
    cw = S5_WIDTH // n_chunks
    ys = []
    for j in range(n_chunks):
        re_cols = slice(2 * j * cw, (2 * j + 1) * cw)
        im_cols = slice((2 * j + 1) * cw, (2 * j + 2) * cw)
        x_ref[:, 2 * j * cw:(2 * j + 2) * cw] = _dot(ub[:, j * LANES:(j + 1) * LANES], bmat_ref[j])
        ar = jnp.broadcast_to(are_ref[:, j * cw:(j + 1) * cw], (nb, cw))
        ai = jnp.broadcast_to(aim_ref[:, j * cw:(j + 1) * cw], (nb, cw))
        xr = state_ref[:, re_cols]
        xi = state_ref[:, im_cols]
        for t in range(ts):
            step = slice(t * nb, (t + 1) * nb)
            nr = ar * xr - ai * xi + x_ref[step, re_cols]
            ni = ar * xi + ai * xr + x_ref[step, im_cols]
            x_ref[step, re_cols] = nr
            x_ref[step, im_cols] = ni
            xr, xi = nr, ni
        state_ref[:, re_cols] = xr
        state_ref[:, im_cols] = xi
        ys.append(_dot(x_ref[:, 2 * j * cw:(2 * j + 2) * cw].astype(BF16), cmat_ref[j]))
    y = jnp.concatenate(ys, axis=1) + d_ref[...] * uf
    z = 0.5 * y * (1.0 + jnp.tanh(math.sqrt(2.0 / math.pi) * (y + 0.044715 * (y * y * y))))
    gate = _dot(z.astype(BF16), wglu_ref[...]) + bglu_ref[...]
    out = z * (1.0 / (1.0 + jnp.exp(-gate)))
    o_ref[...] = jnp.swapaxes(out.reshape(ts, nb, S5_CHANNELS), 0, 1).astype(o_ref.dtype)


def _s5(u, bmat, cmat, are, aim, d, wglu, bglu):
    b, s, _ = u.shape
    ts = min(S5_TIME_TILE, s)
    blk = pl.BlockSpec((b, ts, S5_CHANNELS), lambda i: (0, i, 0))
    return pl.pallas_call(
        _s5_kernel,
        grid=(s // ts,),
        in_specs=[blk, _full_spec(bmat.shape), _full_spec(cmat.shape), _full_spec(are.shape),
                  _full_spec(aim.shape), _full_spec(d.shape), _full_spec(wglu.shape), _full_spec(bglu.shape)],
        out_specs=blk,
        out_shape=jax.ShapeDtypeStruct((b, s, S5_CHANNELS), BF16),
        scratch_shapes=[pltpu.VMEM((ts * b, 2 * S5_WIDTH), F32), pltpu.VMEM((b, 2 * S5_WIDTH), F32)],
        compiler_params=_cparams("arbitrary"),
        name="s5_scan",
    )(u, bmat, cmat, are, aim, d, wglu, bglu)


def _rotate_half_cols(w):
    half = w.shape[-1] // 2
    return jnp.concatenate([-w[..., half:], w[..., :half]], axis=-1)


def _layer0_params(w_in, q_norm, w_uq, kv_norm, w_ukv, seq):
    d = w_in.shape[0]
    w_rope = w_in[:, 2048:2080]
    pad_l = jnp.zeros((d, MLA_NOPE), F32)
    pad_r = jnp.zeros((d, LANES - MLA_NOPE - MLA_ROPE), F32)
    w1 = jnp.concatenate([w_in[:, :2048], pad_l, w_rope, pad_r, pad_l, _rotate_half_cols(w_rope), pad_r], axis=1)

    wq3 = w_uq.reshape(MLA_RANK, N_HEADS, MLA_NOPE + MLA_ROPE)
    nope, rope = wq3[..., :MLA_NOPE], wq3[..., MLA_NOPE:]
    z_nope = jnp.zeros_like(nope)
    z_pad = jnp.zeros((MLA_RANK, N_HEADS, LANES - MLA_NOPE - MLA_ROPE), F32)
    wq = jnp.concatenate([nope, rope, z_pad], axis=-1).reshape(MLA_RANK, N_HEADS * LANES)
    wqr = jnp.concatenate([z_nope, _rotate_half_cols(rope), z_pad], axis=-1).reshape(MLA_RANK, N_HEADS * LANES)

    wkv3 = w_ukv.reshape(MLA_RANK, N_HEADS, 2 * HEAD_DIM)
    wkn = jnp.concatenate([wkv3[..., :MLA_NOPE], jnp.zeros((MLA_RANK, N_HEADS, LANES - MLA_NOPE), F32)],
                          axis=-1).reshape(MLA_RANK, N_HEADS * LANES)
    wv = wkv3[..., MLA_NOPE:].reshape(MLA_RANK, HEADS_W)

    half = MLA_ROPE // 2
    freqs = ROPE_BASE ** (-jnp.arange(half, dtype=F32) / half)
    ang = jnp.arange(seq, dtype=F32)[:, None] * freqs
    cos = jnp.concatenate([jnp.cos(ang)] * 2, axis=1)
    sin = jnp.concatenate([jnp.sin(ang)] * 2, axis=1)
    ones = jnp.ones((seq, MLA_NOPE), F32)
    zl = jnp.zeros((seq, MLA_NOPE), F32)
    zr = jnp.zeros((seq, LANES - MLA_NOPE - MLA_ROPE), F32)
    scale = (MLA_NOPE + MLA_ROPE) ** -0.5 * LOG2_E
    cosq = jnp.concatenate([ones, cos, zr], axis=1) * scale
    sinq = jnp.concatenate([zl, sin, zr], axis=1) * scale
    cosk = jnp.concatenate([zl, cos, zr], axis=1)
    sink = jnp.concatenate([zl, sin, zr], axis=1)
    return (w1.astype(BF16), q_norm.reshape(1, -1), kv_norm.reshape(1, -1), wq.astype(BF16), wqr.astype(BF16),
            wkn.astype(BF16), wv.astype(BF16), cosq, sinq, cosk, sink)


def _s5_params(lam_re, lam_im, log_dt, b_re, b_im, c_re, c_im):
    dt = jnp.exp(log_dt)[:, None]
    mag = jnp.exp(lam_re * dt)
    ab_re, ab_im = mag * jnp.cos(lam_im * dt), mag * jnp.sin(lam_im * dt)
    den = lam_re * lam_re + lam_im * lam_im
    nr, ni = ab_re - 1.0, ab_im
    f_re, f_im = (nr * lam_re + ni * lam_im) / den, (ni * lam_re - nr * lam_im) / den
    bb_re = f_re[..., None] * b_re - f_im[..., None] * b_im
    bb_im = f_re[..., None] * b_im + f_im[..., None] * b_re

    gpc = LANES // S5_GROUP
    n_chunks = S5_GROUPS // gpc
    eye = jnp.eye(gpc, dtype=F32)

    def in_blocks(bb):
        t = bb.reshape(n_chunks, gpc, S5_STATE, S5_GROUP)
        return jnp.einsum('cgph,gk->cghkp', t, eye).reshape(n_chunks, gpc * S5_GROUP, gpc * S5_STATE)

    def out_blocks(cc):
        t = cc.reshape(n_chunks, gpc, S5_GROUP, S5_STATE)
        return jnp.einsum('cghp,gk->cgpkh', t, eye).reshape(n_chunks, gpc * S5_STATE, gpc * S5_GROUP)

    bmat = jnp.concatenate([in_blocks(bb_re), in_blocks(bb_im)], axis=2)
    cmat = jnp.concatenate([out_blocks(c_re), -out_blocks(c_im)], axis=1)
    a_re = ab_re.reshape(1, S5_WIDTH)
    a_im = ab_im.reshape(1, S5_WIDTH)
    return bmat.astype(BF16), cmat.astype(BF16), a_re, a_im


def kernel(x, ab_w_in, ab_q_norm, ab_w_uq, ab_kv_norm, ab_w_ukv, ab_w_out, cd_w_in, s5_lambda_re, s5_lambda_im,
           s5_log_dt, s5_b_re, s5_b_im, s5_c_re, s5_c_im, s5_d, s5_w_glu, s5_b_glu, cd_w_out, ln1_g, ln1_b, ln2_g,
           ln2_b, ffn_w_gate, ffn_w_up, ffn_w_down):
    b, s, d = x.shape
    t = b * s
    x2 = x.reshape(t, d)
    vec = lambda a: a.reshape(1, -1)

    p0 = _layer0_params(ab_w_in[0], ab_q_norm[0], ab_w_uq[0], ab_kv_norm[0], ab_w_ukv[0], s)
    qsb, ksb, vsb, qf, kf, vm = _proj_ab(x2, *p0, seq=s)
    sh = lambda a: a.reshape(b, s, a.shape[-1])
    o_sb = _sb_attention(sh(qsb), sh(ksb), sh(vsb)).reshape(t, HEADS_W)
    o_mla = _mla_attention(sh(qf), sh(kf), sh(vm)).reshape(t, HEADS_W)
    wg, wu, wd = ffn_w_gate.astype(BF16), ffn_w_up.astype(BF16), ffn_w_down.astype(BF16)
    x2 = _layer_tail(o_sb, o_mla, x2, ab_w_out.astype(BF16), ln1_g, ln1_b, wg, wu, wd, ln2_g, ln2_b, layer=0)

    u, q, k, v = _proj_cd(x2, cd_w_in[0].astype(BF16))
    bmat, cmat, a_re, a_im = _s5_params(s5_lambda_re[0], s5_lambda_im[0], s5_log_dt[0], s5_b_re[0], s5_b_im[0],
                                        s5_c_re[0], s5_c_im[0])
    o_s5 = _s5(sh(u), bmat, cmat, a_re, a_im, vec(s5_d[0]), s5_w_glu[0].astype(BF16), vec(s5_b_glu[0]))
    o_moba = _moba_attention(sh(q), sh(k), sh(v))
    x2 = _layer_tail(o_s5.reshape(t, S5_CHANNELS), o_moba.reshape(t, HEADS_W), x2, cd_w_out.astype(BF16),
                     ln1_g, ln1_b, wg, wu, wd, ln2_g, ln2_b, layer=1)
    return x2.reshape(b, s, d)
```

```python
import functools
import math

import jax
import jax.numpy as jnp
from jax import lax
from jax.experimental import pallas as pl
from jax.experimental.pallas import tpu as pltpu

F32 = jnp.float32
BF16 = jnp.bfloat16

D_MODEL = 1024
HEAD_DIM = 64
N_HEADS = 8
HEADS_W = N_HEADS * HEAD_DIM
MLA_RANK = 256
MLA_NOPE = 64
MLA_ROPE = 32
ROPE_BASE = 10000.0
S5_CHANNELS = 512
S5_GROUP = 16
S5_GROUPS = 32
S5_STATE = 64
S5_WIDTH = S5_GROUPS * S5_STATE
MOBA_BLOCK = 256
MOBA_TOPK = 3
D_FF = 2816
DEPTH = 2
DN_ALPHA = (2 * DEPTH) ** 0.25
LN_EPS = 1e-5
RMS_EPS = 1e-6

LANES = 128
SUBLANES = 8
BF16_ROWS = 16
NEG_BIG = -1e30
LOG2_E = math.log2(math.e)
VMEM_LIMIT = 56 * 1024 * 1024

ROW_TILE = 1024
TAIL_ROW_TILE = 1024
TAIL_STRIPS = 4
SB_GROUP = 256
SB_Q_TILE = 256
SB_EXIT = -176.0
Q_TILE = 512
S5_TIME_TILE = 128


def _cparams(*sem):
    return pltpu.CompilerParams(dimension_semantics=sem, vmem_limit_bytes=VMEM_LIMIT)


def _full_spec(shape):
    nd = len(shape)
    return pl.BlockSpec(shape, lambda *_: (0,) * nd)


def _dot(a, b):
    return jnp.dot(a, b, preferred_element_type=F32)


def _dot_nt(a, b):
    return lax.dot_general(a, b, (((1,), (1,)), ((), ())), preferred_element_type=F32)


def _layer_norm_rows(r, g, b):
    mu = jnp.mean(r, axis=-1, keepdims=True)
    d = r - mu
    var = jnp.mean(d * d, axis=-1, keepdims=True)
    return d * lax.rsqrt(var + LN_EPS) * g + b


def _proj_ab_kernel(x_ref, w1_ref, gq_ref, gkv_ref, wq_ref, wqr_ref, wkn_ref, wv_ref,
                    cosq_ref, sinq_ref, cosk_ref, sink_ref,
                    qsb_ref, ksb_ref, vsb_ref, qf_ref, kf_ref, vm_ref):
    xb = x_ref[...].astype(BF16)
    p = _dot(xb, w1_ref[...])
    qsb_ref[...] = (p[:, 0:512] * (HEAD_DIM ** -0.5 * LOG2_E)).astype(BF16)
    ksb_ref[...] = p[:, 512:1024].astype(BF16)
    vsb_ref[...] = p[:, 1024:1536].astype(BF16)
    cq = p[:, 1536:1792]
    ckv = p[:, 1792:2048]
    rope = p[:, 2048:2176]
    rope_rot = p[:, 2176:2304]

    cqn = cq * lax.rsqrt(jnp.mean(cq * cq, axis=-1, keepdims=True) + RMS_EPS) * gq_ref[...]
    ckvn = ckv * lax.rsqrt(jnp.mean(ckv * ckv, axis=-1, keepdims=True) + RMS_EPS) * gkv_ref[...]
    cqb = cqn.astype(BF16)
    ckvb = ckvn.astype(BF16)

    cosq = jnp.concatenate([cosq_ref[...]] * N_HEADS, axis=1)
    sinq = jnp.concatenate([sinq_ref[...]] * N_HEADS, axis=1)
    qf = _dot(cqb, wq_ref[...]) * cosq + _dot(cqb, wqr_ref[...]) * sinq
    qf_ref[...] = qf.astype(BF16)

    kpe = rope * cosk_ref[...] + rope_rot * sink_ref[...]
    kf = _dot(ckvb, wkn_ref[...]) + jnp.concatenate([kpe] * N_HEADS, axis=1)
    kf_ref[...] = kf.astype(BF16)
    vm_ref[...] = _dot(ckvb, wv_ref[...]).astype(BF16)


def _proj_ab(x2, w1, gq, gkv, wq, wqr, wkn, wv, cosq, sinq, cosk, sink, seq):
    t = x2.shape[0]
    tm = min(ROW_TILE, seq)
    n_seq_tiles = seq // tm
    row = lambda c: pl.BlockSpec((tm, c), lambda i: (i, 0))
    tab = pl.BlockSpec((tm, LANES), lambda i: (i % n_seq_tiles, 0))
    out_w = (HEADS_W, HEADS_W, HEADS_W, N_HEADS * LANES, N_HEADS * LANES, HEADS_W)
    return pl.pallas_call(
        _proj_ab_kernel,
        grid=(t // tm,),
        in_specs=[row(D_MODEL), _full_spec(w1.shape), _full_spec(gq.shape), _full_spec(gkv.shape),
                  _full_spec(wq.shape), _full_spec(wqr.shape), _full_spec(wkn.shape), _full_spec(wv.shape),
                  tab, tab, tab, tab],
        out_specs=[row(c) for c in out_w],
        out_shape=[jax.ShapeDtypeStruct((t, c), BF16) for c in out_w],
        compiler_params=_cparams("parallel"),
        name="proj_ab",
    )(x2, w1, gq, gkv, wq, wqr, wkn, wv, cosq, sinq, cosk, sink)


def _sb_tile(qh, k2, v2, acc, run, diag, suffix):
    tq = qh.shape[0]
    grp = suffix.shape[0]
    z = _dot_nt(qh, k2)
    if diag:
        r = lax.broadcasted_iota(jnp.int32, (tq, tq), 0)
        c = lax.broadcasted_iota(jnp.int32, (tq, tq), 1)
        own = jnp.where(c < r, z[:, -tq:], NEG_BIG)
        z = own if z.shape[1] == tq else jnp.concatenate([z[:, :-tq], own], axis=1)
    soft = jnp.log2(1.0 + jnp.exp2(-jnp.abs(z)))
    log_beta = jnp.minimum(z, 0.0) - soft
    log_keep = log_beta - z
    hi = log_keep.astype(BF16)
    laters = []
    for g in reversed(range(z.shape[1] // grp)):
        cols = slice(g * grp, (g + 1) * grp)
        laters.insert(0, _dot(hi[:, cols], suffix) + run)
        run = run + jnp.sum(log_keep[:, cols], axis=-1, keepdims=True)
    w = jnp.exp2(log_beta + jnp.concatenate(laters, axis=1))
    return acc + _dot(w.astype(BF16), v2), run


def _pair_masks():
    lane = lax.broadcasted_iota(jnp.int32, (1, LANES), 1)
    return lane < HEAD_DIM


def _suffix_matrix(grp):
    rg = lax.broadcasted_iota(jnp.int32, (grp, grp), 0)
    cg = lax.broadcasted_iota(jnp.int32, (grp, grp), 1)
    return jnp.where(rg > cg, 1.0, 0.0).astype(BF16)


def _split_heads(q2, first):
    zero = jnp.zeros_like(q2)
    return jnp.where(first, q2, zero), jnp.where(first, zero, q2)


def _sb_band_tile(qi, q_ref, k_ref, v_ref, o_ref, acc_ref, run_ref, top_ref, first, suffix, tq):
    qs = qi * tq
    q_heads = _split_heads(q_ref[qs:qs + tq, :], first)
    b0 = max(0, qs - tq)
    w = qs + tq - b0
    accs, runs = [], []
    for h in range(2):
        acc, run = _sb_tile(q_heads[h], k_ref[b0:b0 + w, :], v_ref[b0:b0 + w, :],
                            jnp.zeros((tq, LANES), F32), jnp.zeros((tq, 1), F32), True, suffix)
        accs.append(acc)
        runs.append(run)
        if b0 > 0:
            acc_ref[h, qs:qs + tq, :] = acc
            run_ref[h, qs:qs + tq, :] = run
    if b0 > 0:
        top_ref[qi] = jnp.maximum(jnp.max(runs[0]), jnp.max(runs[1]))
    o_ref[qs:qs + tq, :] = jnp.where(first, accs[0], accs[1]).astype(o_ref.dtype)


def _sb_finish(q_ref, k_ref, v_ref, o_ref, acc_ref, run_ref, top_ref, first, suffix, tq):
    def finish(qi, _):
        @pl.when(top_ref[qi] > SB_EXIT)
        def _():
            rows = pl.ds(pl.multiple_of(qi * tq, tq), tq)
            q_heads = _split_heads(q_ref[rows, :], first)

            def unfinished(st):
                j, _, ra, _, rb = st
                return (j >= 0) & (jnp.maximum(jnp.max(ra), jnp.max(rb)) > SB_EXIT)

            def earlier(st):
                j, acc_a, ra, acc_b, rb = st
                keys = pl.ds(pl.multiple_of(j * tq, tq), tq)
                k2 = k_ref[keys, :]
                v2 = v_ref[keys, :]
                acc_a, ra = _sb_tile(q_heads[0], k2, v2, acc_a, ra, False, suffix)
                acc_b, rb = _sb_tile(q_heads[1], k2, v2, acc_b, rb, False, suffix)
                return j - 1, acc_a, ra, acc_b, rb

            st = lax.while_loop(unfinished, earlier, (qi - 2, acc_ref[0, rows, :], run_ref[0, rows, :],
                                                      acc_ref[1, rows, :], run_ref[1, rows, :]))
            o_ref[rows, :] = jnp.where(first, st[1], st[3]).astype(o_ref.dtype)

        return 0

    lax.fori_loop(2, q_ref.shape[0] // tq, finish, 0)


def _mla_tile(qi, q_ref, k_ref, v_ref, o_ref, first, tq):
    rows = slice(qi * tq, (qi + 1) * tq)
    end = (qi + 1) * tq
    v_ones = _with_ones(v_ref[:end, :], first)
    out = [_causal_softmax_pv(_dot_nt(q_ref[rows, h * LANES:(h + 1) * LANES], k_ref[:end, h * LANES:(h + 1) * LANES]),
                              v_ones[h], first) for h in range(2)]
    o_ref[rows, :] = jnp.where(first, out[0], out[1]).astype(o_ref.dtype)


def _layer0_attn_kernel(qs_ref, ks_ref, vs_ref, qm_ref, km_ref, vm_ref, osb_ref, omla_ref, acc_ref, run_ref, top_ref, *,
                        tq_sb, tq_mla):
    seq = qs_ref.shape[0]
    first = _pair_masks()
    suffix = _suffix_matrix(min(SB_GROUP, tq_sb))
    sb_tiles = list(reversed(range(seq // tq_sb)))
    mla_tiles = list(reversed(range(seq // tq_mla)))
    per = -(-len(sb_tiles) // len(mla_tiles))
    while sb_tiles or mla_tiles:
        if mla_tiles:
            _mla_tile(mla_tiles.pop(0), qm_ref, km_ref, vm_ref, omla_ref, first, tq_mla)
        for _ in range(per):
            if sb_tiles:
                _sb_band_tile(sb_tiles.pop(0), qs_ref, ks_ref, vs_ref, osb_ref, acc_ref, run_ref, top_ref, first,
                              suffix, tq_sb)
    _sb_finish(qs_ref, ks_ref, vs_ref, osb_ref, acc_ref, run_ref, top_ref, first, suffix, tq_sb)


def _layer0_attention(qsb, ksb, vsb, qf, kf, vm):
    b, s, _ = qsb.shape
    tq_sb, tq_mla = min(SB_Q_TILE, s), min(Q_TILE, s)
    assert s % tq_sb == 0 and s % tq_mla == 0
    pair = pl.BlockSpec((None, s, LANES), lambda bi, h: (bi, 0, h))
    wide = pl.BlockSpec((None, s, 2 * LANES), lambda bi, h: (bi, 0, h))
    out = jax.ShapeDtypeStruct((b, s, HEADS_W), BF16)
    return pl.pallas_call(
        functools.partial(_layer0_attn_kernel, tq_sb=tq_sb, tq_mla=tq_mla),
        grid=(b, N_HEADS // 2),
        in_specs=[pair, pair, pair, wide, wide, pair],
        out_specs=[pair, pair],
        out_shape=[out, out],
        scratch_shapes=[pltpu.VMEM((2, s, LANES), F32), pltpu.VMEM((2, s, 1), F32),
                        pltpu.SMEM((s // tq_sb,), F32)],
        compiler_params=_cparams("parallel", "parallel"),
        name="sb_mla_attn",
    )(qsb, ksb, vsb, qf, kf, vm)


def _causal_softmax_pv(s, v_ones, first):
    tq = s.shape[0]
    r = lax.broadcasted_iota(jnp.int32, (tq, tq), 0)
    c = lax.broadcasted_iota(jnp.int32, (tq, tq), 1)
    own = jnp.where(c <= r, s[:, -tq:], NEG_BIG)
    s = own if s.shape[1] == tq else jnp.concatenate([s[:, :-tq], own], axis=1)
    p = jnp.exp2(s - jnp.max(s, axis=-1, keepdims=True))
    acc = _dot(p.astype(BF16), v_ones)
    return acc / pltpu.roll(acc, HEAD_DIM, 1)


def _with_ones(v2, first):
    one = jnp.ones_like(v2)
    return jnp.where(first, v2, one), jnp.where(first, one, v2)


def _moba_kernel(q_ref, k_ref, v_ref, o_ref, *, tq):
    seq = q_ref.shape[0]
    nblk = seq // MOBA_BLOCK
    first = _pair_masks()

    means = [jnp.mean(k_ref[n * MOBA_BLOCK:(n + 1) * MOBA_BLOCK, :].astype(F32), axis=0, keepdims=True)
             for n in range(nblk)]
    km = jnp.concatenate(means + [jnp.zeros((BF16_ROWS - nblk, LANES), F32)], axis=0)
    k1 = km.astype(BF16)
    rem = km - k1.astype(F32)
    k2 = rem.astype(BF16)
    k3 = (rem - k2.astype(F32)).astype(BF16)

    rowid = lax.broadcasted_iota(jnp.int32, (SUBLANES, tq), 0)

    def biased_queries(qi):
        q2 = q_ref[qi * tq:(qi + 1) * tq, :]
        zero = jnp.zeros_like(q2)
        qblk = qi * (tq // MOBA_BLOCK) + lax.broadcasted_iota(jnp.int32, (SUBLANES, tq), 1) // MOBA_BLOCK
        valid = rowid < qblk
        q_aug = []
        for qh in (jnp.where(first, q2, zero), jnp.where(first, zero, q2)):
            gate_t = _dot_nt(k1, qh) + _dot_nt(k2, qh) + _dot_nt(k3, qh)
            g = jnp.where(valid, gate_t[0:SUBLANES, :], -jnp.inf)
            beaten = jnp.zeros((SUBLANES, tq), jnp.int32)
            for m in range(nblk):
                gm = g[m:m + 1, :]
                wins = (gm > g) | ((gm == g) & (rowid > m))
                beaten = beaten + jnp.where(wins, 1, 0)
            usable = (valid & (beaten < MOBA_TOPK)) | (rowid == qblk)
            bias_t = jnp.where(usable, 0.0, NEG_BIG)
            bias_t = jnp.concatenate([bias_t, jnp.zeros((LANES - SUBLANES, tq), F32)], axis=0)
            q_aug.append(jnp.concatenate([qh, bias_t.T.astype(BF16)], axis=1))
        return q_aug

    order = list(reversed(range(seq // tq)))
    q_next = biased_queries(order[0])
    for i, qi in enumerate(order):
        q_aug = q_next
        if i + 1 < len(order):
            q_next = biased_queries(order[i + 1])
        rows = slice(qi * tq, (qi + 1) * tq)
        end = (qi + 1) * tq
        blk = lax.broadcasted_iota(jnp.int32, (end, LANES), 0) // MOBA_BLOCK
        onehot = jnp.where(blk == lax.broadcasted_iota(jnp.int32, (end, LANES), 1), 1.0, 0.0).astype(BF16)
        k_aug = jnp.concatenate([k_ref[:end, :], onehot], axis=1)
        v_ones = _with_ones(v_ref[:end, :], first)
        out = [_causal_softmax_pv(_dot_nt(q_aug[h], k_aug), v_ones[h], first) for h in range(2)]
        o_ref[rows, :] = jnp.where(first, out[0], out[1]).astype(o_ref.dtype)


def _moba_attention(q, k, v):
    b, s, _ = q.shape
    tq = min(Q_TILE, s)
    assert s % tq == 0 and tq % MOBA_BLOCK == 0 and s // MOBA_BLOCK <= SUBLANES
    spec = pl.BlockSpec((None, s, LANES), lambda bi, h: (bi, 0, h))
    return pl.pallas_call(
        functools.partial(_moba_kernel, tq=tq),
        grid=(b, N_HEADS // 2),
        in_specs=[spec, spec, spec],
        out_specs=spec,
        out_shape=jax.ShapeDtypeStruct((b, s, HEADS_W), BF16),
        compiler_params=_cparams("parallel", "parallel"),
        name="moba_attn",
    )(q, k, v)


def _tail_kernel(a_ref, b_ref, x_ref, wo_ref, g1_ref, beta1_ref, wg_ref, wu_ref, wd_ref, g2_ref, beta2_ref, o_ref, *,
                 layer):
    tm = x_ref.shape[0]
    strip = tm // TAIL_STRIPS
    rows = [slice(r0, r0 + strip) for r0 in range(0, tm, strip)]
    wa = a_ref.shape[1]
    ln = slice(layer, layer + 1)

    def mix(r):
        return _dot(a_ref[r, :], wo_ref[:wa, :]) + _dot(b_ref[r, :], wo_ref[wa:, :])

    def norm1(r, m):
        return _layer_norm_rows(DN_ALPHA * x_ref[r, :] + m, g1_ref[ln, :], beta1_ref[ln, :])

    def ffn(x1):
        xb = x1.astype(BF16)
        gate = _dot(xb, wg_ref[...])
        up = _dot(xb, wu_ref[...])
        h = (gate * (1.0 / (1.0 + jnp.exp(-gate))) * up).astype(BF16)
        return _dot(h, wd_ref[...])

    def norm2(r, x1, y):
        o_ref[r, :] = _layer_norm_rows(DN_ALPHA * x1 + y, g2_ref[ln, :], beta2_ref[ln, :])

    n = len(rows)
    m = [None] * n
    x1 = [None] * n
    y = [None] * n
    m[0] = mix(rows[0])
    for k in range(n):
        if k + 1 < n:
            m[k + 1] = mix(rows[k + 1])
        x1[k] = norm1(rows[k], m[k])
        if k > 0:
            norm2(rows[k - 1], x1[k - 1], y[k - 1])
        y[k] = ffn(x1[k])
    norm2(rows[n - 1], x1[n - 1], y[n - 1])


def _layer_tail(a, b, x2, w_out, g1, beta1, wg, wu, wd, g2, beta2, layer):
    t = x2.shape[0]
    tm = min(TAIL_ROW_TILE, t)
    row = lambda c: pl.BlockSpec((tm, c), lambda i: (i, 0))
    whole = lambda arr: pl.BlockSpec(arr.shape, lambda i: (0, 0), pipeline_mode=pl.Buffered(1))
    of_layer = lambda arr, l: pl.BlockSpec((None,) + arr.shape[1:], lambda i: (l, 0, 0),
                                           pipeline_mode=pl.Buffered(1))
    return pl.pallas_call(
        functools.partial(_tail_kernel, layer=layer),
        grid=(t // tm,),
        in_specs=[row(a.shape[1]), row(b.shape[1]), row(D_MODEL), of_layer(w_out, 0), whole(g1), whole(beta1),
                  of_layer(wg, layer), of_layer(wu, layer), of_layer(wd, layer), whole(g2), whole(beta2)],
        out_specs=row(D_MODEL),
        out_shape=jax.ShapeDtypeStruct((t, D_MODEL), F32),
        compiler_params=_cparams("parallel"),
        name="layer_tail",
    )(a, b, x2, w_out, g1, beta1, wg, wu, wd, g2, beta2)


def _proj_cd_kernel(x_ref, w_ref, u_ref, q_ref, k_ref, v_ref):
    p = _dot(x_ref[...].astype(BF16), w_ref[...])
    u_ref[...] = p[:, 0:512].astype(BF16)
    q_ref[...] = (p[:, 512:1024] * (HEAD_DIM ** -0.5 * LOG2_E)).astype(BF16)
    k_ref[...] = p[:, 1024:1536].astype(BF16)
    v_ref[...] = p[:, 1536:2048].astype(BF16)


def _proj_cd(x2, w):
    t = x2.shape[0]
    tm = min(ROW_TILE, t)
    row = lambda c: pl.BlockSpec((tm, c), lambda i: (i, 0))
    return pl.pallas_call(
        _proj_cd_kernel,
        grid=(t // tm,),
        in_specs=[row(D_MODEL), _full_spec(w.shape)],
        out_specs=[row(HEADS_W)] * 4,
        out_shape=[jax.ShapeDtypeStruct((t, HEADS_W), BF16)] * 4,
        compiler_params=_cparams("parallel"),
        name="proj_cd",
    )(x2, w)


def _s5_kernel(u_ref, bmat_ref, cmat_ref, are_ref, aim_ref, d_ref, wglu_ref, bglu_ref, o_ref,
               x_ref, state_ref):
    nb, ts, _ = u_ref.shape
    rows = ts * nb

    @pl.when(pl.program_id(0) == 0)
    def _():
        state_ref[...] = jnp.zeros_like(state_ref)

    uf = jnp.swapaxes(u_ref[...].astype(F32), 0, 1).reshape(rows, S5_CHANNELS)
    ub = uf.astype(BF16)
    n_chunks = S5_CHANNELS // LANES
    cw = S5_WIDTH // n_chunks
    ys = []
    for j in range(n_chunks):
        re_cols = slice(2 * j * cw, (2 * j + 1) * cw)
        im_cols = slice((2 * j + 1) * cw, (2 * j + 2) * cw)
        x_ref[:, 2 * j * cw:(2 * j + 2) * cw] = _dot(ub[:, j * LANES:(j + 1) * LANES], bmat_ref[j])
        ar = jnp.broadcast_to(are_ref[:, j * cw:(j + 1) * cw], (nb, cw))
        ai = jnp.broadcast_to(aim_ref[:, j * cw:(j + 1) * cw], (nb, cw))
        xr = state_ref[:, re_cols]
        xi = state_ref[:, im_cols]
        for t in range(ts):
            step = slice(t * nb, (t + 1) * nb)
            nr = ar * xr - ai * xi + x_ref[step, re_cols]
            ni = ar * xi + ai * xr + x_ref[step, im_cols]
            x_ref[step, re_cols] = nr
            x_ref[step, im_cols] = ni
            xr, xi = nr, ni
        state_ref[:, re_cols] = xr
        state_ref[:, im_cols] = xi
        ys.append(_dot(x_ref[:, 2 * j * cw:(2 * j + 2) * cw].astype(BF16), cmat_ref[j]))
    y = jnp.concatenate(ys, axis=1) + d_ref[...] * uf
    z = 0.5 * y * (1.0 + jnp.tanh(math.sqrt(2.0 / math.pi) * (y + 0.044715 * (y * y * y))))
    gate = _dot(z.astype(BF16), wglu_ref[...]) + bglu_ref[...]
    out = z * (1.0 / (1.0 + jnp.exp(-gate)))
    o_ref[...] = jnp.swapaxes(out.reshape(ts, nb, S5_CHANNELS), 0, 1).astype(o_ref.dtype)


def _s5(u, bmat, cmat, are, aim, d, wglu, bglu):
    b, s, _ = u.shape
    ts = min(S5_TIME_TILE, s)
    blk = pl.BlockSpec((b, ts, S5_CHANNELS), lambda i: (0, i, 0))
    return pl.pallas_call(
        _s5_kernel,
        grid=(s // ts,),
        in_specs=[blk, _full_spec(bmat.shape), _full_spec(cmat.shape), _full_spec(are.shape),
                  _full_spec(aim.shape), _full_spec(d.shape), _full_spec(wglu.shape), _full_spec(bglu.shape)],
        out_specs=blk,
        out_shape=jax.ShapeDtypeStruct((b, s, S5_CHANNELS), BF16),
        scratch_shapes=[pltpu.VMEM((ts * b, 2 * S5_WIDTH), F32), pltpu.VMEM((b, 2 * S5_WIDTH), F32)],
        compiler_params=_cparams("arbitrary"),
        name="s5_scan",
    )(u, bmat, cmat, are, aim, d, wglu, bglu)


def _rotate_half_cols(w):
    half = w.shape[-1] // 2
    return jnp.concatenate([-w[..., half:], w[..., :half]], axis=-1)


def _layer0_params(w_in, q_norm, w_uq, kv_norm, w_ukv, seq):
    d = w_in.shape[0]
    w_rope = w_in[:, 2048:2080]
    pad_l = jnp.zeros((d, MLA_NOPE), F32)
    pad_r = jnp.zeros((d, LANES - MLA_NOPE - MLA_ROPE), F32)
    w1 = jnp.concatenate([w_in[:, :2048], pad_l, w_rope, pad_r, pad_l, _rotate_half_cols(w_rope), pad_r], axis=1)

    wq3 = w_uq.reshape(MLA_RANK, N_HEADS, MLA_NOPE + MLA_ROPE)
    nope, rope = wq3[..., :MLA_NOPE], wq3[..., MLA_NOPE:]
    z_nope = jnp.zeros_like(nope)
    z_pad = jnp.zeros((MLA_RANK, N_HEADS, LANES - MLA_NOPE - MLA_ROPE), F32)
    wq = jnp.concatenate([nope, rope, z_pad], axis=-1).reshape(MLA_RANK, N_HEADS * LANES)
    wqr = jnp.concatenate([z_nope, _rotate_half_cols(rope), z_pad], axis=-1).reshape(MLA_RANK, N_HEADS * LANES)

    wkv3 = w_ukv.reshape(MLA_RANK, N_HEADS, 2 * HEAD_DIM)
    wkn = jnp.concatenate([wkv3[..., :MLA_NOPE], jnp.zeros((MLA_RANK, N_HEADS, LANES - MLA_NOPE), F32)],
                          axis=-1).reshape(MLA_RANK, N_HEADS * LANES)
    wv = wkv3[..., MLA_NOPE:].reshape(MLA_RANK, HEADS_W)

    half = MLA_ROPE // 2
    freqs = ROPE_BASE ** (-jnp.arange(half, dtype=F32) / half)
    ang = jnp.arange(seq, dtype=F32)[:, None] * freqs
    cos = jnp.concatenate([jnp.cos(ang)] * 2, axis=1)
    sin = jnp.concatenate([jnp.sin(ang)] * 2, axis=1)
    ones = jnp.ones((seq, MLA_NOPE), F32)
    zl = jnp.zeros((seq, MLA_NOPE), F32)
    zr = jnp.zeros((seq, LANES - MLA_NOPE - MLA_ROPE), F32)
    scale = (MLA_NOPE + MLA_ROPE) ** -0.5 * LOG2_E
    cosq = jnp.concatenate([ones, cos, zr], axis=1) * scale
    sinq = jnp.concatenate([zl, sin, zr], axis=1) * scale
    cosk = jnp.concatenate([zl, cos, zr], axis=1)
    sink = jnp.concatenate([zl, sin, zr], axis=1)
    return (w1.astype(BF16), q_norm.reshape(1, -1), kv_norm.reshape(1, -1), wq.astype(BF16), wqr.astype(BF16),
            wkn.astype(BF16), wv.astype(BF16), cosq, sinq, cosk, sink)


def _s5_params(lam_re, lam_im, log_dt, b_re, b_im, c_re, c_im):
    dt = jnp.exp(log_dt)[:, None]
    mag = jnp.exp(lam_re * dt)
    ab_re, ab_im = mag * jnp.cos(lam_im * dt), mag * jnp.sin(lam_im * dt)
    den = lam_re * lam_re + lam_im * lam_im
    nr, ni = ab_re - 1.0, ab_im
    f_re, f_im = (nr * lam_re + ni * lam_im) / den, (ni * lam_re - nr * lam_im) / den
    bb_re = f_re[..., None] * b_re - f_im[..., None] * b_im
    bb_im = f_re[..., None] * b_im + f_im[..., None] * b_re

    gpc = LANES // S5_GROUP
    n_chunks = S5_GROUPS // gpc
    eye = jnp.eye(gpc, dtype=F32)

    def in_blocks(bb):
        t = bb.reshape(n_chunks, gpc, S5_STATE, S5_GROUP)
        return jnp.einsum('cgph,gk->cghkp', t, eye).reshape(n_chunks, gpc * S5_GROUP, gpc * S5_STATE)

    def out_blocks(cc):
        t = cc.reshape(n_chunks, gpc, S5_GROUP, S5_STATE)
        return jnp.einsum('cghp,gk->cgpkh', t, eye).reshape(n_chunks, gpc * S5_STATE, gpc * S5_GROUP)

    bmat = jnp.concatenate([in_blocks(bb_re), in_blocks(bb_im)], axis=2)
    cmat = jnp.concatenate([out_blocks(c_re), -out_blocks(c_im)], axis=1)
    a_re = ab_re.reshape(1, S5_WIDTH)
    a_im = ab_im.reshape(1, S5_WIDTH)
    return bmat.astype(BF16), cmat.astype(BF16), a_re, a_im


def kernel(x, ab_w_in, ab_q_norm, ab_w_uq, ab_kv_norm, ab_w_ukv, ab_w_out, cd_w_in, s5_lambda_re, s5_lambda_im,
           s5_log_dt, s5_b_re, s5_b_im, s5_c_re, s5_c_im, s5_d, s5_w_glu, s5_b_glu, cd_w_out, ln1_g, ln1_b, ln2_g,
           ln2_b, ffn_w_gate, ffn_w_up, ffn_w_down):
    b, s, d = x.shape
    t = b * s
    x2 = x.reshape(t, d)
    vec = lambda a: a.reshape(1, -1)

    p0 = _layer0_params(ab_w_in[0], ab_q_norm[0], ab_w_uq[0], ab_kv_norm[0], ab_w_ukv[0], s)
    qsb, ksb, vsb, qf, kf, vm = _proj_ab(x2, *p0, seq=s)
    sh = lambda a: a.reshape(b, s, a.shape[-1])
    o_sb, o_mla = _layer0_attention(sh(qsb), sh(ksb), sh(vsb), sh(qf), sh(kf), sh(vm))
    o_sb, o_mla = o_sb.reshape(t, HEADS_W), o_mla.reshape(t, HEADS_W)
    wg, wu, wd = ffn_w_gate.astype(BF16), ffn_w_up.astype(BF16), ffn_w_down.astype(BF16)
    x2 = _layer_tail(o_sb, o_mla, x2, ab_w_out.astype(BF16), ln1_g, ln1_b, wg, wu, wd, ln2_g, ln2_b, layer=0)

    u, q, k, v = _proj_cd(x2, cd_w_in[0].astype(BF16))
    bmat, cmat, a_re, a_im = _s5_params(s5_lambda_re[0], s5_lambda_im[0], s5_log_dt[0], s5_b_re[0], s5_b_im[0],
                                        s5_c_re[0], s5_c_im[0])
    o_s5 = _s5(sh(u), bmat, cmat, a_re, a_im, vec(s5_d[0]), s5_w_glu[0].astype(BF16), vec(s5_b_glu[0]))
    o_moba = _moba_attention(sh(q), sh(k), sh(v))
    x2 = _layer_tail(o_s5.reshape(t, S5_CHANNELS), o_moba.reshape(t, HEADS_W), x2, cd_w_out.astype(BF16),
                     ln1_g, ln1_b, wg, wu, wd, ln2_g, ln2_b, layer=1)
    return x2.reshape(b, s, d)
```

```python
import functools
import math

import jax
import jax.numpy as jnp
from jax import lax
from jax.experimental import pallas as pl
from jax.experimental.pallas import tpu as pltpu

F32 = jnp.float32
BF16 = jnp.bfloat16

D_MODEL = 1024
HEAD_DIM = 64
N_HEADS = 8
HEADS_W = N_HEADS * HEAD_DIM
MLA_RANK = 256
MLA_NOPE = 64
MLA_ROPE = 32
ROPE_BASE = 10000.0
S5_CHANNELS = 512
S5_GROUP = 16
S5_GROUPS = 32
S5_STATE = 64
S5_WIDTH = S5_GROUPS * S5_STATE
MOBA_BLOCK = 256
MOBA_TOPK = 3
D_FF = 2816
DEPTH = 2
DN_ALPHA = (2 * DEPTH) ** 0.25
LN_EPS = 1e-5
RMS_EPS = 1e-6

LANES = 128
SUBLANES = 8
BF16_ROWS = 16
NEG_BIG = -1e30
LOG2_E = math.log2(math.e)
VMEM_LIMIT = 56 * 1024 * 1024

ROW_TILE = 1024
TAIL_ROW_TILE = 1024
TAIL_STRIPS = 4
SB_GROUP = 256
SB_Q_TILE = 256
SB_EXIT = -176.0
Q_TILE = 512


def _cparams(*sem):
    return pltpu.CompilerParams(dimension_semantics=sem, vmem_limit_bytes=VMEM_LIMIT)


def _full_spec(shape):
    nd = len(shape)
    return pl.BlockSpec(shape, lambda *_: (0,) * nd)


def _dot(a, b):
    return jnp.dot(a, b, preferred_element_type=F32)


def _dot_nt(a, b):
    return lax.dot_general(a, b, (((1,), (1,)), ((), ())), preferred_element_type=F32)


def _layer_norm_rows(r, g, b):
    mu = jnp.mean(r, axis=-1, keepdims=True)
    d = r - mu
    var = jnp.mean(d * d, axis=-1, keepdims=True)
    return d * lax.rsqrt(var + LN_EPS) * g + b


def _proj_ab_kernel(x_ref, w1_ref, gq_ref, gkv_ref, wq_ref, wqr_ref, wkn_ref, wv_ref,
                    cosq_ref, sinq_ref, cosk_ref, sink_ref,
                    qsb_ref, ksb_ref, vsb_ref, qf_ref, kf_ref, vm_ref):
    xb = x_ref[...].astype(BF16)
    p = _dot(xb, w1_ref[...])
    qsb_ref[...] = (p[:, 0:512] * (HEAD_DIM ** -0.5 * LOG2_E)).astype(BF16)
    ksb_ref[...] = p[:, 512:1024].astype(BF16)
    vsb_ref[...] = p[:, 1024:1536].astype(BF16)
    cq = p[:, 1536:1792]
    ckv = p[:, 1792:2048]
    rope = p[:, 2048:2176]
    rope_rot = p[:, 2176:2304]

    cqn = cq * lax.rsqrt(jnp.mean(cq * cq, axis=-1, keepdims=True) + RMS_EPS) * gq_ref[...]
    ckvn = ckv * lax.rsqrt(jnp.mean(ckv * ckv, axis=-1, keepdims=True) + RMS_EPS) * gkv_ref[...]
    cqb = cqn.astype(BF16)
    ckvb = ckvn.astype(BF16)

    cosq = jnp.concatenate([cosq_ref[...]] * N_HEADS, axis=1)
    sinq = jnp.concatenate([sinq_ref[...]] * N_HEADS, axis=1)
    qf = _dot(cqb, wq_ref[...]) * cosq + _dot(cqb, wqr_ref[...]) * sinq
    qf_ref[...] = qf.astype(BF16)

    kpe = rope * cosk_ref[...] + rope_rot * sink_ref[...]
    kf = _dot(ckvb, wkn_ref[...]) + jnp.concatenate([kpe] * N_HEADS, axis=1)
    kf_ref[...] = kf.astype(BF16)
    vm_ref[...] = _dot(ckvb, wv_ref[...]).astype(BF16)


def _proj_ab(x2, w1, gq, gkv, wq, wqr, wkn, wv, cosq, sinq, cosk, sink, seq):
    t = x2.shape[0]
    tm = min(ROW_TILE, seq)
    n_seq_tiles = seq // tm
    row = lambda c: pl.BlockSpec((tm, c), lambda i: (i, 0))
    tab = pl.BlockSpec((tm, LANES), lambda i: (i % n_seq_tiles, 0))
    out_w = (HEADS_W, HEADS_W, HEADS_W, N_HEADS * LANES, N_HEADS * LANES, HEADS_W)
    return pl.pallas_call(
        _proj_ab_kernel,
        grid=(t // tm,),
        in_specs=[row(D_MODEL), _full_spec(w1.shape), _full_spec(gq.shape), _full_spec(gkv.shape),
                  _full_spec(wq.shape), _full_spec(wqr.shape), _full_spec(wkn.shape), _full_spec(wv.shape),
                  tab, tab, tab, tab],
        out_specs=[row(c) for c in out_w],
        out_shape=[jax.ShapeDtypeStruct((t, c), BF16) for c in out_w],
        compiler_params=_cparams("parallel"),
        name="proj_ab",
    )(x2, w1, gq, gkv, wq, wqr, wkn, wv, cosq, sinq, cosk, sink)


def _sb_tile(qh, k2, v2, acc, run, diag, suffix):
    tq = qh.shape[0]
    grp = suffix.shape[0]
    z = _dot_nt(qh, k2)
    if diag:
        r = lax.broadcasted_iota(jnp.int32, (tq, tq), 0)
        c = lax.broadcasted_iota(jnp.int32, (tq, tq), 1)
        own = jnp.where(c < r, z[:, -tq:], NEG_BIG)
        z = own if z.shape[1] == tq else jnp.concatenate([z[:, :-tq], own], axis=1)
    soft = jnp.log2(1.0 + jnp.exp2(-jnp.abs(z)))
    log_beta = jnp.minimum(z, 0.0) - soft
    log_keep = log_beta - z
    hi = log_keep.astype(BF16)
    laters = []
    for g in reversed(range(z.shape[1] // grp)):
        cols = slice(g * grp, (g + 1) * grp)
        laters.insert(0, _dot(hi[:, cols], suffix) + run)
        run = run + jnp.sum(log_keep[:, cols], axis=-1, keepdims=True)
    w = jnp.exp2(log_beta + jnp.concatenate(laters, axis=1))
    return acc + _dot(w.astype(BF16), v2), run


def _pair_masks():
    lane = lax.broadcasted_iota(jnp.int32, (1, LANES), 1)
    return lane < HEAD_DIM


def _suffix_matrix(grp):
    rg = lax.broadcasted_iota(jnp.int32, (grp, grp), 0)
    cg = lax.broadcasted_iota(jnp.int32, (grp, grp), 1)
    return jnp.where(rg > cg, 1.0, 0.0).astype(BF16)


def _split_heads(q2, first):
    zero = jnp.zeros_like(q2)
    return jnp.where(first, q2, zero), jnp.where(first, zero, q2)


def _sb_band_tile(qi, q_ref, k_ref, v_ref, o_ref, acc_ref, run_ref, top_ref, first, suffix, tq):
    qs = qi * tq
    q_heads = _split_heads(q_ref[qs:qs + tq, :], first)
    b0 = max(0, qs - tq)
    w = qs + tq - b0
    accs, runs = [], []
    for h in range(2):
        acc, run = _sb_tile(q_heads[h], k_ref[b0:b0 + w, :], v_ref[b0:b0 + w, :],
                            jnp.zeros((tq, LANES), F32), jnp.zeros((tq, 1), F32), True, suffix)
        accs.append(acc)
        runs.append(run)
        if b0 > 0:
            acc_ref[h, qs:qs + tq, :] = acc
            run_ref[h, qs:qs + tq, :] = run
    if b0 > 0:
        top_ref[qi] = jnp.maximum(jnp.max(runs[0]), jnp.max(runs[1]))
    o_ref[qs:qs + tq, :] = jnp.where(first, accs[0], accs[1]).astype(o_ref.dtype)


def _sb_finish(q_ref, k_ref, v_ref, o_ref, acc_ref, run_ref, top_ref, first, suffix, tq):
    def finish(qi, _):
        @pl.when(top_ref[qi] > SB_EXIT)
        def _():
            rows = pl.ds(pl.multiple_of(qi * tq, tq), tq)
            q_heads = _split_heads(q_ref[rows, :], first)

            def unfinished(st):
                j, _, ra, _, rb = st
                return (j >= 0) & (jnp.maximum(jnp.max(ra), jnp.max(rb)) > SB_EXIT)

            def earlier(st):
                j, acc_a, ra, acc_b, rb = st
                keys = pl.ds(pl.multiple_of(j * tq, tq), tq)
                k2 = k_ref[keys, :]
                v2 = v_ref[keys, :]
                acc_a, ra = _sb_tile(q_heads[0], k2, v2, acc_a, ra, False, suffix)
                acc_b, rb = _sb_tile(q_heads[1], k2, v2, acc_b, rb, False, suffix)
                return j - 1, acc_a, ra, acc_b, rb

            st = lax.while_loop(unfinished, earlier, (qi - 2, acc_ref[0, rows, :], run_ref[0, rows, :],
                                                      acc_ref[1, rows, :], run_ref[1, rows, :]))
            o_ref[rows, :] = jnp.where(first, st[1], st[3]).astype(o_ref.dtype)

        return 0

    lax.fori_loop(2, q_ref.shape[0] // tq, finish, 0)


def _mla_tile(qi, q_ref, k_ref, v_ref, o_ref, first, tq):
    rows = slice(qi * tq, (qi + 1) * tq)
    end = (qi + 1) * tq
    v_ones = _with_ones(v_ref[:end, :], first)
    out = [_causal_softmax_pv(_dot_nt(q_ref[rows, h * LANES:(h + 1) * LANES], k_ref[:end, h * LANES:(h + 1) * LANES]),
                              v_ones[h], first) for h in range(2)]
    o_ref[rows, :] = jnp.where(first, out[0], out[1]).astype(o_ref.dtype)


def _layer0_attn_kernel(qs_ref, ks_ref, vs_ref, qm_ref, km_ref, vm_ref, osb_ref, omla_ref, acc_ref, run_ref, top_ref, *,
                        tq_sb, tq_mla):
    seq = qs_ref.shape[0]
    first = _pair_masks()
    suffix = _suffix_matrix(min(SB_GROUP, tq_sb))
    sb_tiles = list(reversed(range(seq // tq_sb)))
    mla_tiles = list(reversed(range(seq // tq_mla)))
    per = -(-len(sb_tiles) // len(mla_tiles))
    while sb_tiles or mla_tiles:
        if mla_tiles:
            _mla_tile(mla_tiles.pop(0), qm_ref, km_ref, vm_ref, omla_ref, first, tq_mla)
        for _ in range(per):
            if sb_tiles:
                _sb_band_tile(sb_tiles.pop(0), qs_ref, ks_ref, vs_ref, osb_ref, acc_ref, run_ref, top_ref, first,
                              suffix, tq_sb)
    _sb_finish(qs_ref, ks_ref, vs_ref, osb_ref, acc_ref, run_ref, top_ref, first, suffix, tq_sb)


def _layer0_attention(qsb, ksb, vsb, qf, kf, vm):
    b, s, _ = qsb.shape
    tq_sb, tq_mla = min(SB_Q_TILE, s), min(Q_TILE, s)
    assert s % tq_sb == 0 and s % tq_mla == 0
    pair = pl.BlockSpec((None, s, LANES), lambda bi, h: (bi, 0, h))
    wide = pl.BlockSpec((None, s, 2 * LANES), lambda bi, h: (bi, 0, h))
    out = jax.ShapeDtypeStruct((b, s, HEADS_W), BF16)
    return pl.pallas_call(
        functools.partial(_layer0_attn_kernel, tq_sb=tq_sb, tq_mla=tq_mla),
        grid=(b, N_HEADS // 2),
        in_specs=[pair, pair, pair, wide, wide, pair],
        out_specs=[pair, pair],
        out_shape=[out, out],
        scratch_shapes=[pltpu.VMEM((2, s, LANES), F32), pltpu.VMEM((2, s, 1), F32),
                        pltpu.SMEM((s // tq_sb,), F32)],
        compiler_params=_cparams("parallel", "parallel"),
        name="sb_mla_attn",
    )(qsb, ksb, vsb, qf, kf, vm)


def _causal_softmax_pv(s, v_ones, first):
    tq = s.shape[0]
    r = lax.broadcasted_iota(jnp.int32, (tq, tq), 0)
    c = lax.broadcasted_iota(jnp.int32, (tq, tq), 1)
    own = jnp.where(c <= r, s[:, -tq:], NEG_BIG)
    s = own if s.shape[1] == tq else jnp.concatenate([s[:, :-tq], own], axis=1)
    p = jnp.exp2(s - jnp.max(s, axis=-1, keepdims=True))
    acc = _dot(p.astype(BF16), v_ones)
    return acc / pltpu.roll(acc, HEAD_DIM, 1)


def _with_ones(v2, first):
    one = jnp.ones_like(v2)
    return jnp.where(first, v2, one), jnp.where(first, one, v2)


def _moba_block_means(k_ref):
    nblk = k_ref.shape[0] // MOBA_BLOCK
    means = [jnp.mean(k_ref[n * MOBA_BLOCK:(n + 1) * MOBA_BLOCK, :].astype(F32), axis=0, keepdims=True)
             for n in range(nblk)]
    km = jnp.concatenate(means + [jnp.zeros((BF16_ROWS - nblk, LANES), F32)], axis=0)
    k1 = km.astype(BF16)
    rem = km - k1.astype(F32)
    k2 = rem.astype(BF16)
    return k1, k2, (rem - k2.astype(F32)).astype(BF16)


def _moba_biased_queries(qi, q_ref, kmeans, first, tq):
    nblk = q_ref.shape[0] // MOBA_BLOCK
    k1, k2, k3 = kmeans
    q2 = q_ref[qi * tq:(qi + 1) * tq, :]
    rowid = lax.broadcasted_iota(jnp.int32, (SUBLANES, tq), 0)
    qblk = qi * (tq // MOBA_BLOCK) + lax.broadcasted_iota(jnp.int32, (SUBLANES, tq), 1) // MOBA_BLOCK
    valid = rowid < qblk
    q_aug = []
    for qh in _split_heads(q2, first):
        gate_t = _dot_nt(k1, qh) + _dot_nt(k2, qh) + _dot_nt(k3, qh)
        g = jnp.where(valid, gate_t[0:SUBLANES, :], -jnp.inf)
        beaten = jnp.zeros((SUBLANES, tq), jnp.int32)
        for m in range(nblk):
            gm = g[m:m + 1, :]
            wins = (gm > g) | ((gm == g) & (rowid > m))
            beaten = beaten + jnp.where(wins, 1, 0)
        usable = (valid & (beaten < MOBA_TOPK)) | (rowid == qblk)
        bias_t = jnp.where(usable, 0.0, NEG_BIG)
        bias_t = jnp.concatenate([bias_t, jnp.zeros((LANES - SUBLANES, tq), F32)], axis=0)
        q_aug.append(jnp.concatenate([qh, bias_t.T.astype(BF16)], axis=1))
    return q_aug


def _moba_tile(qi, q_aug, k_ref, v_ref, o_ref, first, tq):
    rows = slice(qi * tq, (qi + 1) * tq)
    end = (qi + 1) * tq
    blk = lax.broadcasted_iota(jnp.int32, (end, LANES), 0) // MOBA_BLOCK
    onehot = jnp.where(blk == lax.broadcasted_iota(jnp.int32, (end, LANES), 1), 1.0, 0.0).astype(BF16)
    k_aug = jnp.concatenate([k_ref[:end, :], onehot], axis=1)
    v_ones = _with_ones(v_ref[:end, :], first)
    out = [_causal_softmax_pv(_dot_nt(q_aug[h], k_aug), v_ones[h], first) for h in range(2)]
    o_ref[rows, :] = jnp.where(first, out[0], out[1]).astype(o_ref.dtype)


def _s5_chain(j, ub, bmat_ref, cmat_ref, are_ref, aim_ref, x_ref, state_ref, nb, ts):
    cw = S5_WIDTH // (S5_CHANNELS // LANES)
    re_cols = slice(2 * j * cw, (2 * j + 1) * cw)
    im_cols = slice((2 * j + 1) * cw, (2 * j + 2) * cw)
    x_ref[:, 2 * j * cw:(2 * j + 2) * cw] = _dot(ub[:, j * LANES:(j + 1) * LANES], bmat_ref[j])
    ar = jnp.broadcast_to(are_ref[:, j * cw:(j + 1) * cw], (nb, cw))
    ai = jnp.broadcast_to(aim_ref[:, j * cw:(j + 1) * cw], (nb, cw))
    xr = state_ref[:, re_cols]
    xi = state_ref[:, im_cols]
    for t in range(ts):
        step = slice(t * nb, (t + 1) * nb)
        nr = ar * xr - ai * xi + x_ref[step, re_cols]
        ni = ar * xi + ai * xr + x_ref[step, im_cols]
        x_ref[step, re_cols] = nr
        x_ref[step, im_cols] = ni
        xr, xi = nr, ni
    state_ref[:, re_cols] = xr
    state_ref[:, im_cols] = xi
    return _dot(x_ref[:, 2 * j * cw:(2 * j + 2) * cw].astype(BF16), cmat_ref[j])


def _layer1_mixer_kernel(u_ref, bmat_ref, cmat_ref, are_ref, aim_ref, d_ref, wglu_ref, bglu_ref, q_ref, k_ref, v_ref,
                         os5_ref, omoba_ref, x_ref, state_ref, *, tq):
    nb, ts, _ = u_ref.shape
    rows = ts * nb
    seq = q_ref.shape[0]
    first = _pair_masks()

    @pl.when(pl.program_id(0) == 0)
    def _():
        state_ref[...] = jnp.zeros_like(state_ref)

    uf = jnp.swapaxes(u_ref[...].astype(F32), 0, 1).reshape(rows, S5_CHANNELS)
    ub = uf.astype(BF16)
    kmeans = _moba_block_means(k_ref)

    tiles = list(reversed(range(seq // tq)))
    chains = list(range(S5_CHANNELS // LANES))
    ys = []
    while tiles or chains:
        if tiles:
            qi = tiles.pop(0)
            _moba_tile(qi, _moba_biased_queries(qi, q_ref, kmeans, first, tq), k_ref, v_ref, omoba_ref, first, tq)
        if chains:
            ys.append(_s5_chain(chains.pop(0), ub, bmat_ref, cmat_ref, are_ref, aim_ref, x_ref, state_ref, nb, ts))

    y = jnp.concatenate(ys, axis=1) + d_ref[...] * uf
    z = 0.5 * y * (1.0 + jnp.tanh(math.sqrt(2.0 / math.pi) * (y + 0.044715 * (y * y * y))))
    gate = _dot(z.astype(BF16), wglu_ref[...]) + bglu_ref[...]
    out = z * (1.0 / (1.0 + jnp.exp(-gate)))
    os5_ref[...] = jnp.swapaxes(out.reshape(ts, nb, S5_CHANNELS), 0, 1).astype(os5_ref.dtype)


def _layer1_mixer(u, bmat, cmat, are, aim, d, wglu, bglu, q, k, v):
    b, s, _ = u.shape
    n_pairs = N_HEADS // 2
    steps = b * n_pairs
    assert s % steps == 0
    ts = s // steps
    tq = min(Q_TILE, s)
    assert s % tq == 0 and tq % MOBA_BLOCK == 0 and s // MOBA_BLOCK <= SUBLANES
    chunk = pl.BlockSpec((b, ts, S5_CHANNELS), lambda i: (0, i, 0))
    pair = pl.BlockSpec((None, s, LANES), lambda i: (i // n_pairs, 0, i % n_pairs))
    return pl.pallas_call(
        functools.partial(_layer1_mixer_kernel, tq=tq),
        grid=(steps,),
        in_specs=[chunk, _full_spec(bmat.shape), _full_spec(cmat.shape), _full_spec(are.shape),
                  _full_spec(aim.shape), _full_spec(d.shape), _full_spec(wglu.shape), _full_spec(bglu.shape),
                  pair, pair, pair],
        out_specs=[chunk, pair],
        out_shape=[jax.ShapeDtypeStruct((b, s, S5_CHANNELS), BF16), jax.ShapeDtypeStruct((b, s, HEADS_W), BF16)],
        scratch_shapes=[pltpu.VMEM((ts * b, 2 * S5_WIDTH), F32), pltpu.VMEM((b, 2 * S5_WIDTH), F32)],
        compiler_params=_cparams("arbitrary"),
        name="s5_moba",
    )(u, bmat, cmat, are, aim, d, wglu, bglu, q, k, v)


def _tail_kernel(a_ref, b_ref, x_ref, wo_ref, g1_ref, beta1_ref, wg_ref, wu_ref, wd_ref, g2_ref, beta2_ref, o_ref, *,
                 layer):
    tm = x_ref.shape[0]
    strip = tm // TAIL_STRIPS
    rows = [slice(r0, r0 + strip) for r0 in range(0, tm, strip)]
    wa = a_ref.shape[1]
    ln = slice(layer, layer + 1)

    def mix(r):
        return _dot(a_ref[r, :], wo_ref[:wa, :]) + _dot(b_ref[r, :], wo_ref[wa:, :])

    def norm1(r, m):
        return _layer_norm_rows(DN_ALPHA * x_ref[r, :] + m, g1_ref[ln, :], beta1_ref[ln, :])

    def ffn(x1):
        xb = x1.astype(BF16)
        gate = _dot(xb, wg_ref[...])
        up = _dot(xb, wu_ref[...])
        h = (gate * (1.0 / (1.0 + jnp.exp(-gate))) * up).astype(BF16)
        return _dot(h, wd_ref[...])

    def norm2(r, x1, y):
        o_ref[r, :] = _layer_norm_rows(DN_ALPHA * x1 + y, g2_ref[ln, :], beta2_ref[ln, :])

    n = len(rows)
    m = [None] * n
    x1 = [None] * n
    y = [None] * n
    m[0] = mix(rows[0])
    for k in range(n):
        if k + 1 < n:
            m[k + 1] = mix(rows[k + 1])
        x1[k] = norm1(rows[k], m[k])
        if k > 0:
            norm2(rows[k - 1], x1[k - 1], y[k - 1])
        y[k] = ffn(x1[k])
    norm2(rows[n - 1], x1[n - 1], y[n - 1])


def _layer_tail(a, b, x2, w_out, g1, beta1, wg, wu, wd, g2, beta2, layer):
    t = x2.shape[0]
    tm = min(TAIL_ROW_TILE, t)
    row = lambda c: pl.BlockSpec((tm, c), lambda i: (i, 0))
    whole = lambda arr: pl.BlockSpec(arr.shape, lambda i: (0, 0), pipeline_mode=pl.Buffered(1))
    of_layer = lambda arr, l: pl.BlockSpec((None,) + arr.shape[1:], lambda i: (l, 0, 0),
                                           pipeline_mode=pl.Buffered(1))
    return pl.pallas_call(
        functools.partial(_tail_kernel, layer=layer),
        grid=(t // tm,),
        in_specs=[row(a.shape[1]), row(b.shape[1]), row(D_MODEL), of_layer(w_out, 0), whole(g1), whole(beta1),
                  of_layer(wg, layer), of_layer(wu, layer), of_layer(wd, layer), whole(g2), whole(beta2)],
        out_specs=row(D_MODEL),
        out_shape=jax.ShapeDtypeStruct((t, D_MODEL), F32),
        compiler_params=_cparams("parallel"),
        name="layer_tail",
    )(a, b, x2, w_out, g1, beta1, wg, wu, wd, g2, beta2)


def _proj_cd_kernel(x_ref, w_ref, u_ref, q_ref, k_ref, v_ref):
    p = _dot(x_ref[...].astype(BF16), w_ref[...])
    u_ref[...] = p[:, 0:512].astype(BF16)
    q_ref[...] = (p[:, 512:1024] * (HEAD_DIM ** -0.5 * LOG2_E)).astype(BF16)
    k_ref[...] = p[:, 1024:1536].astype(BF16)
    v_ref[...] = p[:, 1536:2048].astype(BF16)


def _proj_cd(x2, w):
    t = x2.shape[0]
    tm = min(ROW_TILE, t)
    row = lambda c: pl.BlockSpec((tm, c), lambda i: (i, 0))
    return pl.pallas_call(
        _proj_cd_kernel,
        grid=(t // tm,),
        in_specs=[row(D_MODEL), _full_spec(w.shape)],
        out_specs=[row(HEADS_W)] * 4,
        out_shape=[jax.ShapeDtypeStruct((t, HEADS_W), BF16)] * 4,
        compiler_params=_cparams("parallel"),
        name="proj_cd",
    )(x2, w)


def _rotate_half_cols(w):
    half = w.shape[-1] // 2
    return jnp.concatenate([-w[..., half:], w[..., :half]], axis=-1)


def _layer0_params(w_in, q_norm, w_uq, kv_norm, w_ukv, seq):
    d = w_in.shape[0]
    w_rope = w_in[:, 2048:2080]
    pad_l = jnp.zeros((d, MLA_NOPE), F32)
    pad_r = jnp.zeros((d, LANES - MLA_NOPE - MLA_ROPE), F32)
    w1 = jnp.concatenate([w_in[:, :2048], pad_l, w_rope, pad_r, pad_l, _rotate_half_cols(w_rope), pad_r], axis=1)

    wq3 = w_uq.reshape(MLA_RANK, N_HEADS, MLA_NOPE + MLA_ROPE)
    nope, rope = wq3[..., :MLA_NOPE], wq3[..., MLA_NOPE:]
    z_nope = jnp.zeros_like(nope)
    z_pad = jnp.zeros((MLA_RANK, N_HEADS, LANES - MLA_NOPE - MLA_ROPE), F32)
    wq = jnp.concatenate([nope, rope, z_pad], axis=-1).reshape(MLA_RANK, N_HEADS * LANES)
    wqr = jnp.concatenate([z_nope, _rotate_half_cols(rope), z_pad], axis=-1).reshape(MLA_RANK, N_HEADS * LANES)

    wkv3 = w_ukv.reshape(MLA_RANK, N_HEADS, 2 * HEAD_DIM)
    wkn = jnp.concatenate([wkv3[..., :MLA_NOPE], jnp.zeros((MLA_RANK, N_HEADS, LANES - MLA_NOPE), F32)],
                          axis=-1).reshape(MLA_RANK, N_HEADS * LANES)
    wv = wkv3[..., MLA_NOPE:].reshape(MLA_RANK, HEADS_W)

    half = MLA_ROPE // 2
    freqs = ROPE_BASE ** (-jnp.arange(half, dtype=F32) / half)
    ang = jnp.arange(seq, dtype=F32)[:, None] * freqs
    cos = jnp.concatenate([jnp.cos(ang)] * 2, axis=1)
    sin = jnp.concatenate([jnp.sin(ang)] * 2, axis=1)
    ones = jnp.ones((seq, MLA_NOPE), F32)
    zl = jnp.zeros((seq, MLA_NOPE), F32)
    zr = jnp.zeros((seq, LANES - MLA_NOPE - MLA_ROPE), F32)
    scale = (MLA_NOPE + MLA_ROPE) ** -0.5 * LOG2_E
    cosq = jnp.concatenate([ones, cos, zr], axis=1) * scale
    sinq = jnp.concatenate([zl, sin, zr], axis=1) * scale
    cosk = jnp.concatenate([zl, cos, zr], axis=1)
    sink = jnp.concatenate([zl, sin, zr], axis=1)
    return (w1.astype(BF16), q_norm.reshape(1, -1), kv_norm.reshape(1, -1), wq.astype(BF16), wqr.astype(BF16),
            wkn.astype(BF16), wv.astype(BF16), cosq, sinq, cosk, sink)


def _s5_params(lam_re, lam_im, log_dt, b_re, b_im, c_re, c_im):
    dt = jnp.exp(log_dt)[:, None]
    mag = jnp.exp(lam_re * dt)
    ab_re, ab_im = mag * jnp.cos(lam_im * dt), mag * jnp.sin(lam_im * dt)
    den = lam_re * lam_re + lam_im * lam_im
    nr, ni = ab_re - 1.0, ab_im
    f_re, f_im = (nr * lam_re + ni * lam_im) / den, (ni * lam_re - nr * lam_im) / den
    bb_re = f_re[..., None] * b_re - f_im[..., None] * b_im
    bb_im = f_re[..., None] * b_im + f_im[..., None] * b_re

    gpc = LANES // S5_GROUP
    n_chunks = S5_GROUPS // gpc
    eye = jnp.eye(gpc, dtype=F32)

    def in_blocks(bb):
        t = bb.reshape(n_chunks, gpc, S5_STATE, S5_GROUP)
        return jnp.einsum('cgph,gk->cghkp', t, eye).reshape(n_chunks, gpc * S5_GROUP, gpc * S5_STATE)

    def out_blocks(cc):
        t = cc.reshape(n_chunks, gpc, S5_GROUP, S5_STATE)
        return jnp.einsum('cghp,gk->cgpkh', t, eye).reshape(n_chunks, gpc * S5_STATE, gpc * S5_GROUP)

    bmat = jnp.concatenate([in_blocks(bb_re), in_blocks(bb_im)], axis=2)
    cmat = jnp.concatenate([out_blocks(c_re), -out_blocks(c_im)], axis=1)
    a_re = ab_re.reshape(1, S5_WIDTH)
    a_im = ab_im.reshape(1, S5_WIDTH)
    return bmat.astype(BF16), cmat.astype(BF16), a_re, a_im


def kernel(x, ab_w_in, ab_q_norm, ab_w_uq, ab_kv_norm, ab_w_ukv, ab_w_out, cd_w_in, s5_lambda_re, s5_lambda_im,
           s5_log_dt, s5_b_re, s5_b_im, s5_c_re, s5_c_im, s5_d, s5_w_glu, s5_b_glu, cd_w_out, ln1_g, ln1_b, ln2_g,
           ln2_b, ffn_w_gate, ffn_w_up, ffn_w_down):
    b, s, d = x.shape
    t = b * s
    x2 = x.reshape(t, d)
    vec = lambda a: a.reshape(1, -1)

    p0 = _layer0_params(ab_w_in[0], ab_q_norm[0], ab_w_uq[0], ab_kv_norm[0], ab_w_ukv[0], s)
    qsb, ksb, vsb, qf, kf, vm = _proj_ab(x2, *p0, seq=s)
    sh = lambda a: a.reshape(b, s, a.shape[-1])
    o_sb, o_mla = _layer0_attention(sh(qsb), sh(ksb), sh(vsb), sh(qf), sh(kf), sh(vm))
    o_sb, o_mla = o_sb.reshape(t, HEADS_W), o_mla.reshape(t, HEADS_W)
    wg, wu, wd = ffn_w_gate.astype(BF16), ffn_w_up.astype(BF16), ffn_w_down.astype(BF16)
    x2 = _layer_tail(o_sb, o_mla, x2, ab_w_out.astype(BF16), ln1_g, ln1_b, wg, wu, wd, ln2_g, ln2_b, layer=0)

    u, q, k, v = _proj_cd(x2, cd_w_in[0].astype(BF16))
    bmat, cmat, a_re, a_im = _s5_params(s5_lambda_re[0], s5_lambda_im[0], s5_log_dt[0], s5_b_re[0], s5_b_im[0],
                                        s5_c_re[0], s5_c_im[0])
    o_s5, o_moba = _layer1_mixer(sh(u), bmat, cmat, a_re, a_im, vec(s5_d[0]), s5_w_glu[0].astype(BF16),
                                 vec(s5_b_glu[0]), sh(q), sh(k), sh(v))
    x2 = _layer_tail(o_s5.reshape(t, S5_CHANNELS), o_moba.reshape(t, HEADS_W), x2, cd_w_out.astype(BF16),
                     ln1_g, ln1_b, wg, wu, wd, ln2_g, ln2_b, layer=1)
    return x2.reshape(b, s, d)
```

```python
import functools
import math

import jax
import jax.numpy as jnp
from jax import lax
from jax.experimental import pallas as pl
from jax.experimental.pallas import tpu as pltpu

F32 = jnp.float32
BF16 = jnp.bfloat16

D_MODEL = 1024
HEAD_DIM = 64
N_HEADS = 8
HEADS_W = N_HEADS * HEAD_DIM
MLA_RANK = 256
MLA_NOPE = 64
MLA_ROPE = 32
AB_LATENT_END = 3 * HEADS_W + 2 * MLA_RANK
ROPE_BASE = 10000.0
S5_CHANNELS = 512
S5_GROUP = 16
S5_GROUPS = 32
S5_STATE = 64
S5_WIDTH = S5_GROUPS * S5_STATE
MOBA_BLOCK = 256
MOBA_TOPK = 3
D_FF = 2816
DEPTH = 2
DN_ALPHA = (2 * DEPTH) ** 0.25
LN_EPS = 1e-5
RMS_EPS = 1e-6

LANES = 128
SUBLANES = 8
BF16_ROWS = 16
NEG_BIG = -1e30
LOG2_E = math.log2(math.e)
VMEM_LIMIT = 56 * 1024 * 1024

ROW_TILE = 1024
TAIL_ROW_TILE = 1024
TAIL_STRIPS = 4
SB_GROUP = 256
SB_Q_TILE = 256
SB_EXIT = -176.0
Q_TILE = 512


def _cparams(*sem):
    return pltpu.CompilerParams(dimension_semantics=sem, vmem_limit_bytes=VMEM_LIMIT)


def _full_spec(shape):
    nd = len(shape)
    return pl.BlockSpec(shape, lambda *_: (0,) * nd)


def _dot(a, b):
    return jnp.dot(a, b, preferred_element_type=F32)


def _dot_nt(a, b):
    return lax.dot_general(a, b, (((1,), (1,)), ((), ())), preferred_element_type=F32)


def _layer_norm_rows(r, g, b):
    mu = jnp.mean(r, axis=-1, keepdims=True)
    d = r - mu
    var = jnp.mean(d * d, axis=-1, keepdims=True)
    return d * lax.rsqrt(var + LN_EPS) * g + b


def _proj_ab_kernel(x_ref, w1_ref, gq_ref, gkv_ref, wq_ref, wqr_ref, wkn_ref, wv_ref,
                    cosq_ref, sinq_ref, cosk_ref, sink_ref,
                    qsb_ref, ksb_ref, vsb_ref, qf_ref, kf_ref, vm_ref):
    xb = x_ref[...].astype(BF16)
    p = _dot(xb, w1_ref[...])
    c0 = 3 * HEADS_W
    qsb_ref[...] = (p[:, 0:HEADS_W] * (HEAD_DIM ** -0.5 * LOG2_E)).astype(BF16)
    ksb_ref[...] = p[:, HEADS_W:2 * HEADS_W].astype(BF16)
    vsb_ref[...] = p[:, 2 * HEADS_W:c0].astype(BF16)
    cq = p[:, c0:c0 + MLA_RANK]
    ckv = p[:, c0 + MLA_RANK:AB_LATENT_END]
    rope = p[:, AB_LATENT_END:AB_LATENT_END + LANES]
    rope_rot = p[:, AB_LATENT_END + LANES:AB_LATENT_END + 2 * LANES]

    cqn = cq * lax.rsqrt(jnp.mean(cq * cq, axis=-1, keepdims=True) + RMS_EPS) * gq_ref[...]
    ckvn = ckv * lax.rsqrt(jnp.mean(ckv * ckv, axis=-1, keepdims=True) + RMS_EPS) * gkv_ref[...]
    cqb = cqn.astype(BF16)
    ckvb = ckvn.astype(BF16)

    cosq = jnp.concatenate([cosq_ref[...]] * N_HEADS, axis=1)
    sinq = jnp.concatenate([sinq_ref[...]] * N_HEADS, axis=1)
    qf = _dot(cqb, wq_ref[...]) * cosq + _dot(cqb, wqr_ref[...]) * sinq
    qf_ref[...] = qf.astype(BF16)

    kpe = rope * cosk_ref[...] + rope_rot * sink_ref[...]
    kf = _dot(ckvb, wkn_ref[...]) + jnp.concatenate([kpe] * N_HEADS, axis=1)
    kf_ref[...] = kf.astype(BF16)
    vm_ref[...] = _dot(ckvb, wv_ref[...]).astype(BF16)


def _proj_ab(x2, w1, gq, gkv, wq, wqr, wkn, wv, cosq, sinq, cosk, sink, seq):
    t = x2.shape[0]
    tm = min(ROW_TILE, seq)
    n_seq_tiles = seq // tm
    row = lambda c: pl.BlockSpec((tm, c), lambda i: (i, 0))
    tab = pl.BlockSpec((tm, LANES), lambda i: (i % n_seq_tiles, 0))
    out_w = (HEADS_W, HEADS_W, HEADS_W, N_HEADS * LANES, N_HEADS * LANES, HEADS_W)
    return pl.pallas_call(
        _proj_ab_kernel,
        grid=(t // tm,),
        in_specs=[row(D_MODEL), _full_spec(w1.shape), _full_spec(gq.shape), _full_spec(gkv.shape),
                  _full_spec(wq.shape), _full_spec(wqr.shape), _full_spec(wkn.shape), _full_spec(wv.shape),
                  tab, tab, tab, tab],
        out_specs=[row(c) for c in out_w],
        out_shape=[jax.ShapeDtypeStruct((t, c), BF16) for c in out_w],
        compiler_params=_cparams("parallel"),
        name="proj_ab",
    )(x2, w1, gq, gkv, wq, wqr, wkn, wv, cosq, sinq, cosk, sink)


def _sb_tile(qh, k2, v2, acc, run, diag, suffix):
    tq = qh.shape[0]
    grp = suffix.shape[0]
    z = _dot_nt(qh, k2)
    if diag:
        r = lax.broadcasted_iota(jnp.int32, (tq, tq), 0)
        c = lax.broadcasted_iota(jnp.int32, (tq, tq), 1)
        own = jnp.where(c < r, z[:, -tq:], NEG_BIG)
        z = own if z.shape[1] == tq else jnp.concatenate([z[:, :-tq], own], axis=1)
    soft = jnp.log2(1.0 + jnp.exp2(-jnp.abs(z)))
    log_beta = jnp.minimum(z, 0.0) - soft
    log_keep = log_beta - z
    hi = log_keep.astype(BF16)
    laters = []
    for g in reversed(range(z.shape[1] // grp)):
        cols = slice(g * grp, (g + 1) * grp)
        laters.insert(0, _dot(hi[:, cols], suffix) + run)
        run = run + jnp.sum(log_keep[:, cols], axis=-1, keepdims=True)
    w = jnp.exp2(log_beta + jnp.concatenate(laters, axis=1))
    return acc + _dot(w.astype(BF16), v2), run


def _pair_masks():
    lane = lax.broadcasted_iota(jnp.int32, (1, LANES), 1)
    return lane < HEAD_DIM


def _suffix_matrix(grp):
    rg = lax.broadcasted_iota(jnp.int32, (grp, grp), 0)
    cg = lax.broadcasted_iota(jnp.int32, (grp, grp), 1)
    return jnp.where(rg > cg, 1.0, 0.0).astype(BF16)


def _split_heads(q2, first):
    zero = jnp.zeros_like(q2)
    return jnp.where(first, q2, zero), jnp.where(first, zero, q2)


def _sb_band_tile(qi, q_ref, k_ref, v_ref, o_ref, acc_ref, run_ref, top_ref, first, suffix, tq):
    qs = qi * tq
    q_heads = _split_heads(q_ref[qs:qs + tq, :], first)
    b0 = max(0, qs - tq)
    w = qs + tq - b0
    accs, runs = [], []
    for h in range(2):
        acc, run = _sb_tile(q_heads[h], k_ref[b0:b0 + w, :], v_ref[b0:b0 + w, :],
                            jnp.zeros((tq, LANES), F32), jnp.zeros((tq, 1), F32), True, suffix)
        accs.append(acc)
        runs.append(run)
        if b0 > 0:
            acc_ref[h, qs:qs + tq, :] = acc
            run_ref[h, qs:qs + tq, :] = run
    if b0 > 0:
        top_ref[qi] = jnp.maximum(jnp.max(runs[0]), jnp.max(runs[1]))
    o_ref[qs:qs + tq, :] = jnp.where(first, accs[0], accs[1]).astype(o_ref.dtype)


def _sb_finish(q_ref, k_ref, v_ref, o_ref, acc_ref, run_ref, top_ref, first, suffix, tq):
    def finish(qi, _):
        @pl.when(top_ref[qi] > SB_EXIT)
        def _():
            rows = pl.ds(pl.multiple_of(qi * tq, tq), tq)
            q_heads = _split_heads(q_ref[rows, :], first)

            def unfinished(st):
                j, _, ra, _, rb = st
                return (j >= 0) & (jnp.maximum(jnp.max(ra), jnp.max(rb)) > SB_EXIT)

            def earlier(st):
                j, acc_a, ra, acc_b, rb = st
                keys = pl.ds(pl.multiple_of(j * tq, tq), tq)
                k2 = k_ref[keys, :]
                v2 = v_ref[keys, :]
                acc_a, ra = _sb_tile(q_heads[0], k2, v2, acc_a, ra, False, suffix)
                acc_b, rb = _sb_tile(q_heads[1], k2, v2, acc_b, rb, False, suffix)
                return j - 1, acc_a, ra, acc_b, rb

            st = lax.while_loop(unfinished, earlier, (qi - 2, acc_ref[0, rows, :], run_ref[0, rows, :],
                                                      acc_ref[1, rows, :], run_ref[1, rows, :]))
            o_ref[rows, :] = jnp.where(first, st[1], st[3]).astype(o_ref.dtype)

        return 0

    lax.fori_loop(2, q_ref.shape[0] // tq, finish, 0)


def _mla_tile(qi, q_ref, k_ref, v_ref, o_ref, first, tq):
    rows = slice(qi * tq, (qi + 1) * tq)
    end = (qi + 1) * tq
    v_ones = _with_ones(v_ref[:end, :], first)
    out = [_causal_softmax_pv(_dot_nt(q_ref[rows, h * LANES:(h + 1) * LANES], k_ref[:end, h * LANES:(h + 1) * LANES]),
                              v_ones[h], first) for h in range(2)]
    o_ref[rows, :] = jnp.where(first, out[0], out[1]).astype(o_ref.dtype)


def _layer0_attn_kernel(qs_ref, ks_ref, vs_ref, qm_ref, km_ref, vm_ref, osb_ref, omla_ref, acc_ref, run_ref, top_ref, *,
                        tq_sb, tq_mla):
    seq = qs_ref.shape[0]
    first = _pair_masks()
    suffix = _suffix_matrix(min(SB_GROUP, tq_sb))
    sb_tiles = list(reversed(range(seq // tq_sb)))
    mla_tiles = list(reversed(range(seq // tq_mla)))
    per = -(-len(sb_tiles) // len(mla_tiles))
    while sb_tiles or mla_tiles:
        if mla_tiles:
            _mla_tile(mla_tiles.pop(0), qm_ref, km_ref, vm_ref, omla_ref, first, tq_mla)
        for _ in range(per):
            if sb_tiles:
                _sb_band_tile(sb_tiles.pop(0), qs_ref, ks_ref, vs_ref, osb_ref, acc_ref, run_ref, top_ref, first,
                              suffix, tq_sb)
    _sb_finish(qs_ref, ks_ref, vs_ref, osb_ref, acc_ref, run_ref, top_ref, first, suffix, tq_sb)


def _layer0_attention(qsb, ksb, vsb, qf, kf, vm):
    b, s, _ = qsb.shape
    tq_sb, tq_mla = min(SB_Q_TILE, s), min(Q_TILE, s)
    assert s % tq_sb == 0 and s % tq_mla == 0
    pair = pl.BlockSpec((None, s, LANES), lambda bi, h: (bi, 0, h))
    wide = pl.BlockSpec((None, s, 2 * LANES), lambda bi, h: (bi, 0, h))
    out = jax.ShapeDtypeStruct((b, s, HEADS_W), BF16)
    return pl.pallas_call(
        functools.partial(_layer0_attn_kernel, tq_sb=tq_sb, tq_mla=tq_mla),
        grid=(b, N_HEADS // 2),
        in_specs=[pair, pair, pair, wide, wide, pair],
        out_specs=[pair, pair],
        out_shape=[out, out],
        scratch_shapes=[pltpu.VMEM((2, s, LANES), F32), pltpu.VMEM((2, s, 1), F32),
                        pltpu.SMEM((s // tq_sb,), F32)],
        compiler_params=_cparams("parallel", "parallel"),
        name="sb_mla_attn",
    )(qsb, ksb, vsb, qf, kf, vm)


def _causal_softmax_pv(s, v_ones, first):
    tq = s.shape[0]
    r = lax.broadcasted_iota(jnp.int32, (tq, tq), 0)
    c = lax.broadcasted_iota(jnp.int32, (tq, tq), 1)
    own = jnp.where(c <= r, s[:, -tq:], NEG_BIG)
    s = own if s.shape[1] == tq else jnp.concatenate([s[:, :-tq], own], axis=1)
    p = jnp.exp2(s - jnp.max(s, axis=-1, keepdims=True))
    acc = _dot(p.astype(BF16), v_ones)
    return acc / pltpu.roll(acc, HEAD_DIM, 1)


def _with_ones(v2, first):
    one = jnp.ones_like(v2)
    return jnp.where(first, v2, one), jnp.where(first, one, v2)


def _moba_block_means(k_ref):
    nblk = k_ref.shape[0] // MOBA_BLOCK
    means = [jnp.mean(k_ref[n * MOBA_BLOCK:(n + 1) * MOBA_BLOCK, :].astype(F32), axis=0, keepdims=True)
             for n in range(nblk)]
    km = jnp.concatenate(means + [jnp.zeros((BF16_ROWS - nblk, LANES), F32)], axis=0)
    k1 = km.astype(BF16)
    rem = km - k1.astype(F32)
    k2 = rem.astype(BF16)
    return k1, k2, (rem - k2.astype(F32)).astype(BF16)


def _moba_biased_queries(qi, q_ref, kmeans, first, tq):
    nblk = q_ref.shape[0] // MOBA_BLOCK
    k1, k2, k3 = kmeans
    q2 = q_ref[qi * tq:(qi + 1) * tq, :]
    rowid = lax.broadcasted_iota(jnp.int32, (SUBLANES, tq), 0)
    qblk = qi * (tq // MOBA_BLOCK) + lax.broadcasted_iota(jnp.int32, (SUBLANES, tq), 1) // MOBA_BLOCK
    valid = rowid < qblk
    q_aug = []
    for qh in _split_heads(q2, first):
        gate_t = _dot_nt(k1, qh) + _dot_nt(k2, qh) + _dot_nt(k3, qh)
        g = jnp.where(valid, gate_t[0:SUBLANES, :], -jnp.inf)
        beaten = jnp.zeros((SUBLANES, tq), jnp.int32)
        for m in range(nblk):
            gm = g[m:m + 1, :]
            wins = (gm > g) | ((gm == g) & (rowid > m))
            beaten = beaten + jnp.where(wins, 1, 0)
        usable = (valid & (beaten < MOBA_TOPK)) | (rowid == qblk)
        bias_t = jnp.where(usable, 0.0, NEG_BIG)
        bias_t = jnp.concatenate([bias_t, jnp.zeros((LANES - SUBLANES, tq), F32)], axis=0)
        q_aug.append(jnp.concatenate([qh, bias_t.T.astype(BF16)], axis=1))
    return q_aug


def _moba_tile(qi, q_aug, k_ref, v_ref, o_ref, first, tq):
    rows = slice(qi * tq, (qi + 1) * tq)
    end = (qi + 1) * tq
    blk = lax.broadcasted_iota(jnp.int32, (end, LANES), 0) // MOBA_BLOCK
    onehot = jnp.where(blk == lax.broadcasted_iota(jnp.int32, (end, LANES), 1), 1.0, 0.0).astype(BF16)
    k_aug = jnp.concatenate([k_ref[:end, :], onehot], axis=1)
    v_ones = _with_ones(v_ref[:end, :], first)
    out = [_causal_softmax_pv(_dot_nt(q_aug[h], k_aug), v_ones[h], first) for h in range(2)]
    o_ref[rows, :] = jnp.where(first, out[0], out[1]).astype(o_ref.dtype)


def _s5_chain(j, ub, bmat_ref, cmat_ref, are_ref, aim_ref, x_ref, state_ref, nb, ts):
    cw = S5_WIDTH // (S5_CHANNELS // LANES)
    re_cols = slice(2 * j * cw, (2 * j + 1) * cw)
    im_cols = slice((2 * j + 1) * cw, (2 * j + 2) * cw)
    x_ref[:, 2 * j * cw:(2 * j + 2) * cw] = _dot(ub[:, j * LANES:(j + 1) * LANES], bmat_ref[j])
    ar = jnp.broadcast_to(are_ref[:, j * cw:(j + 1) * cw], (nb, cw))
    ai = jnp.broadcast_to(aim_ref[:, j * cw:(j + 1) * cw], (nb, cw))
    xr = state_ref[:, re_cols]
    xi = state_ref[:, im_cols]
    for t in range(ts):
        step = slice(t * nb, (t + 1) * nb)
        nr = ar * xr - ai * xi + x_ref[step, re_cols]
        ni = ar * xi + ai * xr + x_ref[step, im_cols]
        x_ref[step, re_cols] = nr
        x_ref[step, im_cols] = ni
        xr, xi = nr, ni
    state_ref[:, re_cols] = xr
    state_ref[:, im_cols] = xi
    return _dot(x_ref[:, 2 * j * cw:(2 * j + 2) * cw].astype(BF16), cmat_ref[j])


def _layer1_mixer_kernel(u_ref, bmat_ref, cmat_ref, are_ref, aim_ref, d_ref, wglu_ref, bglu_ref, q_ref, k_ref, v_ref,
                         os5_ref, omoba_ref, x_ref, state_ref, *, tq):
    nb, ts, _ = u_ref.shape
    rows = ts * nb
    seq = q_ref.shape[0]
    first = _pair_masks()

    @pl.when(pl.program_id(0) == 0)
    def _():
        state_ref[...] = jnp.zeros_like(state_ref)

    uf = jnp.swapaxes(u_ref[...].astype(F32), 0, 1).reshape(rows, S5_CHANNELS)
    ub = uf.astype(BF16)
    kmeans = _moba_block_means(k_ref)

    tiles = list(reversed(range(seq // tq)))
    chains = list(range(S5_CHANNELS // LANES))
    ys = []
    while tiles or chains:
        if tiles:
            qi = tiles.pop(0)
            _moba_tile(qi, _moba_biased_queries(qi, q_ref, kmeans, first, tq), k_ref, v_ref, omoba_ref, first, tq)
        if chains:
            ys.append(_s5_chain(chains.pop(0), ub, bmat_ref, cmat_ref, are_ref, aim_ref, x_ref, state_ref, nb, ts))

    y = jnp.concatenate(ys, axis=1) + d_ref[...] * uf
    z = 0.5 * y * (1.0 + jnp.tanh(math.sqrt(2.0 / math.pi) * (y + 0.044715 * (y * y * y))))
    gate = _dot(z.astype(BF16), wglu_ref[...]) + bglu_ref[...]
    out = z * (1.0 / (1.0 + jnp.exp(-gate)))
    os5_ref[...] = jnp.swapaxes(out.reshape(ts, nb, S5_CHANNELS), 0, 1).astype(os5_ref.dtype)


def _layer1_mixer(u, bmat, cmat, are, aim, d, wglu, bglu, q, k, v):
    b, s, _ = u.shape
    n_pairs = N_HEADS // 2
    steps = b * n_pairs
    assert s % steps == 0
    ts = s // steps
    tq = min(Q_TILE, s)
    assert s % tq == 0 and tq % MOBA_BLOCK == 0 and s // MOBA_BLOCK <= SUBLANES
    chunk = pl.BlockSpec((b, ts, S5_CHANNELS), lambda i: (0, i, 0))
    pair = pl.BlockSpec((None, s, LANES), lambda i: (i // n_pairs, 0, i % n_pairs))
    return pl.pallas_call(
        functools.partial(_layer1_mixer_kernel, tq=tq),
        grid=(steps,),
        in_specs=[chunk, _full_spec(bmat.shape), _full_spec(cmat.shape), _full_spec(are.shape),
                  _full_spec(aim.shape), _full_spec(d.shape), _full_spec(wglu.shape), _full_spec(bglu.shape),
                  pair, pair, pair],
        out_specs=[chunk, pair],
        out_shape=[jax.ShapeDtypeStruct((b, s, S5_CHANNELS), BF16), jax.ShapeDtypeStruct((b, s, HEADS_W), BF16)],
        scratch_shapes=[pltpu.VMEM((ts * b, 2 * S5_WIDTH), F32), pltpu.VMEM((b, 2 * S5_WIDTH), F32)],
        compiler_params=_cparams("arbitrary"),
        name="s5_moba",
    )(u, bmat, cmat, are, aim, d, wglu, bglu, q, k, v)


def _tail_kernel(a_ref, b_ref, x_ref, wo_ref, g1_ref, beta1_ref, wg_ref, wu_ref, wd_ref, g2_ref, beta2_ref, o_ref, *,
                 layer):
    tm = x_ref.shape[0]
    strip = tm // TAIL_STRIPS
    rows = [slice(r0, r0 + strip) for r0 in range(0, tm, strip)]
    wa = a_ref.shape[1]
    ln = slice(layer, layer + 1)

    def mix(r):
        return _dot(a_ref[r, :], wo_ref[:wa, :]) + _dot(b_ref[r, :], wo_ref[wa:, :])

    def norm1(r, m):
        return _layer_norm_rows(DN_ALPHA * x_ref[r, :] + m, g1_ref[ln, :], beta1_ref[ln, :])

    def ffn(x1):
        xb = x1.astype(BF16)
        gate = _dot(xb, wg_ref[...])
        up = _dot(xb, wu_ref[...])
        h = (gate * (1.0 / (1.0 + jnp.exp(-gate))) * up).astype(BF16)
        return _dot(h, wd_ref[...])

    def norm2(r, x1, y):
        o_ref[r, :] = _layer_norm_rows(DN_ALPHA * x1 + y, g2_ref[ln, :], beta2_ref[ln, :])

    n = len(rows)
    m = [None] * n
    x1 = [None] * n
    y = [None] * n
    m[0] = mix(rows[0])
    for k in range(n):
        if k + 1 < n:
            m[k + 1] = mix(rows[k + 1])
        x1[k] = norm1(rows[k], m[k])
        if k > 0:
            norm2(rows[k - 1], x1[k - 1], y[k - 1])
        y[k] = ffn(x1[k])
    norm2(rows[n - 1], x1[n - 1], y[n - 1])


def _layer_tail(a, b, x2, w_out, g1, beta1, wg, wu, wd, g2, beta2, layer):
    t = x2.shape[0]
    tm = min(TAIL_ROW_TILE, t)
    row = lambda c: pl.BlockSpec((tm, c), lambda i: (i, 0))
    whole = lambda arr: pl.BlockSpec(arr.shape, lambda i: (0, 0), pipeline_mode=pl.Buffered(1))
    of_layer = lambda arr, l: pl.BlockSpec((None,) + arr.shape[1:], lambda i: (l, 0, 0),
                                           pipeline_mode=pl.Buffered(1))
    return pl.pallas_call(
        functools.partial(_tail_kernel, layer=layer),
        grid=(t // tm,),
        in_specs=[row(a.shape[1]), row(b.shape[1]), row(D_MODEL), of_layer(w_out, 0), whole(g1), whole(beta1),
                  of_layer(wg, layer), of_layer(wu, layer), of_layer(wd, layer), whole(g2), whole(beta2)],
        out_specs=row(D_MODEL),
        out_shape=jax.ShapeDtypeStruct((t, D_MODEL), F32),
        compiler_params=_cparams("parallel"),
        name="layer_tail",
    )(a, b, x2, w_out, g1, beta1, wg, wu, wd, g2, beta2)


def _proj_cd_kernel(x_ref, w_ref, u_ref, q_ref, k_ref, v_ref):
    p = _dot(x_ref[...].astype(BF16), w_ref[...])
    c0 = S5_CHANNELS
    u_ref[...] = p[:, 0:c0].astype(BF16)
    q_ref[...] = (p[:, c0:c0 + HEADS_W] * (HEAD_DIM ** -0.5 * LOG2_E)).astype(BF16)
    k_ref[...] = p[:, c0 + HEADS_W:c0 + 2 * HEADS_W].astype(BF16)
    v_ref[...] = p[:, c0 + 2 * HEADS_W:c0 + 3 * HEADS_W].astype(BF16)


def _proj_cd(x2, w):
    t = x2.shape[0]
    tm = min(ROW_TILE, t)
    row = lambda c: pl.BlockSpec((tm, c), lambda i: (i, 0))
    return pl.pallas_call(
        _proj_cd_kernel,
        grid=(t // tm,),
        in_specs=[row(D_MODEL), _full_spec(w.shape)],
        out_specs=[row(HEADS_W)] * 4,
        out_shape=[jax.ShapeDtypeStruct((t, HEADS_W), BF16)] * 4,
        compiler_params=_cparams("parallel"),
        name="proj_cd",
    )(x2, w)


def _rotate_half_cols(w):
    half = w.shape[-1] // 2
    return jnp.concatenate([-w[..., half:], w[..., :half]], axis=-1)


def _layer0_params(w_in, q_norm, w_uq, kv_norm, w_ukv, seq):
    d = w_in.shape[0]
    w_rope = w_in[:, AB_LATENT_END:AB_LATENT_END + MLA_ROPE]
    pad_l = jnp.zeros((d, MLA_NOPE), F32)
    pad_r = jnp.zeros((d, LANES - MLA_NOPE - MLA_ROPE), F32)
    w1 = jnp.concatenate([w_in[:, :AB_LATENT_END], pad_l, w_rope, pad_r, pad_l, _rotate_half_cols(w_rope), pad_r], axis=1)

    wq3 = w_uq.reshape(MLA_RANK, N_HEADS, MLA_NOPE + MLA_ROPE)
    nope, rope = wq3[..., :MLA_NOPE], wq3[..., MLA_NOPE:]
    z_nope = jnp.zeros_like(nope)
    z_pad = jnp.zeros((MLA_RANK, N_HEADS, LANES - MLA_NOPE - MLA_ROPE), F32)
    wq = jnp.concatenate([nope, rope, z_pad], axis=-1).reshape(MLA_RANK, N_HEADS * LANES)
    wqr = jnp.concatenate([z_nope, _rotate_half_cols(rope), z_pad], axis=-1).reshape(MLA_RANK, N_HEADS * LANES)

    wkv3 = w_ukv.reshape(MLA_RANK, N_HEADS, 2 * HEAD_DIM)
    wkn = jnp.concatenate([wkv3[..., :MLA_NOPE], jnp.zeros((MLA_RANK, N_HEADS, LANES - MLA_NOPE), F32)],
                          axis=-1).reshape(MLA_RANK, N_HEADS * LANES)
    wv = wkv3[..., MLA_NOPE:].reshape(MLA_RANK, HEADS_W)

    half = MLA_ROPE // 2
    freqs = ROPE_BASE ** (-jnp.arange(half, dtype=F32) / half)
    ang = jnp.arange(seq, dtype=F32)[:, None] * freqs
    cos = jnp.concatenate([jnp.cos(ang)] * 2, axis=1)
    sin = jnp.concatenate([jnp.sin(ang)] * 2, axis=1)
    ones = jnp.ones((seq, MLA_NOPE), F32)
    zl = jnp.zeros((seq, MLA_NOPE), F32)
    zr = jnp.zeros((seq, LANES - MLA_NOPE - MLA_ROPE), F32)
    scale = (MLA_NOPE + MLA_ROPE) ** -0.5 * LOG2_E
    cosq = jnp.concatenate([ones, cos, zr], axis=1) * scale
    sinq = jnp.concatenate([zl, sin, zr], axis=1) * scale
    cosk = jnp.concatenate([zl, cos, zr], axis=1)
    sink = jnp.concatenate([zl, sin, zr], axis=1)
    return (w1.astype(BF16), q_norm.reshape(1, -1), kv_norm.reshape(1, -1), wq.astype(BF16), wqr.astype(BF16),
            wkn.astype(BF16), wv.astype(BF16), cosq, sinq, cosk, sink)


def _s5_params(lam_re, lam_im, log_dt, b_re, b_im, c_re, c_im):
    dt = jnp.exp(log_dt)[:, None]
    mag = jnp.exp(lam_re * dt)
    ab_re, ab_im = mag * jnp.cos(lam_im * dt), mag * jnp.sin(lam_im * dt)
    den = lam_re * lam_re + lam_im * lam_im
    nr, ni = ab_re - 1.0, ab_im
    f_re, f_im = (nr * lam_re + ni * lam_im) / den, (ni * lam_re - nr * lam_im) / den
    bb_re = f_re[..., None] * b_re - f_im[..., None] * b_im
    bb_im = f_re[..., None] * b_im + f_im[..., None] * b_re

    gpc = LANES // S5_GROUP
    n_chunks = S5_GROUPS // gpc
    eye = jnp.eye(gpc, dtype=F32)

    def in_blocks(bb):
        t = bb.reshape(n_chunks, gpc, S5_STATE, S5_GROUP)
        return jnp.einsum('cgph,gk->cghkp', t, eye).reshape(n_chunks, gpc * S5_GROUP, gpc * S5_STATE)

    def out_blocks(cc):
        t = cc.reshape(n_chunks, gpc, S5_GROUP, S5_STATE)
        return jnp.einsum('cghp,gk->cgpkh', t, eye).reshape(n_chunks, gpc * S5_STATE, gpc * S5_GROUP)

    bmat = jnp.concatenate([in_blocks(bb_re), in_blocks(bb_im)], axis=2)
    cmat = jnp.concatenate([out_blocks(c_re), -out_blocks(c_im)], axis=1)
    a_re = ab_re.reshape(1, S5_WIDTH)
    a_im = ab_im.reshape(1, S5_WIDTH)
    return bmat.astype(BF16), cmat.astype(BF16), a_re, a_im


def kernel(x, ab_w_in, ab_q_norm, ab_w_uq, ab_kv_norm, ab_w_ukv, ab_w_out, cd_w_in, s5_lambda_re, s5_lambda_im,
           s5_log_dt, s5_b_re, s5_b_im, s5_c_re, s5_c_im, s5_d, s5_w_glu, s5_b_glu, cd_w_out, ln1_g, ln1_b, ln2_g,
           ln2_b, ffn_w_gate, ffn_w_up, ffn_w_down):
    b, s, d = x.shape
    t = b * s
    x2 = x.reshape(t, d)
    vec = lambda a: a.reshape(1, -1)

    p0 = _layer0_params(ab_w_in[0], ab_q_norm[0], ab_w_uq[0], ab_kv_norm[0], ab_w_ukv[0], s)
    qsb, ksb, vsb, qf, kf, vm = _proj_ab(x2, *p0, seq=s)
    sh = lambda a: a.reshape(b, s, a.shape[-1])
    o_sb, o_mla = _layer0_attention(sh(qsb), sh(ksb), sh(vsb), sh(qf), sh(kf), sh(vm))
    o_sb, o_mla = o_sb.reshape(t, HEADS_W), o_mla.reshape(t, HEADS_W)
    wg, wu, wd = ffn_w_gate.astype(BF16), ffn_w_up.astype(BF16), ffn_w_down.astype(BF16)
    x2 = _layer_tail(o_sb, o_mla, x2, ab_w_out.astype(BF16), ln1_g, ln1_b, wg, wu, wd, ln2_g, ln2_b, layer=0)

    u, q, k, v = _proj_cd(x2, cd_w_in[0].astype(BF16))
    bmat, cmat, a_re, a_im = _s5_params(s5_lambda_re[0], s5_lambda_im[0], s5_log_dt[0], s5_b_re[0], s5_b_im[0],
                                        s5_c_re[0], s5_c_im[0])
    o_s5, o_moba = _layer1_mixer(sh(u), bmat, cmat, a_re, a_im, vec(s5_d[0]), s5_w_glu[0].astype(BF16),
                                 vec(s5_b_glu[0]), sh(q), sh(k), sh(v))
    x2 = _layer_tail(o_s5.reshape(t, S5_CHANNELS), o_moba.reshape(t, HEADS_W), x2, cd_w_out.astype(BF16),
                     ln1_g, ln1_b, wg, wu, wd, ln2_g, ln2_b, layer=1)
    return x2.reshape(b, s, d)
```
